```python
import math
import jax, jax.numpy as jnp
from jax import lax
import numpy as np

D_MODEL = 1024
BATCH = 8
SEQ = 4096
DEPTH = 2

CTX_LEN = 256
GRID_W = 64
N_EVEN = (DEPTH + 1) // 2
N_ODD = DEPTH // 2
A_WIDTH = D_MODEL // 2
A_DK = 128
A_DV = 128
A_HEADS = A_WIDTH // A_DV
A_QKV = A_HEADS * (2 * A_DK + A_DV)
SHORT_CONV = 5
GDN_CHUNK = 64
B_WIDTH = D_MODEL - A_WIDTH
B_HD = 128
B_HEADS = B_WIDTH // B_HD
B_KV = B_HEADS // 2
Q_BLOCK = 128
ROPE_THETA = 10000.0
CONV_WIDTH = D_MODEL
CONV_K = 31
NORM_EPS = 1e-6
EVEN_SPLITS = (A_QKV, A_WIDTH, 2 * A_HEADS, 2 * A_HEADS, B_HEADS * B_HD, B_KV * B_HD, B_KV * B_HD, B_WIDTH)
EVEN_IN = sum(EVEN_SPLITS)

kernel_name = "hybrid_gdn_gqa_conformer_prefix_dit"

F32 = jnp.float32


def split_cols(a, sizes):
    return jnp.split(a, np.cumsum(sizes)[:-1].tolist(), axis=-1)


def rms_norm(x, g):
    xf = x.astype(F32)
    y = xf * lax.rsqrt(jnp.mean(xf * xf, axis=-1, keepdims=True) + NORM_EPS)
    return (y * g.astype(F32)).astype(x.dtype)


def layer_norm(x, g, b):
    xf = x.astype(F32)
    mu = jnp.mean(xf, axis=-1, keepdims=True)
    var = jnp.mean(jnp.square(xf - mu), axis=-1, keepdims=True)
    return ((xf - mu) * lax.rsqrt(var + 1e-5) * g.astype(F32) + b.astype(F32)).astype(x.dtype)


def l2_normalize(x):
    return x * lax.rsqrt(jnp.sum(x * x, axis=-1, keepdims=True) + NORM_EPS)


def depthwise_conv(x, w):
    k = w.shape[0]
    return lax.conv_general_dilated(x, w[:, None, :].astype(x.dtype), window_strides=(1,),
                                    padding=[(k // 2, k // 2)], dimension_numbers=('NWC', 'WIO', 'NWC'),
                                    feature_group_count=x.shape[-1])


def adaln(cond, w, b):
    m = (jax.nn.silu(cond) @ w + b)[:, None, :]
    return jnp.split(m, 3, axis=-1)


def axial_rope_tables(n_tokens, head_dim):
    rows = n_tokens // GRID_W
    row = jnp.repeat(jnp.arange(rows, dtype=F32), GRID_W)
    col = jnp.tile(jnp.arange(GRID_W, dtype=F32), rows)
    axis_dim = head_dim // 2
    inv_freq = ROPE_THETA ** (-jnp.arange(0, axis_dim, 2, dtype=F32) / axis_dim)
    ang_r = row[:, None] * inv_freq
    ang_c = col[:, None] * inv_freq
    return (jnp.cos(ang_r), jnp.sin(ang_r), jnp.cos(ang_c), jnp.sin(ang_c))


def rope_rotate(x, cos, sin):
    x1, x2 = jnp.split(x, 2, axis=-1)
    cs, sn = cos[None, :, None, :], sin[None, :, None, :]
    return jnp.concatenate([x1 * cs - x2 * sn, x2 * cs + x1 * sn], axis=-1)


def apply_axial_rope(x, tabs):
    cr, sr, cc, sc = tabs
    half = x.shape[-1] // 2
    xf = x.astype(F32)
    y = jnp.concatenate([rope_rotate(xf[..., :half], cr, sr), rope_rotate(xf[..., half:], cc, sc)], axis=-1)
    return y.astype(x.dtype)


def chunk_gated_delta(q, k, v, g, beta, s0, with_output):
    bsz, nh, t, dk = q.shape
    dv = v.shape[-1]
    n = t // GDN_CHUNK

    def blk(a):
        return a.reshape(bsz, nh, n, GDN_CHUNK, *a.shape[3:])

    q, k, v, g, beta = blk(q), blk(k), blk(v), blk(g), blk(beta)
    g = jnp.cumsum(g, axis=-1)
    idx = jnp.arange(GDN_CHUNK)
    incl = idx[:, None] >= idx[None, :]
    strict = idx[:, None] > idx[None, :]
    decay = jnp.exp(jnp.where(incl, g[..., :, None] - g[..., None, :], -jnp.inf))
    kb = k * beta[..., None]
    lower = jnp.where(strict, jnp.einsum('bhnid,bhnjd->bhnij', kb, k) * decay, 0.0)
    tmat = jnp.eye(GDN_CHUNK, dtype=q.dtype) + lower
    u = lax.linalg.triangular_solve(tmat, v * beta[..., None], left_side=True, lower=True)
    w = lax.linalg.triangular_solve(tmat, kb * jnp.exp(g)[..., None], left_side=True, lower=True)
    k_dec = k * jnp.exp(g[..., -1:] - g)[..., None]
    g_tot = jnp.exp(g[..., -1])
    if with_output:
        a_intra = jnp.einsum('bhnid,bhnjd->bhnij', q, k) * decay
        q_dec = q * jnp.exp(g)[..., None]
        xs = tuple(jnp.moveaxis(a, 2, 0) for a in (w, u, k_dec, g_tot, q_dec, a_intra))
    else:
        xs = tuple(jnp.moveaxis(a, 2, 0) for a in (w, u, k_dec, g_tot))

    def step(state, inp):
        w_c, u_c, kd_c, gt_c = inp[:4]
        v_new = u_c - jnp.einsum('bhid,bhde->bhie', w_c, state)
        new_state = state * gt_c[..., None, None] + jnp.einsum('bhid,bhie->bhde', kd_c, v_new)
        if with_output:
            qd_c, a_c = inp[4:]
            o_c = jnp.einsum('bhid,bhde->bhie', qd_c, state) + jnp.einsum('bhij,bhje->bhie', a_c, v_new)
            return new_state, o_c
        return new_state, None

    s_final, o = lax.scan(step, s0, xs)
    if with_output:
        o = jnp.moveaxis(o, 0, 2).reshape(bsz, nh, t, dv)
    return o, s_final


def bidir_gdn(q, k, v, g, beta, s0_f, s0_b, with_output):
    o_f, s_f = chunk_gated_delta(q, k, v, g[0], beta[0], s0_f, with_output)
    o_b, s_b = chunk_gated_delta(jnp.flip(q, 2), jnp.flip(k, 2), jnp.flip(v, 2), jnp.flip(g[1], -1),
                                 jnp.flip(beta[1], -1), s0_b, with_output)
    o = o_f + jnp.flip(o_b, 2) if with_output else None
    return o, s_f, s_b


def gdn_prep(qkv, a, b, conv_w, a_log, dt_bias):
    bsz, t, _ = qkv.shape
    u = jax.nn.silu(depthwise_conv(qkv, conv_w)).astype(F32)
    q, k, v = split_cols(u, [A_HEADS * A_DK, A_HEADS * A_DK, A_HEADS * A_DV])

    def heads(y, d):
        return y.reshape(bsz, t, A_HEADS, d).transpose(0, 2, 1, 3)

    q = l2_normalize(heads(q, A_DK)) * (A_DK ** -0.5)
    k = l2_normalize(heads(k, A_DK))
    v = heads(v, A_DV)

    def dirs(y):
        return y.astype(F32).reshape(bsz, t, 2, A_HEADS).transpose(2, 0, 3, 1)

    beta = jax.nn.sigmoid(dirs(b))
    g = -jnp.exp(a_log.astype(F32))[:, None, :, None] * jax.nn.softplus(
        dirs(a) + dt_bias.astype(F32)[:, None, :, None])
    return q, k, v, g, beta


def gdn_out(o, z, norm_g):
    bsz, nh, t, dv = o.shape
    o = rms_norm(o.transpose(0, 2, 1, 3), norm_g)
    y = o * jax.nn.silu(z.astype(F32).reshape(bsz, t, nh, dv))
    return y.reshape(bsz, t, nh * dv).astype(z.dtype)


def gqa_prep(q, k, v, qn_g, kn_g):
    bsz, t, _ = q.shape
    q = rms_norm(q.reshape(bsz, t, B_HEADS, B_HD), qn_g)
    k = rms_norm(k.reshape(bsz, t, B_KV, B_HD), kn_g)
    v = v.reshape(bsz, t, B_KV, B_HD)
    return q, k, v


def grouped_attend(q, k, v):
    bsz, nq = q.shape[:2]
    qg = q.reshape(bsz, nq, B_KV, B_HEADS // B_KV, B_HD)
    s = jnp.einsum('bqhgd,bnhd->bhgqn', qg, k).astype(F32) * (B_HD ** -0.5)
    p = jax.nn.softmax(s, axis=-1).astype(v.dtype)
    o = jnp.einsum('bhgqn,bnhd->bqhgd', p, v)
    return o.reshape(bsz, nq, B_HEADS * B_HD)


def gdn_gqa_mixer(hl, hc, w_in, conv_w, a_log, dt_bias, gdn_g, qn_g, kn_g, w_out, rope, ctx_out):
    bsz, t, _ = hl.shape
    pl = hl @ w_in
    pc = hc @ w_in
    qkv_l, za_l, a_l, b_l, qb_l, kb_l, vb_l, zb_l = split_cols(pl, EVEN_SPLITS)
    qkv_c, za_c, a_c, b_c, qb_c, kb_c, vb_c, zb_c = split_cols(pc, EVEN_SPLITS)
    qa_c, ka_c, va_c, ga_c, ba_c = gdn_prep(qkv_c, a_c, b_c, conv_w, a_log, dt_bias)
    qa_l, ka_l, va_l, ga_l, ba_l = gdn_prep(qkv_l, a_l, b_l, conv_w, a_log, dt_bias)
    zero = jnp.zeros((bsz, A_HEADS, A_DK, A_DV), F32)
    oa_c, s_f, s_b = bidir_gdn(qa_c, ka_c, va_c, ga_c, ba_c, zero, zero, ctx_out)
    oa_l, _, _ = bidir_gdn(qa_l, ka_l, va_l, ga_l, ba_l, s_f, s_b, True)
    ya_l = gdn_out(oa_l, za_l, gdn_g)
    q_l, k_l, v_l = gqa_prep(qb_l, kb_l, vb_l, qn_g, kn_g)
    q_l, k_l = apply_axial_rope(q_l, rope), apply_axial_rope(k_l, rope)
    q_c, k_c, v_c = gqa_prep(qb_c, kb_c, vb_c, qn_g, kn_g)
    k_all = jnp.concatenate([k_l, k_c], axis=1)
    v_all = jnp.concatenate([v_l, v_c], axis=1)
    nblk = t // Q_BLOCK
    q_blocks = q_l.reshape(bsz, nblk, Q_BLOCK, B_HEADS, B_HD).swapaxes(0, 1)
    ob = lax.map(lambda qq: grouped_attend(qq, k_all, v_all), q_blocks)
    ob = ob.swapaxes(0, 1).reshape(bsz, t, B_WIDTH)
    yb_l = ob * jax.nn.silu(zb_l)
    out_l = jnp.concatenate([ya_l, yb_l.astype(ya_l.dtype)], axis=-1) @ w_out
    if not ctx_out:
        return out_l, None
    ya_c = gdn_out(oa_c, za_c, gdn_g)
    yb_c = grouped_attend(q_c, k_c, v_c) * jax.nn.silu(zb_c)
    out_c = jnp.concatenate([ya_c, yb_c.astype(ya_c.dtype)], axis=-1) @ w_out
    return out_l, out_c


def conformer_mixer(h, w_in, b_in, dw_w, dw_b, ln_g, ln_b, w_out, b_out):
    p = h @ w_in + b_in
    a, glu_g, z = jnp.split(p, 3, axis=-1)
    u = a * jax.nn.sigmoid(glu_g)
    u = depthwise_conv(u, dw_w) + dw_b
    u = layer_norm(u, ln_g, ln_b)
    u = jax.nn.silu(u) * jax.nn.silu(z)
    return u @ w_out + b_out


def setup_inputs(seed: int = 0) -> dict:
    key = jax.random.key(seed)
    ks = jax.random.split(key, 24)

    def nrm(k, shape, scale):
        return jax.random.normal(k, shape, F32) * scale

    dt = jnp.exp(jax.random.uniform(ks[11], (N_EVEN, 2, A_HEADS), F32, math.log(1e-3), math.log(1e-1)))
    return {
        "x": nrm(ks[0], (BATCH, SEQ, D_MODEL), 1.0),
        "c": nrm(ks[1], (BATCH, D_MODEL), 1.0),
        "ctx": nrm(ks[2], (BATCH, CTX_LEN, D_MODEL), 1.0),
        "c_ctx": nrm(ks[3], (D_MODEL,), 1.0),
        "ada_w": nrm(ks[4], (DEPTH, D_MODEL, 3 * D_MODEL), 0.5 * D_MODEL ** -0.5),
        "ada_b": nrm(ks[5], (DEPTH, 3 * D_MODEL), 0.02),
        "pre_norm_g": 1.0 + nrm(ks[6], (DEPTH, D_MODEL), 0.02),
        "post_norm_g": 1.0 + nrm(ks[7], (DEPTH, D_MODEL), 0.02),
        "ev_w_in": nrm(ks[8], (N_EVEN, D_MODEL, EVEN_IN), D_MODEL ** -0.5),
        "ev_short_conv_w": nrm(ks[9], (N_EVEN, SHORT_CONV, A_QKV), SHORT_CONV ** -0.5),
        "ev_a_log": jnp.log(jax.random.uniform(ks[10], (N_EVEN, 2, A_HEADS), F32, 1.0, 16.0)),
        "ev_dt_bias": dt + jnp.log(-jnp.expm1(-dt)),
        "ev_gdn_norm_g": 1.0 + nrm(ks[12], (N_EVEN, A_DV), 0.02),
        "ev_q_norm_g": 1.0 + nrm(ks[13], (N_EVEN, B_HD), 0.02),
        "ev_k_norm_g": 1.0 + nrm(ks[14], (N_EVEN, B_HD), 0.02),
        "ev_w_out": nrm(ks[15], (N_EVEN, A_WIDTH + B_WIDTH, D_MODEL), (A_WIDTH + B_WIDTH) ** -0.5),
        "od_w_in": nrm(ks[16], (N_ODD, D_MODEL, 3 * CONV_WIDTH), D_MODEL ** -0.5),
        "od_b_in": nrm(ks[17], (N_ODD, 3 * CONV_WIDTH), 0.02),
        "od_dw_w": nrm(ks[18], (N_ODD, CONV_K, CONV_WIDTH), CONV_K ** -0.5),
        "od_dw_b": nrm(ks[19], (N_ODD, CONV_WIDTH), 0.02),
        "od_ln_g": 1.0 + nrm(ks[20], (N_ODD, CONV_WIDTH), 0.02),
        "od_ln_b": nrm(ks[21], (N_ODD, CONV_WIDTH), 0.02),
        "od_w_out": nrm(ks[22], (N_ODD, CONV_WIDTH, D_MODEL), CONV_WIDTH ** -0.5),
        "od_b_out": nrm(ks[23], (N_ODD, D_MODEL), 0.02),
    }


def reference(x, c, ctx, c_ctx, ada_w, ada_b, pre_norm_g, post_norm_g, ev_w_in, ev_short_conv_w, ev_a_log,
              ev_dt_bias, ev_gdn_norm_g, ev_q_norm_g, ev_k_norm_g, ev_w_out, od_w_in, od_b_in, od_dw_w, od_dw_b,
              od_ln_g, od_ln_b, od_w_out, od_b_out):
    rope = axial_rope_tables(x.shape[1], B_HD)
    xl, xc = x, ctx
    for layer in range(DEPTH):
        even = layer % 2 == 0
        i = layer // 2
        ctx_out = any(j % 2 == 0 for j in range(layer + 1, DEPTH))
        shift_l, scale_l, gate_l = adaln(c, ada_w[layer], ada_b[layer])
        hl = rms_norm(xl, pre_norm_g[layer]) * (1.0 + scale_l) + shift_l
        if even or ctx_out:
            shift_c, scale_c, gate_c = adaln(c_ctx[None, :], ada_w[layer], ada_b[layer])
            hc = rms_norm(xc, pre_norm_g[layer]) * (1.0 + scale_c) + shift_c
        if even:
            out_l, out_c = gdn_gqa_mixer(hl, hc, ev_w_in[i], ev_short_conv_w[i], ev_a_log[i], ev_dt_bias[i],
                                         ev_gdn_norm_g[i], ev_q_norm_g[i], ev_k_norm_g[i], ev_w_out[i],
                                         rope, ctx_out)
        else:
            out_l = conformer_mixer(hl, od_w_in[i], od_b_in[i], od_dw_w[i], od_dw_b[i], od_ln_g[i], od_ln_b[i],
                                    od_w_out[i], od_b_out[i])
            out_c = conformer_mixer(hc, od_w_in[i], od_b_in[i], od_dw_w[i], od_dw_b[i], od_ln_g[i], od_ln_b[i],
                                    od_w_out[i], od_b_out[i]) if ctx_out else None
        xl = xl + gate_l * rms_norm(out_l, post_norm_g[layer])
        if ctx_out:
            xc = xc + gate_c * rms_norm(out_c, post_norm_g[layer])
    return xl
```

```python
import functools
import math

import numpy as np
import jax
import jax.numpy as jnp
from jax import lax
from jax.experimental import pallas as pl
from jax.experimental.pallas import tpu as pltpu

F32 = jnp.float32
BF16 = jnp.bfloat16
HIGHEST = lax.Precision.HIGHEST

GRID_W = 64
A_DK = 128
A_DV = 128
SHORT_CONV = 5
GDN_CHUNK = 64
B_HD = 128
ROPE_THETA = 10000.0
CONV_K = 31
NORM_EPS = 1e-6
LN_EPS = 1e-5
NEG_BIG = -1e30

VMEM_LIMIT = 56 * 1024 * 1024


def _sigmoid(x):
    return 1.0 / (1.0 + jnp.exp(-x))


def _silu(x):
    return x * _sigmoid(x)


def _softplus(x):
    return jnp.maximum(x, 0.0) + jnp.log(1.0 + jnp.exp(-jnp.abs(x)))


def _cparams(sem):
    return pltpu.CompilerParams(dimension_semantics=sem, vmem_limit_bytes=VMEM_LIMIT)


def _ada_kernel(c_ref, w_ref, b_ref, o_ref):
    s = _silu(c_ref[...])
    o_ref[0] = jnp.dot(s, w_ref[0], preferred_element_type=F32, precision=HIGHEST) + b_ref[0]


def _ada(cond, ada_w, ada_b):
    depth, d, d3 = ada_w.shape
    rows = cond.shape[0]
    nt = d3 // d
    return pl.pallas_call(
        _ada_kernel,
        out_shape=jax.ShapeDtypeStruct((depth, rows, d3), F32),
        grid=(depth, nt),
        in_specs=[
            pl.BlockSpec((rows, d), lambda l, j: (0, 0)),
            pl.BlockSpec((1, d, d), lambda l, j: (l, 0, j)),
            pl.BlockSpec((1, 1, d), lambda l, j: (l, 0, j)),
        ],
        out_specs=pl.BlockSpec((1, rows, d), lambda l, j: (l, 0, j)),
        compiler_params=_cparams(("parallel", "parallel")),
        name="ada",
    )(cond, ada_w, ada_b.reshape(depth, 1, d3))


def _inproj0_kernel(x_ref, ctx_ref, sh_ref, sc_ref, g_ref, w_ref, cs_ref, sn_ref, qg_ref, kg_ref, alog_ref, dtb_ref,
                    qkv_o, za_o, qb_o, kb_o, vb_o, zb_o, gb_o, h_scr, *, n_lat_tiles, secs):
    i = pl.program_id(1)
    mult = g_ref[...] * (1.0 + sc_ref[0])
    shift = sh_ref[0]

    def modulate(src):
        xf = src[0]
        rs = lax.rsqrt(jnp.mean(xf * xf, axis=-1, keepdims=True) + NORM_EPS)
        h_scr[...] = (xf * rs * mult + shift).astype(BF16)

    @pl.when(i < n_lat_tiles)
    def _():
        modulate(x_ref)

    @pl.when(i >= n_lat_tiles)
    def _():
        modulate(ctx_ref)

    h = h_scr[...]

    def proj(name):
        c0, c1 = secs[name]
        return jnp.dot(h, w_ref[:, c0:c1], preferred_element_type=F32)

    qkv_o[0] = proj("qkv").astype(BF16)
    za_o[0] = _silu(proj("za")).astype(BF16)
    zb_o[0] = _silu(proj("zb")).astype(BF16)
    vb_o[0] = proj("vb").astype(BF16)

    cs = cs_ref[...]
    sn = sn_ref[...]

    def norm_rope(y, g, out_scale):
        ms = jnp.mean(y * y, axis=-1, keepdims=True)
        yn = y * lax.rsqrt(ms + NORM_EPS) * g
        return (yn * cs + pltpu.roll(yn, B_HD // 2, 1) * sn) * out_scale

    yq = proj("qb")
    qg = qg_ref[...]
    for hh in range(yq.shape[1] // B_HD):
        sl = slice(hh * B_HD, (hh + 1) * B_HD)
        qb_o[0, :, sl] = norm_rope(yq[:, sl], qg, B_HD ** -0.5).astype(BF16)
    yk = proj("kb")
    kg = kg_ref[...]
    for hh in range(yk.shape[1] // B_HD):
        sl = slice(hh * B_HD, (hh + 1) * B_HD)
        kb_o[0, :, sl] = norm_rope(yk[:, sl], kg, 1.0).astype(BF16)

    ab = proj("ab")
    nab = gb_o.shape[2]
    gval = -jnp.exp(alog_ref[...]) * _softplus(ab + dtb_ref[...])
    bval = _sigmoid(ab)
    lane = lax.broadcasted_iota(jnp.int32, ab.shape, 1)
    gb_o[0] = jnp.where(lane < nab // 2, gval, bval)[:, :nab]


def _inproj0(x, ctx, mod0, pre_g, w2, secs, cs_tab, sn_tab, qg, kg, alog, dtb, tm):
    bsz, t, d = x.shape
    nctx = ctx.shape[1]
    tall = t + nctx
    assert t % tm == 0 and nctx == tm
    n_lat = t // tm
    ntile = n_lat + 1
    nab = 16

    def widths(name):
        return secs[name][1] - secs[name][0]

    def row_spec(c):
        return pl.BlockSpec((1, tm, c), lambda b, i: (b, i, 0))

    mod_rows = mod0.shape[0]
    mod3 = mod0.reshape(mod_rows, 1, 3 * d)
    ctx_row = bsz
    outs = [
        jax.ShapeDtypeStruct((bsz, tall, widths("qkv")), BF16),
        jax.ShapeDtypeStruct((bsz, tall, widths("za")), BF16),
        jax.ShapeDtypeStruct((bsz, tall, widths("qb")), BF16),
        jax.ShapeDtypeStruct((bsz, tall, widths("kb")), BF16),
        jax.ShapeDtypeStruct((bsz, tall, widths("vb")), BF16),
        jax.ShapeDtypeStruct((bsz, tall, widths("zb")), BF16),
        jax.ShapeDtypeStruct((bsz, tall, nab), F32),
    ]
    kern = functools.partial(_inproj0_kernel, n_lat_tiles=n_lat, secs=secs)
    return pl.pallas_call(
        kern,
        out_shape=outs,
        grid=(bsz, ntile),
        in_specs=[
            pl.BlockSpec((1, tm, d), lambda b, i: (b, jnp.minimum(i, n_lat - 1), 0)),
            pl.BlockSpec((1, tm, d), lambda b, i: (b, 0, 0)),
            pl.BlockSpec((1, 1, d), lambda b, i: (jnp.where(i < n_lat, b, ctx_row), 0, 0)),
            pl.BlockSpec((1, 1, d), lambda b, i: (jnp.where(i < n_lat, b, ctx_row), 0, 1)),
            pl.BlockSpec((1, d), lambda b, i: (0, 0)),
            pl.BlockSpec(w2.shape, lambda b, i: (0, 0)),
            pl.BlockSpec((tm, B_HD), lambda b, i: (i, 0)),
            pl.BlockSpec((tm, B_HD), lambda b, i: (i, 0)),
            pl.BlockSpec((1, B_HD), lambda b, i: (0, 0)),
            pl.BlockSpec((1, B_HD), lambda b, i: (0, 0)),
            pl.BlockSpec((1, 128), lambda b, i: (0, 0)),
            pl.BlockSpec((1, 128), lambda b, i: (0, 0)),
        ],
        out_specs=[row_spec(widths("qkv")), row_spec(widths("za")), row_spec(widths("qb")), row_spec(widths("kb")),
                   row_spec(widths("vb")), row_spec(widths("zb")), row_spec(nab)],
        scratch_shapes=[pltpu.VMEM((tm, d), BF16)],
        compiler_params=_cparams(("parallel", "arbitrary")),
        name="inproj0",
    )(x, ctx, mod3, mod3, pre_g.reshape(1, d), w2, cs_tab, sn_tab, qg, kg, alog, dtb)


def _gdnconv_kernel(cur_ref, prev_ref, next_ref, w_ref, o_ref, buf, *, n_lat_tiles, halo, n_qk_blocks):
    i = pl.program_id(1)
    tm = cur_ref.shape[1]
    prev_ok = jnp.logical_and(i > 0, i < n_lat_tiles)
    next_ok = i < n_lat_tiles - 1
    buf[0:halo, :] = jnp.where(prev_ok, prev_ref[0].astype(F32), 0.0)
    buf[halo:halo + tm, :] = cur_ref[0].astype(F32)
    buf[halo + tm:2 * halo + tm, :] = jnp.where(next_ok, next_ref[0].astype(F32), 0.0)
    pad = SHORT_CONV // 2
    ncb = cur_ref.shape[2] // 128
    for cb in range(ncb):
        cs = slice(cb * 128, (cb + 1) * 128)
        acc = None
        for j in range(SHORT_CONV):
            term = buf[halo - pad + j:halo - pad + j + tm, cs] * w_ref[j:j + 1, cs]
            acc = term if acc is None else acc + term
        u = _silu(acc)
        if cb < n_qk_blocks:
            u = u * lax.rsqrt(jnp.sum(u * u, axis=-1, keepdims=True) + NORM_EPS)
            if cb < n_qk_blocks // 2:
                u = u * (A_DK ** -0.5)
        o_ref[0, :, cs] = u.astype(BF16)


def _gdnconv(qkv_raw, conv_w, tm, n_lat, n_heads):
    bsz, tall, c = qkv_raw.shape
    halo = 16
    nhb = tm // halo
    last_blk = tall // halo - 1
    kern = functools.partial(_gdnconv_kernel, n_lat_tiles=n_lat, halo=halo, n_qk_blocks=2 * n_heads)
    return pl.pallas_call(
        kern,
        out_shape=jax.ShapeDtypeStruct((bsz, tall, c), BF16),
        grid=(bsz, tall // tm),
        in_specs=[
            pl.BlockSpec((1, tm, c), lambda b, i: (b, i, 0)),
            pl.BlockSpec((1, halo, c), lambda b, i: (b, jnp.maximum(i * nhb - 1, 0), 0)),
            pl.BlockSpec((1, halo, c), lambda b, i: (b, jnp.minimum((i + 1) * nhb, last_blk), 0)),
            pl.BlockSpec(conv_w.shape, lambda b, i: (0, 0)),
        ],
        out_specs=pl.BlockSpec((1, tm, c), lambda b, i: (b, i, 0)),
        scratch_shapes=[pltpu.VMEM((tm + 2 * halo, c), F32)],
        compiler_params=_cparams(("parallel", "parallel")),
        name="gdnconv",
    )(qkv_raw, qkv_raw, qkv_raw, conv_w)


def _gdn_kernel(q_ref, k_ref, v_ref, gr_ref, sz_ref, ng_ref, o_ref,
                wq_scr, u_scr, kd_scr, am_scr, gt_scr, cr_scr, s_scr, oacc, *, n_lat_chunks, n_chunks):
    C = GDN_CHUNK
    ri = lax.broadcasted_iota(jnp.int32, (C, C), 0)
    ci = lax.broadcasted_iota(jnp.int32, (C, C), 1)
    incl = (ri >= ci, ri <= ci)
    strict = (ri > ci, ri < ci)
    eye = (ri == ci).astype(F32)
    tri_row = ((ri <= ci).astype(F32), (ri >= ci).astype(F32))
    tri_col = (incl[0].astype(F32), incl[1].astype(F32))

    for d in range(2):
        cr_scr[d] = jnp.dot(gr_ref[0, 0, d], tri_row[d], preferred_element_type=F32, precision=HIGHEST)

    def prep(n, carry):
        r0 = pl.multiple_of(n * C, C)
        rows = pl.ds(r0, C)
        kb = k_ref[0, rows, :]
        qb = q_ref[0, rows, :]
        k32 = kb.astype(F32)
        q32 = qb.astype(F32)
        v32 = v_ref[0, rows, :].astype(F32)
        nt = (((1,), (1,)), ((), ()))
        kk = lax.dot_general(kb, kb, nt, preferred_element_type=F32)
        qk = lax.dot_general(qb, kb, nt, preferred_element_type=F32)
        for d in range(2):
            g_row = gr_ref[0, 0, d, pl.ds(n, 1), :]
            beta_row = gr_ref[0, 0, 2 + d, pl.ds(n, 1), :]
            c_row = cr_scr[d, pl.ds(n, 1), :]
            c_col = jnp.sum(tri_col[d] * g_row, axis=-1, keepdims=True)
            beta_col = jnp.sum(eye * beta_row, axis=-1, keepdims=True)
            tot = jnp.sum(g_row, axis=-1, keepdims=True)
            decay = jnp.exp(jnp.where(incl[d], c_col - c_row, NEG_BIG))
            x = jnp.where(strict[d], kk * decay, 0.0) * (-beta_col)
            xb = x.astype(BF16)
            y = x
            p = jnp.dot(xb, xb, preferred_element_type=F32)
            nsq = int(math.log2(C)) - 1
            for m in range(nsq):
                pb = p.astype(BF16)
                yb = y.astype(BF16)
                if m < nsq - 1:
                    r = jnp.dot(pb, jnp.concatenate([pb, yb], axis=1), preferred_element_type=F32)
                    y = y + p + r[:, C:]
                    p = r[:, :C]
                else:
                    y = y + p + jnp.dot(pb, yb, preferred_element_type=F32)
            e_col = jnp.exp(c_col)
            rhs = jnp.concatenate([v32 * beta_col, k32 * (beta_col * e_col)], axis=1)
            uw = rhs + jnp.dot(y.astype(BF16), rhs.astype(BF16), preferred_element_type=F32)
            u_scr[d, rows, :] = uw[:, :A_DV]
            wq_scr[d, n, 0:C, :] = uw[:, A_DV:].astype(BF16)
            wq_scr[d, n, C:2 * C, :] = (q32 * e_col).astype(BF16)
            kd_scr[d, rows, :] = (k32 * jnp.exp(tot - c_col)).astype(BF16)
            am_scr[d, n] = jnp.where(incl[d], qk * decay, 0.0).astype(BF16)
            gt_scr[d, n] = jnp.broadcast_to(jnp.exp(tot), (8, 128))
        return carry

    lax.fori_loop(0, n_chunks, prep, 0)

    s_scr[...] = jnp.zeros_like(s_scr)

    def step(d, n, with_out):
        r0 = pl.multiple_of(n * C, C)
        rows = pl.ds(r0, C)
        s = s_scr[d]
        pm = jnp.dot(wq_scr[d, n], s.astype(BF16), preferred_element_type=F32)
        v_new = u_scr[d, rows, :] - pm[:C]
        vb = v_new.astype(BF16)
        ds = lax.dot_general(kd_scr[d, rows, :], vb, (((0,), (0,)), ((), ())), preferred_element_type=F32)
        s_scr[d] = s * gt_scr[d, n][0:1, :] + ds
        if with_out:
            return pm[C:] + jnp.dot(am_scr[d, n], vb, preferred_element_type=F32)
        return None

    n_ctx_chunks = n_chunks - n_lat_chunks

    def ctx_body(sidx, carry):
        step(0, n_lat_chunks + sidx, False)
        step(1, n_chunks - 1 - sidx, False)
        return carry

    lax.fori_loop(0, n_ctx_chunks, ctx_body, 0)

    def lat_body(sidx, carry):
        nf = sidx
        nb = n_lat_chunks - 1 - sidx
        of = step(0, nf, True)
        ob = step(1, nb, True)
        rf = pl.ds(pl.multiple_of(nf * C, C), C)
        rb = pl.ds(pl.multiple_of(nb * C, C), C)

        @pl.when(nf < nb)
        def _():
            oacc[rf, :] = of
            oacc[rb, :] = ob

        @pl.when(nf > nb)
        def _():
            oacc[rf, :] = oacc[rf, :] + of
            oacc[rb, :] = oacc[rb, :] + ob

        return carry

    lax.fori_loop(0, n_lat_chunks, lat_body, 0)

    ng = ng_ref[...]
    blk = 512

    def fin(j, carry):
        rows = pl.ds(pl.multiple_of(j * blk, blk), blk)
        o = oacc[rows, :]
        on = o * lax.rsqrt(jnp.mean(o * o, axis=-1, keepdims=True) + NORM_EPS) * ng
        o_ref[0, rows, :] = (on * sz_ref[0, rows, :].astype(F32)).astype(BF16)
        return carry

    lax.fori_loop(0, (n_lat_chunks * C) // blk, fin, 0)


def _gdn(qkvn, grow, sza, norm_g, n_heads, t):
    bsz, tall, _ = qkvn.shape
    C = GDN_CHUNK
    n_chunks = tall // C
    n_lat_chunks = t // C
    ncp = grow.shape[3]
    kern = functools.partial(_gdn_kernel, n_lat_chunks=n_lat_chunks, n_chunks=n_chunks)
    return pl.pallas_call(
        kern,
        out_shape=jax.ShapeDtypeStruct((bsz, t, n_heads * A_DV), BF16),
        grid=(bsz, n_heads),
        in_specs=[
            pl.BlockSpec((1, tall, A_DK), lambda b, h: (b, 0, h)),
            pl.BlockSpec((1, tall, A_DK), lambda b, h: (b, 0, n_heads + h)),
            pl.BlockSpec((1, tall, A_DV), lambda b, h: (b, 0, 2 * n_heads + h)),
            pl.BlockSpec((1, 1, 4, ncp, C), lambda b, h: (b, h, 0, 0, 0)),
            pl.BlockSpec((1, t, A_DV), lambda b, h: (b, 0, h)),
            pl.BlockSpec((1, A_DV), lambda b, h: (0, 0)),
        ],
        out_specs=pl.BlockSpec((1, t, A_DV), lambda b, h: (b, 0, h)),
        scratch_shapes=[
            pltpu.VMEM((2, n_chunks, 2 * C, A_DK), BF16),
            pltpu.VMEM((2, tall, A_DV), F32),
            pltpu.VMEM((2, tall, A_DK), BF16),
            pltpu.VMEM((2, n_chunks, C, C), BF16),
            pltpu.VMEM((2, n_chunks, 8, 128), F32),
            pltpu.VMEM((2, ncp, C), F32),
            pltpu.VMEM((2, A_DK, A_DV), F32),
            pltpu.VMEM((t, A_DV), F32),
        ],
        compiler_params=_cparams(("parallel", "parallel")),
        name="gdn",
    )(qkvn, qkvn, qkvn, grow, sza, norm_g)


def _attn_kernel(q_ref, k_ref, v_ref, sz_ref, o_ref):
    k = k_ref[0]
    v = v_ref[0]
    nt = (((1,), (1,)), ((), ()))
    for hh in range(q_ref.shape[2] // B_HD):
        sl = slice(hh * B_HD, (hh + 1) * B_HD)
        s = lax.dot_general(q_ref[0, :, sl], k, nt, preferred_element_type=F32)
        m = jnp.max(s, axis=-1, keepdims=True)
        p = jnp.exp(s - m)
        l = jnp.sum(p, axis=-1, keepdims=True)
        o = jnp.dot(p.astype(BF16), v, preferred_element_type=F32)
        o_ref[0, :, sl] = (o * (1.0 / l) * sz_ref[0, :, sl].astype(F32)).astype(BF16)


def _attn(qb, kb, vb, szb, t, tq):
    bsz, tall, qw = qb.shape
    n_kv = kb.shape[2] // B_HD
    gw = qw // n_kv
    return pl.pallas_call(
        _attn_kernel,
        out_shape=jax.ShapeDtypeStruct((bsz, t, qw), BF16),
        grid=(bsz, n_kv, t // tq),
        in_specs=[
            pl.BlockSpec((1, tq, gw), lambda b, g, i: (b, i, g)),
            pl.BlockSpec((1, tall, B_HD), lambda b, g, i: (b, 0, g)),
            pl.BlockSpec((1, tall, B_HD), lambda b, g, i: (b, 0, g)),
            pl.BlockSpec((1, tq, gw), lambda b, g, i: (b, i, g)),
        ],
        out_specs=pl.BlockSpec((1, tq, gw), lambda b, g, i: (b, i, g)),
        compiler_params=_cparams(("parallel", "parallel", "arbitrary")),
        name="attn",
    )(qb, kb, vb, szb)


def _mid_kernel(ya_ref, yb_ref, x_ref, wa_ref, wb_ref, pg_ref, gate_ref, sh_ref, sc_ref, g1_ref, w1_ref, b1_ref,
                xl_o, u_o, sz_o, h_scr):
    d = x_ref.shape[2]
    out = (jnp.dot(ya_ref[0], wa_ref[...], preferred_element_type=F32)
           + jnp.dot(yb_ref[0], wb_ref[...], preferred_element_type=F32))
    on = out * lax.rsqrt(jnp.mean(out * out, axis=-1, keepdims=True) + NORM_EPS) * pg_ref[...]
    xl = x_ref[0] + gate_ref[0] * on
    xl_o[0] = xl
    rs = lax.rsqrt(jnp.mean(xl * xl, axis=-1, keepdims=True) + NORM_EPS)
    h_scr[...] = (xl * rs * (g1_ref[...] * (1.0 + sc_ref[0])) + sh_ref[0]).astype(BF16)
    h = h_scr[...]
    a = jnp.dot(h, w1_ref[:, 0:d], preferred_element_type=F32) + b1_ref[:, 0:d]
    gl = jnp.dot(h, w1_ref[:, d:2 * d], preferred_element_type=F32) + b1_ref[:, d:2 * d]
    u_o[0] = (a * _sigmoid(gl)).astype(BF16)
    z = jnp.dot(h, w1_ref[:, 2 * d:3 * d], preferred_element_type=F32) + b1_ref[:, 2 * d:3 * d]
    sz_o[0] = _silu(z).astype(BF16)


def _mid(ya, yb, x, wa, wb, post_g, mod0, mod1, pre_g1, w1, b1, tm):
    bsz, t, d = x.shape
    aw = ya.shape[2]
    bw = yb.shape[2]
    rows = mod0.shape[0]
    m0 = mod0.reshape(rows, 1, 3 * d)
    m1 = mod1.reshape(rows, 1, 3 * d)
    tok = lambda c: pl.BlockSpec((1, tm, c), lambda b, i: (b, i, 0))
    const2 = lambda a: pl.BlockSpec(a.shape, lambda b, i: (0, 0))
    return pl.pallas_call(
        _mid_kernel,
        out_shape=[jax.ShapeDtypeStruct((bsz, t, d), F32),
                   jax.ShapeDtypeStruct((bsz, t, d), BF16),
                   jax.ShapeDtypeStruct((bsz, t, d), BF16)],
        grid=(bsz, t // tm),
        in_specs=[
            tok(aw), tok(bw), tok(d),
            const2(wa), const2(wb),
            pl.BlockSpec((1, d), lambda b, i: (0, 0)),
            pl.BlockSpec((1, 1, d), lambda b, i: (b, 0, 2)),
            pl.BlockSpec((1, 1, d), lambda b, i: (b, 0, 0)),
            pl.BlockSpec((1, 1, d), lambda b, i: (b, 0, 1)),
            pl.BlockSpec((1, d), lambda b, i: (0, 0)),
            const2(w1),
            pl.BlockSpec((1, 3 * d), lambda b, i: (0, 0)),
        ],
        out_specs=[tok(d), tok(d), tok(d)],
        scratch_shapes=[pltpu.VMEM((tm, d), BF16)],
        compiler_params=_cparams(("parallel", "parallel")),
        name="mid",
    )(ya, yb, x, wa, wb, post_g.reshape(1, d), m0, m1, m1, pre_g1.reshape(1, d), w1, b1.reshape(1, 3 * d))


def _tail_kernel(cur_ref, prev_ref, next_ref, sz_ref, xl_ref, dw_ref, dwb_ref, lng_ref, lnb_ref, wo_ref, bo_ref,
                 pg_ref, gate_ref, o_ref, buf, act, *, halo):
    i = pl.program_id(1)
    nt = pl.num_programs(1)
    tm = cur_ref.shape[1]
    d = cur_ref.shape[2]
    buf[0:halo, :] = jnp.where(i > 0, prev_ref[0].astype(F32), 0.0)
    buf[halo:halo + tm, :] = cur_ref[0].astype(F32)
    buf[halo + tm:2 * halo + tm, :] = jnp.where(i < nt - 1, next_ref[0].astype(F32), 0.0)
    pad = CONV_K // 2
    cw = 256
    for cb in range(d // cw):
        cs = slice(cb * cw, (cb + 1) * cw)
        acc = None
        for j in range(CONV_K):
            term = buf[halo - pad + j:halo - pad + j + tm, cs] * dw_ref[j:j + 1, cs]
            acc = term if acc is None else acc + term
        act[:, cs] = acc + dwb_ref[:, cs]
    u = act[...]
    mu = jnp.mean(u, axis=-1, keepdims=True)
    uc = u - mu
    var = jnp.mean(uc * uc, axis=-1, keepdims=True)
    un = uc * lax.rsqrt(var + LN_EPS) * lng_ref[...] + lnb_ref[...]
    hact = (_silu(un) * sz_ref[0].astype(F32)).astype(BF16)
    out = jnp.dot(hact, wo_ref[...], preferred_element_type=F32) + bo_ref[...]
    on = out * lax.rsqrt(jnp.mean(out * out, axis=-1, keepdims=True) + NORM_EPS) * pg_ref[...]
    o_ref[0] = xl_ref[0] + gate_ref[0] * on


def _tail(u, sz, xl, dw_w, dw_b, ln_g, ln_b, w_out, b_out, post_g, mod1, tm):
    bsz, t, d = xl.shape
    halo = 16
    nhb = tm // halo
    last_blk = t // halo - 1
    rows = mod1.shape[0]
    m1 = mod1.reshape(rows, 1, 3 * d)
    tok = lambda c: pl.BlockSpec((1, tm, c), lambda b, i: (b, i, 0))
    vec = lambda: pl.BlockSpec((1, d), lambda b, i: (0, 0))
    kern = functools.partial(_tail_kernel, halo=halo)
    return pl.pallas_call(
        kern,
        out_shape=jax.ShapeDtypeStruct((bsz, t, d), F32),
        grid=(bsz, t // tm),
        in_specs=[
            tok(d),
            pl.BlockSpec((1, halo, d), lambda b, i: (b, jnp.maximum(i * nhb - 1, 0), 0)),
            pl.BlockSpec((1, halo, d), lambda b, i: (b, jnp.minimum((i + 1) * nhb, last_blk), 0)),
            tok(d), tok(d),
            pl.BlockSpec(dw_w.shape, lambda b, i: (0, 0)),
            vec(), vec(), vec(),
            pl.BlockSpec(w_out.shape, lambda b, i: (0, 0)),
            vec(), vec(),
            pl.BlockSpec((1, 1, d), lambda b, i: (b, 0, 2)),
        ],
        out_specs=tok(d),
        scratch_shapes=[pltpu.VMEM((tm + 2 * halo, d), F32), pltpu.VMEM((tm, d), F32)],
        compiler_params=_cparams(("parallel", "parallel")),
        name="tail",
    )(u, u, u, sz, xl, dw_w, dw_b.reshape(1, d), ln_g.reshape(1, d), ln_b.reshape(1, d), w_out,
      b_out.reshape(1, d), post_g.reshape(1, d), m1)


def _rope_tables(t, nctx):
    rows = t // GRID_W
    row = jnp.repeat(jnp.arange(rows, dtype=F32), GRID_W)
    col = jnp.tile(jnp.arange(GRID_W, dtype=F32), rows)
    axis_dim = B_HD // 2
    inv_freq = ROPE_THETA ** (-jnp.arange(0, axis_dim, 2, dtype=F32) / axis_dim)
    ang = jnp.concatenate([row[:, None] * inv_freq, col[:, None] * inv_freq], axis=1)
    cs = jnp.concatenate([jnp.cos(ang), jnp.cos(ang)], axis=1)
    sn = jnp.concatenate([-jnp.sin(ang), jnp.sin(ang)], axis=1)
    cs = jnp.concatenate([cs, jnp.ones((nctx, B_HD), F32)], axis=0)
    sn = jnp.concatenate([sn, jnp.zeros((nctx, B_HD), F32)], axis=0)
    return cs, sn


def _head_perm():
    q = B_HD // 4
    return np.concatenate([np.arange(0, q), np.arange(2 * q, 3 * q), np.arange(q, 2 * q), np.arange(3 * q, 4 * q)])


def kernel(x, c, ctx, c_ctx, ada_w, ada_b, pre_norm_g, post_norm_g, ev_w_in, ev_short_conv_w, ev_a_log,
           ev_dt_bias, ev_gdn_norm_g, ev_q_norm_g, ev_k_norm_g, ev_w_out, od_w_in, od_b_in, od_dw_w, od_dw_b,
           od_ln_g, od_ln_b, od_w_out, od_b_out):
    bsz, t, d = x.shape
    nctx = ctx.shape[1]
    a_width = d // 2
    a_heads = a_width // A_DV
    a_qkv = a_heads * (2 * A_DK + A_DV)
    b_width = d - a_width
    b_heads = b_width // B_HD
    b_kv = b_heads // 2
    assert ada_w.shape[0] == 2 and ev_w_in.shape[0] == 1 and od_w_in.shape[0] == 1

    mod_rows = ((bsz + 1 + 7) // 8) * 8
    cond = jnp.zeros((mod_rows, d), F32).at[:bsz].set(c).at[bsz].set(c_ctx)
    mod = _ada(cond, ada_w, ada_b)

    w_in = ev_w_in[0]
    splits = np.cumsum([0, a_qkv, a_width, 2 * a_heads, 2 * a_heads, b_heads * B_HD, b_kv * B_HD, b_kv * B_HD,
                        b_width])
    w_qkv, w_za, w_a, w_b, w_qb, w_kb, w_vb, w_zb = [w_in[:, splits[j]:splits[j + 1]] for j in range(8)]
    perm = _head_perm()
    perm_q = np.concatenate([h * B_HD + perm for h in range(b_heads)])
    perm_k = np.concatenate([h * B_HD + perm for h in range(b_kv)])
    ab_pad = 128 - 4 * a_heads
    pieces = [("qkv", w_qkv), ("za", w_za), ("qb", w_qb[:, perm_q]), ("kb", w_kb[:, perm_k]), ("vb", w_vb),
              ("zb", w_zb), ("ab", jnp.concatenate([w_a, w_b, jnp.zeros((d, ab_pad), F32)], axis=1))]
    secs, off = {}, 0
    for name, wpart in pieces:
        secs[name] = (off, off + wpart.shape[1])
        off += wpart.shape[1]
    w2 = jnp.concatenate([p[1] for p in pieces], axis=1).astype(BF16)
    cs_tab, sn_tab = _rope_tables(t, nctx)
    qg = ev_q_norm_g[0][perm].reshape(1, B_HD)
    kg = ev_k_norm_g[0][perm].reshape(1, B_HD)
    alog = jnp.zeros((1, 128), F32).at[0, :2 * a_heads].set(ev_a_log[0].reshape(-1))
    dtb = jnp.zeros((1, 128), F32).at[0, :2 * a_heads].set(ev_dt_bias[0].reshape(-1))

    tm0 = nctx
    qkv_raw, sza, qb, kb, vb, szb, gb = _inproj0(x, ctx, mod[0], pre_norm_g[0], w2, secs, cs_tab, sn_tab, qg, kg,
                                                 alog, dtb, tm0)

    qkvn = _gdnconv(qkv_raw, ev_short_conv_w[0], tm0, t // tm0, a_heads)
    tall = t + nctx
    n_chunks = tall // GDN_CHUNK
    ncp = ((n_chunks + 7) // 8) * 8
    grow = gb.reshape(bsz, n_chunks, GDN_CHUNK, 4, a_heads).transpose(0, 4, 3, 1, 2)
    grow = jnp.pad(grow, ((0, 0), (0, 0), (0, 0), (0, ncp - n_chunks), (0, 0)))
    ya = _gdn(qkvn, grow, sza, ev_gdn_norm_g[0].reshape(1, A_DV), a_heads, t)

    yb = _attn(qb, kb, vb, szb, t, 256)

    w_out0 = ev_w_out[0].astype(BF16)
    xl1, u1, sz1 = _mid(ya, yb, x, w_out0[:a_width], w_out0[a_width:], post_norm_g[0], mod[0], mod[1],
                        pre_norm_g[1], od_w_in[0].astype(BF16), od_b_in[0], 256)
    return _tail(u1, sz1, xl1, od_dw_w[0], od_dw_b[0], od_ln_g[0], od_ln_b[0], od_w_out[0].astype(BF16),
                 od_b_out[0], post_norm_g[1], mod[1], 256)
```

```python
import functools
import math

import numpy as np
import jax
import jax.numpy as jnp
from jax import lax
from jax.experimental import pallas as pl
from jax.experimental.pallas import tpu as pltpu

F32 = jnp.float32
BF16 = jnp.bfloat16
HIGHEST = lax.Precision.HIGHEST

GRID_W = 64
A_DK = 128
A_DV = 128
SHORT_CONV = 5
GDN_CHUNK = 64
B_HD = 128
ROPE_THETA = 10000.0
CONV_K = 31
NORM_EPS = 1e-6
LN_EPS = 1e-5
NEG_BIG = -1e30

VMEM_LIMIT = 56 * 1024 * 1024


def _sigmoid(x):
    return 1.0 / (1.0 + jnp.exp(-x))


def _silu(x):
    return x * _sigmoid(x)


def _softplus(x):
    return jnp.maximum(x, 0.0) + jnp.log(1.0 + jnp.exp(-jnp.abs(x)))


def _cparams(sem):
    return pltpu.CompilerParams(dimension_semantics=sem, vmem_limit_bytes=VMEM_LIMIT)


def _ada_kernel(c_ref, w_ref, b_ref, o_ref):
    s = _silu(c_ref[...])
    o_ref[0] = jnp.dot(s, w_ref[0], preferred_element_type=F32, precision=HIGHEST) + b_ref[0]


def _ada(cond, ada_w, ada_b):
    depth, d, d3 = ada_w.shape
    rows = cond.shape[0]
    nt = d3 // d
    return pl.pallas_call(
        _ada_kernel,
        out_shape=jax.ShapeDtypeStruct((depth, rows, d3), F32),
        grid=(depth, nt),
        in_specs=[
            pl.BlockSpec((rows, d), lambda l, j: (0, 0)),
            pl.BlockSpec((1, d, d), lambda l, j: (l, 0, j)),
            pl.BlockSpec((1, 1, d), lambda l, j: (l, 0, j)),
        ],
        out_specs=pl.BlockSpec((1, rows, d), lambda l, j: (l, 0, j)),
        compiler_params=_cparams(("parallel", "parallel")),
        name="ada",
    )(cond, ada_w, ada_b.reshape(depth, 1, d3))


def _inproj0_kernel(x_ref, ctx_ref, sh_ref, sc_ref, g_ref, w_ref, cs_ref, sn_ref, qg_ref, kg_ref, alog_ref, dtb_ref,
                    qkv_o, za_o, qb_o, kb_o, vb_o, zb_o, gb_o, h_scr, *, n_lat_tiles, secs):
    i = pl.program_id(1)
    mult = g_ref[...] * (1.0 + sc_ref[0])
    shift = sh_ref[0]

    def modulate(src):
        xf = src[0]
        rs = lax.rsqrt(jnp.mean(xf * xf, axis=-1, keepdims=True) + NORM_EPS)
        h_scr[...] = (xf * rs * mult + shift).astype(BF16)

    @pl.when(i < n_lat_tiles)
    def _():
        modulate(x_ref)

    @pl.when(i >= n_lat_tiles)
    def _():
        modulate(ctx_ref)

    h = h_scr[...]

    def proj(name):
        c0, c1 = secs[name]
        return jnp.dot(h, w_ref[:, c0:c1], preferred_element_type=F32)

    qkv_o[0] = proj("qkv").astype(BF16)
    za_o[0] = _silu(proj("za")).astype(BF16)
    zb_o[0] = _silu(proj("zb")).astype(BF16)
    vb_o[0] = proj("vb").astype(BF16)

    cs = cs_ref[...]
    sn = sn_ref[...]

    def norm_rope(y, g, out_scale):
        ms = jnp.mean(y * y, axis=-1, keepdims=True)
        yn = y * lax.rsqrt(ms + NORM_EPS) * g
        return (yn * cs + pltpu.roll(yn, B_HD // 2, 1) * sn) * out_scale

    yq = proj("qb")
    qg = qg_ref[...]
    for hh in range(yq.shape[1] // B_HD):
        sl = slice(hh * B_HD, (hh + 1) * B_HD)
        qb_o[0, :, sl] = norm_rope(yq[:, sl], qg, B_HD ** -0.5).astype(BF16)
    yk = proj("kb")
    kg = kg_ref[...]
    for hh in range(yk.shape[1] // B_HD):
        sl = slice(hh * B_HD, (hh + 1) * B_HD)
        kb_o[0, :, sl] = norm_rope(yk[:, sl], kg, 1.0).astype(BF16)

    ab = proj("ab")
    nab = gb_o.shape[2]
    gval = -jnp.exp(alog_ref[...]) * _softplus(ab + dtb_ref[...])
    bval = _sigmoid(ab)
    lane = lax.broadcasted_iota(jnp.int32, ab.shape, 1)
    gb_o[0] = jnp.where(lane < nab // 2, gval, bval)[:, :nab]


def _inproj0(x, ctx, mod0, pre_g, w2, secs, cs_tab, sn_tab, qg, kg, alog, dtb, tm):
    bsz, t, d = x.shape
    nctx = ctx.shape[1]
    tall = t + nctx
    assert t % tm == 0 and nctx == tm
    n_lat = t // tm
    ntile = n_lat + 1
    nab = 16

    def widths(name):
        return secs[name][1] - secs[name][0]

    def row_spec(c):
        return pl.BlockSpec((1, tm, c), lambda b, i: (b, i, 0))

    mod_rows = mod0.shape[0]
    mod3 = mod0.reshape(mod_rows, 1, 3 * d)
    ctx_row = bsz
    outs = [
        jax.ShapeDtypeStruct((bsz, tall, widths("qkv")), BF16),
        jax.ShapeDtypeStruct((bsz, tall, widths("za")), BF16),
        jax.ShapeDtypeStruct((bsz, tall, widths("qb")), BF16),
        jax.ShapeDtypeStruct((bsz, tall, widths("kb")), BF16),
        jax.ShapeDtypeStruct((bsz, tall, widths("vb")), BF16),
        jax.ShapeDtypeStruct((bsz, tall, widths("zb")), BF16),
        jax.ShapeDtypeStruct((bsz, tall, nab), F32),
    ]
    kern = functools.partial(_inproj0_kernel, n_lat_tiles=n_lat, secs=secs)
    return pl.pallas_call(
        kern,
        out_shape=outs,
        grid=(bsz, ntile),
        in_specs=[
            pl.BlockSpec((1, tm, d), lambda b, i: (b, jnp.minimum(i, n_lat - 1), 0)),
            pl.BlockSpec((1, tm, d), lambda b, i: (b, 0, 0)),
            pl.BlockSpec((1, 1, d), lambda b, i: (jnp.where(i < n_lat, b, ctx_row), 0, 0)),
            pl.BlockSpec((1, 1, d), lambda b, i: (jnp.where(i < n_lat, b, ctx_row), 0, 1)),
            pl.BlockSpec((1, d), lambda b, i: (0, 0)),
            pl.BlockSpec(w2.shape, lambda b, i: (0, 0)),
            pl.BlockSpec((tm, B_HD), lambda b, i: (i, 0)),
            pl.BlockSpec((tm, B_HD), lambda b, i: (i, 0)),
            pl.BlockSpec((1, B_HD), lambda b, i: (0, 0)),
            pl.BlockSpec((1, B_HD), lambda b, i: (0, 0)),
            pl.BlockSpec((1, 128), lambda b, i: (0, 0)),
            pl.BlockSpec((1, 128), lambda b, i: (0, 0)),
        ],
        out_specs=[row_spec(widths("qkv")), row_spec(widths("za")), row_spec(widths("qb")), row_spec(widths("kb")),
                   row_spec(widths("vb")), row_spec(widths("zb")), row_spec(nab)],
        scratch_shapes=[pltpu.VMEM((tm, d), BF16)],
        compiler_params=_cparams(("parallel", "arbitrary")),
        name="inproj0",
    )(x, ctx, mod3, mod3, pre_g.reshape(1, d), w2, cs_tab, sn_tab, qg, kg, alog, dtb)


def _gdnconv_kernel(cur_ref, prev_ref, next_ref, w_ref, o_ref, buf, *, n_lat_tiles, halo, n_qk_blocks):
    i = pl.program_id(1)
    tm = cur_ref.shape[1]
    prev_ok = jnp.logical_and(i > 0, i < n_lat_tiles)
    next_ok = i < n_lat_tiles - 1
    buf[0:halo, :] = jnp.where(prev_ok, prev_ref[0].astype(F32), 0.0)
    buf[halo:halo + tm, :] = cur_ref[0].astype(F32)
    buf[halo + tm:2 * halo + tm, :] = jnp.where(next_ok, next_ref[0].astype(F32), 0.0)
    pad = SHORT_CONV // 2
    ncb = cur_ref.shape[2] // 128
    for cb in range(ncb):
        cs = slice(cb * 128, (cb + 1) * 128)
        acc = None
        for j in range(SHORT_CONV):
            term = buf[halo - pad + j:halo - pad + j + tm, cs] * w_ref[j:j + 1, cs]
            acc = term if acc is None else acc + term
        u = _silu(acc)
        if cb < n_qk_blocks:
            u = u * lax.rsqrt(jnp.sum(u * u, axis=-1, keepdims=True) + NORM_EPS)
            if cb < n_qk_blocks // 2:
                u = u * (A_DK ** -0.5)
        o_ref[0, :, cs] = u.astype(BF16)


def _gdnconv(qkv_raw, conv_w, tm, n_lat, n_heads):
    bsz, tall, c = qkv_raw.shape
    halo = 16
    nhb = tm // halo
    last_blk = tall // halo - 1
    kern = functools.partial(_gdnconv_kernel, n_lat_tiles=n_lat, halo=halo, n_qk_blocks=2 * n_heads)
    return pl.pallas_call(
        kern,
        out_shape=jax.ShapeDtypeStruct((bsz, tall, c), BF16),
        grid=(bsz, tall // tm),
        in_specs=[
            pl.BlockSpec((1, tm, c), lambda b, i: (b, i, 0)),
            pl.BlockSpec((1, halo, c), lambda b, i: (b, jnp.maximum(i * nhb - 1, 0), 0)),
            pl.BlockSpec((1, halo, c), lambda b, i: (b, jnp.minimum((i + 1) * nhb, last_blk), 0)),
            pl.BlockSpec(conv_w.shape, lambda b, i: (0, 0)),
        ],
        out_specs=pl.BlockSpec((1, tm, c), lambda b, i: (b, i, 0)),
        scratch_shapes=[pltpu.VMEM((tm + 2 * halo, c), F32)],
        compiler_params=_cparams(("parallel", "parallel")),
        name="gdnconv",
    )(qkv_raw, qkv_raw, qkv_raw, conv_w)


def _gdn_kernel(q_ref, k_ref, v_ref, gr_ref, sz_ref, ng_ref, o_ref,
                wq_scr, u_scr, kd_scr, am_scr, gt_scr, cr_scr, s_scr, oacc, *, n_lat_chunks, n_chunks, group):
    C = GDN_CHUNK
    ri = lax.broadcasted_iota(jnp.int32, (C, C), 0)
    ci = lax.broadcasted_iota(jnp.int32, (C, C), 1)
    incl = (ri >= ci, ri <= ci)
    strict = (ri > ci, ri < ci)
    eye = (ri == ci).astype(F32)
    tri_row = ((ri <= ci).astype(F32), (ri >= ci).astype(F32))
    tri_col = (incl[0].astype(F32), incl[1].astype(F32))

    for d in range(2):
        cr_scr[d] = jnp.dot(gr_ref[0, 0, d], tri_row[d], preferred_element_type=F32, precision=HIGHEST)

    nt = (((1,), (1,)), ((), ()))
    nsq = int(math.log2(C)) - 1

    def mm(a, b):
        return jnp.dot(a, b, preferred_element_type=F32)

    def prep(gi, carry):
        ch = []
        for i in range(group):
            n = gi * group + i
            rows = pl.ds(pl.multiple_of(n * C, C), C)
            kb = k_ref[0, rows, :]
            qb = q_ref[0, rows, :]
            k32 = kb.astype(F32)
            q32 = qb.astype(F32)
            v32 = v_ref[0, rows, :].astype(F32)
            kk = lax.dot_general(kb, kb, nt, preferred_element_type=F32)
            qk = lax.dot_general(qb, kb, nt, preferred_element_type=F32)
            for d in range(2):
                g_row = gr_ref[0, 0, d, pl.ds(n, 1), :]
                beta_row = gr_ref[0, 0, 2 + d, pl.ds(n, 1), :]
                c_row = cr_scr[d, pl.ds(n, 1), :]
                c_col = jnp.sum(tri_col[d] * g_row, axis=-1, keepdims=True)
                beta_col = jnp.sum(eye * beta_row, axis=-1, keepdims=True)
                tot = jnp.sum(g_row, axis=-1, keepdims=True)
                decay = jnp.exp(jnp.where(incl[d], c_col - c_row, NEG_BIG))
                x = jnp.where(strict[d], kk * decay, 0.0) * (-beta_col)
                e_col = jnp.exp(c_col)
                u_rhs = v32 * beta_col
                w_rhs = k32 * (beta_col * e_col)
                wq_scr[d, n, C:2 * C, :] = (q32 * e_col).astype(BF16)
                kd_scr[d, rows, :] = (k32 * jnp.exp(tot - c_col)).astype(BF16)
                am_scr[d, n] = jnp.where(incl[d], qk * decay, 0.0).astype(BF16)
                gt_scr[d, n] = jnp.broadcast_to(jnp.exp(tot), (8, 128))
                ch.append((d, n, rows, x, u_rhs, w_rhs))
        ys = [c[3] for c in ch]
        xbs = [y.astype(BF16) for y in ys]
        ps = [mm(xb, xb) for xb in xbs]
        for m in range(nsq):
            pbs = [p.astype(BF16) for p in ps]
            ybs = [y.astype(BF16) for y in ys]
            pys = [mm(pb, yb) for pb, yb in zip(pbs, ybs)]
            if m < nsq - 1:
                pps = [mm(pb, pb) for pb in pbs]
            ys = [y + p + py for y, p, py in zip(ys, ps, pys)]
            if m < nsq - 1:
                ps = pps
        ybs = [y.astype(BF16) for y in ys]
        us = [c[4] + mm(yb, c[4].astype(BF16)) for c, yb in zip(ch, ybs)]
        ws = [c[5] + mm(yb, c[5].astype(BF16)) for c, yb in zip(ch, ybs)]
        for c, u, w in zip(ch, us, ws):
            d, n, rows = c[0], c[1], c[2]
            u_scr[d, rows, :] = u
            wq_scr[d, n, 0:C, :] = w.astype(BF16)
        return carry

    lax.fori_loop(0, n_chunks // group, prep, 0)

    s_scr[...] = jnp.zeros_like(s_scr)

    def steps(ns, with_out):
        rows = [pl.ds(pl.multiple_of(n * C, C), C) for n in ns]
        ss = [s_scr[d] for d in range(2)]
        pms = [mm(wq_scr[d, ns[d]], ss[d].astype(BF16)) for d in range(2)]
        vbs = [(u_scr[d, rows[d], :] - pms[d][:C]).astype(BF16) for d in range(2)]
        dss = [lax.dot_general(kd_scr[d, rows[d], :], vbs[d], (((0,), (0,)), ((), ())),
                               preferred_element_type=F32) for d in range(2)]
        outs = None
        if with_out:
            outs = [pms[d][C:] + mm(am_scr[d, ns[d]], vbs[d]) for d in range(2)]
        for d in range(2):
            s_scr[d] = ss[d] * gt_scr[d, ns[d]][0:1, :] + dss[d]
        return outs

    n_ctx_chunks = n_chunks - n_lat_chunks

    def ctx_body(sidx, carry):
        steps((n_lat_chunks + sidx, n_chunks - 1 - sidx), False)
        return carry

    lax.fori_loop(0, n_ctx_chunks, ctx_body, 0)

    def lat_body(sidx, carry):
        nf = sidx
        nb = n_lat_chunks - 1 - sidx
        of, ob = steps((nf, nb), True)
        rf = pl.ds(pl.multiple_of(nf * C, C), C)
        rb = pl.ds(pl.multiple_of(nb * C, C), C)

        @pl.when(nf < nb)
        def _():
            oacc[rf, :] = of
            oacc[rb, :] = ob

        @pl.when(nf > nb)
        def _():
            oacc[rf, :] = oacc[rf, :] + of
            oacc[rb, :] = oacc[rb, :] + ob

        return carry

    lax.fori_loop(0, n_lat_chunks, lat_body, 0)

    ng = ng_ref[...]
    blk = 512

    def fin(j, carry):
        rows = pl.ds(pl.multiple_of(j * blk, blk), blk)
        o = oacc[rows, :]
        on = o * lax.rsqrt(jnp.mean(o * o, axis=-1, keepdims=True) + NORM_EPS) * ng
        o_ref[0, rows, :] = (on * sz_ref[0, rows, :].astype(F32)).astype(BF16)
        return carry

    lax.fori_loop(0, (n_lat_chunks * C) // blk, fin, 0)


def _gdn(qkvn, grow, sza, norm_g, n_heads, t):
    bsz, tall, _ = qkvn.shape
    C = GDN_CHUNK
    n_chunks = tall // C
    n_lat_chunks = t // C
    ncp = grow.shape[3]
    group = 4
    assert n_chunks % group == 0 and n_lat_chunks % 2 == 0
    kern = functools.partial(_gdn_kernel, n_lat_chunks=n_lat_chunks, n_chunks=n_chunks, group=group)
    return pl.pallas_call(
        kern,
        out_shape=jax.ShapeDtypeStruct((bsz, t, n_heads * A_DV), BF16),
        grid=(bsz, n_heads),
        in_specs=[
            pl.BlockSpec((1, tall, A_DK), lambda b, h: (b, 0, h)),
            pl.BlockSpec((1, tall, A_DK), lambda b, h: (b, 0, n_heads + h)),
            pl.BlockSpec((1, tall, A_DV), lambda b, h: (b, 0, 2 * n_heads + h)),
            pl.BlockSpec((1, 1, 4, ncp, C), lambda b, h: (b, h, 0, 0, 0)),
            pl.BlockSpec((1, t, A_DV), lambda b, h: (b, 0, h)),
            pl.BlockSpec((1, A_DV), lambda b, h: (0, 0)),
        ],
        out_specs=pl.BlockSpec((1, t, A_DV), lambda b, h: (b, 0, h)),
        scratch_shapes=[
            pltpu.VMEM((2, n_chunks, 2 * C, A_DK), BF16),
            pltpu.VMEM((2, tall, A_DV), F32),
            pltpu.VMEM((2, tall, A_DK), BF16),
            pltpu.VMEM((2, n_chunks, C, C), BF16),
            pltpu.VMEM((2, n_chunks, 8, 128), F32),
            pltpu.VMEM((2, ncp, C), F32),
            pltpu.VMEM((2, A_DK, A_DV), F32),
            pltpu.VMEM((t, A_DV), F32),
        ],
        compiler_params=_cparams(("parallel", "parallel")),
        name="gdn",
    )(qkvn, qkvn, qkvn, grow, sza, norm_g)


def _attn_kernel(q_ref, k_ref, v_ref, sz_ref, o_ref):
    k = k_ref[0]
    v = v_ref[0]
    nt = (((1,), (1,)), ((), ()))
    for hh in range(q_ref.shape[2] // B_HD):
        sl = slice(hh * B_HD, (hh + 1) * B_HD)
        s = lax.dot_general(q_ref[0, :, sl], k, nt, preferred_element_type=F32)
        m = jnp.max(s, axis=-1, keepdims=True)
        p = jnp.exp(s - m)
        l = jnp.sum(p, axis=-1, keepdims=True)
        o = jnp.dot(p.astype(BF16), v, preferred_element_type=F32)
        o_ref[0, :, sl] = (o * (1.0 / l) * sz_ref[0, :, sl].astype(F32)).astype(BF16)


def _attn(qb, kb, vb, szb, t, tq):
    bsz, tall, qw = qb.shape
    n_kv = kb.shape[2] // B_HD
    gw = qw // n_kv
    return pl.pallas_call(
        _attn_kernel,
        out_shape=jax.ShapeDtypeStruct((bsz, t, qw), BF16),
        grid=(bsz, n_kv, t // tq),
        in_specs=[
            pl.BlockSpec((1, tq, gw), lambda b, g, i: (b, i, g)),
            pl.BlockSpec((1, tall, B_HD), lambda b, g, i: (b, 0, g)),
            pl.BlockSpec((1, tall, B_HD), lambda b, g, i: (b, 0, g)),
            pl.BlockSpec((1, tq, gw), lambda b, g, i: (b, i, g)),
        ],
        out_specs=pl.BlockSpec((1, tq, gw), lambda b, g, i: (b, i, g)),
        compiler_params=_cparams(("parallel", "parallel", "arbitrary")),
        name="attn",
    )(qb, kb, vb, szb)


def _mid_kernel(ya_ref, yb_ref, x_ref, wa_ref, wb_ref, pg_ref, gate_ref, sh_ref, sc_ref, g1_ref, w1_ref, b1_ref,
                xl_o, u_o, sz_o, h_scr):
    d = x_ref.shape[2]
    out = (jnp.dot(ya_ref[0], wa_ref[...], preferred_element_type=F32)
           + jnp.dot(yb_ref[0], wb_ref[...], preferred_element_type=F32))
    on = out * lax.rsqrt(jnp.mean(out * out, axis=-1, keepdims=True) + NORM_EPS) * pg_ref[...]
    xl = x_ref[0] + gate_ref[0] * on
    xl_o[0] = xl
    rs = lax.rsqrt(jnp.mean(xl * xl, axis=-1, keepdims=True) + NORM_EPS)
    h_scr[...] = (xl * rs * (g1_ref[...] * (1.0 + sc_ref[0])) + sh_ref[0]).astype(BF16)
    h = h_scr[...]
    a = jnp.dot(h, w1_ref[:, 0:d], preferred_element_type=F32) + b1_ref[:, 0:d]
    gl = jnp.dot(h, w1_ref[:, d:2 * d], preferred_element_type=F32) + b1_ref[:, d:2 * d]
    u_o[0] = (a * _sigmoid(gl)).astype(BF16)
    z = jnp.dot(h, w1_ref[:, 2 * d:3 * d], preferred_element_type=F32) + b1_ref[:, 2 * d:3 * d]
    sz_o[0] = _silu(z).astype(BF16)


def _mid(ya, yb, x, wa, wb, post_g, mod0, mod1, pre_g1, w1, b1, tm):
    bsz, t, d = x.shape
    aw = ya.shape[2]
    bw = yb.shape[2]
    rows = mod0.shape[0]
    m0 = mod0.reshape(rows, 1, 3 * d)
    m1 = mod1.reshape(rows, 1, 3 * d)
    tok = lambda c: pl.BlockSpec((1, tm, c), lambda b, i: (b, i, 0))
    const2 = lambda a: pl.BlockSpec(a.shape, lambda b, i: (0, 0))
    return pl.pallas_call(
        _mid_kernel,
        out_shape=[jax.ShapeDtypeStruct((bsz, t, d), F32),
                   jax.ShapeDtypeStruct((bsz, t, d), BF16),
                   jax.ShapeDtypeStruct((bsz, t, d), BF16)],
        grid=(bsz, t // tm),
        in_specs=[
            tok(aw), tok(bw), tok(d),
            const2(wa), const2(wb),
            pl.BlockSpec((1, d), lambda b, i: (0, 0)),
            pl.BlockSpec((1, 1, d), lambda b, i: (b, 0, 2)),
            pl.BlockSpec((1, 1, d), lambda b, i: (b, 0, 0)),
            pl.BlockSpec((1, 1, d), lambda b, i: (b, 0, 1)),
            pl.BlockSpec((1, d), lambda b, i: (0, 0)),
            const2(w1),
            pl.BlockSpec((1, 3 * d), lambda b, i: (0, 0)),
        ],
        out_specs=[tok(d), tok(d), tok(d)],
        scratch_shapes=[pltpu.VMEM((tm, d), BF16)],
        compiler_params=_cparams(("parallel", "parallel")),
        name="mid",
    )(ya, yb, x, wa, wb, post_g.reshape(1, d), m0, m1, m1, pre_g1.reshape(1, d), w1, b1.reshape(1, 3 * d))


def _tail_kernel(cur_ref, prev_ref, next_ref, sz_ref, xl_ref, dw_ref, dwb_ref, lng_ref, lnb_ref, wo_ref, bo_ref,
                 pg_ref, gate_ref, o_ref, buf, act, *, halo):
    i = pl.program_id(1)
    nt = pl.num_programs(1)
    tm = cur_ref.shape[1]
    d = cur_ref.shape[2]
    buf[0:halo, :] = jnp.where(i > 0, prev_ref[0].astype(F32), 0.0)
    buf[halo:halo + tm, :] = cur_ref[0].astype(F32)
    buf[halo + tm:2 * halo + tm, :] = jnp.where(i < nt - 1, next_ref[0].astype(F32), 0.0)
    pad = CONV_K // 2
    cw = 256
    for cb in range(d // cw):
        cs = slice(cb * cw, (cb + 1) * cw)
        acc = None
        for j in range(CONV_K):
            term = buf[halo - pad + j:halo - pad + j + tm, cs] * dw_ref[j:j + 1, cs]
            acc = term if acc is None else acc + term
        act[:, cs] = acc + dwb_ref[:, cs]
    u = act[...]
    mu = jnp.mean(u, axis=-1, keepdims=True)
    uc = u - mu
    var = jnp.mean(uc * uc, axis=-1, keepdims=True)
    un = uc * lax.rsqrt(var + LN_EPS) * lng_ref[...] + lnb_ref[...]
    hact = (_silu(un) * sz_ref[0].astype(F32)).astype(BF16)
    out = jnp.dot(hact, wo_ref[...], preferred_element_type=F32) + bo_ref[...]
    on = out * lax.rsqrt(jnp.mean(out * out, axis=-1, keepdims=True) + NORM_EPS) * pg_ref[...]
    o_ref[0] = xl_ref[0] + gate_ref[0] * on


def _tail(u, sz, xl, dw_w, dw_b, ln_g, ln_b, w_out, b_out, post_g, mod1, tm):
    bsz, t, d = xl.shape
    halo = 16
    nhb = tm // halo
    last_blk = t // halo - 1
    rows = mod1.shape[0]
    m1 = mod1.reshape(rows, 1, 3 * d)
    tok = lambda c: pl.BlockSpec((1, tm, c), lambda b, i: (b, i, 0))
    vec = lambda: pl.BlockSpec((1, d), lambda b, i: (0, 0))
    kern = functools.partial(_tail_kernel, halo=halo)
    return pl.pallas_call(
        kern,
        out_shape=jax.ShapeDtypeStruct((bsz, t, d), F32),
        grid=(bsz, t // tm),
        in_specs=[
            tok(d),
            pl.BlockSpec((1, halo, d), lambda b, i: (b, jnp.maximum(i * nhb - 1, 0), 0)),
            pl.BlockSpec((1, halo, d), lambda b, i: (b, jnp.minimum((i + 1) * nhb, last_blk), 0)),
            tok(d), tok(d),
            pl.BlockSpec(dw_w.shape, lambda b, i: (0, 0)),
            vec(), vec(), vec(),
            pl.BlockSpec(w_out.shape, lambda b, i: (0, 0)),
            vec(), vec(),
            pl.BlockSpec((1, 1, d), lambda b, i: (b, 0, 2)),
        ],
        out_specs=tok(d),
        scratch_shapes=[pltpu.VMEM((tm + 2 * halo, d), F32), pltpu.VMEM((tm, d), F32)],
        compiler_params=_cparams(("parallel", "parallel")),
        name="tail",
    )(u, u, u, sz, xl, dw_w, dw_b.reshape(1, d), ln_g.reshape(1, d), ln_b.reshape(1, d), w_out,
      b_out.reshape(1, d), post_g.reshape(1, d), m1)


def _rope_tables(t, nctx):
    rows = t // GRID_W
    row = jnp.repeat(jnp.arange(rows, dtype=F32), GRID_W)
    col = jnp.tile(jnp.arange(GRID_W, dtype=F32), rows)
    axis_dim = B_HD // 2
    inv_freq = ROPE_THETA ** (-jnp.arange(0, axis_dim, 2, dtype=F32) / axis_dim)
    ang = jnp.concatenate([row[:, None] * inv_freq, col[:, None] * inv_freq], axis=1)
    cs = jnp.concatenate([jnp.cos(ang), jnp.cos(ang)], axis=1)
    sn = jnp.concatenate([-jnp.sin(ang), jnp.sin(ang)], axis=1)
    cs = jnp.concatenate([cs, jnp.ones((nctx, B_HD), F32)], axis=0)
    sn = jnp.concatenate([sn, jnp.zeros((nctx, B_HD), F32)], axis=0)
    return cs, sn


def _head_perm():
    q = B_HD // 4
    return np.concatenate([np.arange(0, q), np.arange(2 * q, 3 * q), np.arange(q, 2 * q), np.arange(3 * q, 4 * q)])


def kernel(x, c, ctx, c_ctx, ada_w, ada_b, pre_norm_g, post_norm_g, ev_w_in, ev_short_conv_w, ev_a_log,
           ev_dt_bias, ev_gdn_norm_g, ev_q_norm_g, ev_k_norm_g, ev_w_out, od_w_in, od_b_in, od_dw_w, od_dw_b,
           od_ln_g, od_ln_b, od_w_out, od_b_out):
    bsz, t, d = x.shape
    nctx = ctx.shape[1]
    a_width = d // 2
    a_heads = a_width // A_DV
    a_qkv = a_heads * (2 * A_DK + A_DV)
    b_width = d - a_width
    b_heads = b_width // B_HD
    b_kv = b_heads // 2
    assert ada_w.shape[0] == 2 and ev_w_in.shape[0] == 1 and od_w_in.shape[0] == 1

    mod_rows = ((bsz + 1 + 7) // 8) * 8
    cond = jnp.zeros((mod_rows, d), F32).at[:bsz].set(c).at[bsz].set(c_ctx)
    mod = _ada(cond, ada_w, ada_b)

    w_in = ev_w_in[0]
    splits = np.cumsum([0, a_qkv, a_width, 2 * a_heads, 2 * a_heads, b_heads * B_HD, b_kv * B_HD, b_kv * B_HD,
                        b_width])
    w_qkv, w_za, w_a, w_b, w_qb, w_kb, w_vb, w_zb = [w_in[:, splits[j]:splits[j + 1]] for j in range(8)]
    perm = _head_perm()
    perm_q = np.concatenate([h * B_HD + perm for h in range(b_heads)])
    perm_k = np.concatenate([h * B_HD + perm for h in range(b_kv)])
    ab_pad = 128 - 4 * a_heads
    pieces = [("qkv", w_qkv), ("za", w_za), ("qb", w_qb[:, perm_q]), ("kb", w_kb[:, perm_k]), ("vb", w_vb),
              ("zb", w_zb), ("ab", jnp.concatenate([w_a, w_b, jnp.zeros((d, ab_pad), F32)], axis=1))]
    secs, off = {}, 0
    for name, wpart in pieces:
        secs[name] = (off, off + wpart.shape[1])
        off += wpart.shape[1]
    w2 = jnp.concatenate([p[1] for p in pieces], axis=1).astype(BF16)
    cs_tab, sn_tab = _rope_tables(t, nctx)
    qg = ev_q_norm_g[0][perm].reshape(1, B_HD)
    kg = ev_k_norm_g[0][perm].reshape(1, B_HD)
    alog = jnp.zeros((1, 128), F32).at[0, :2 * a_heads].set(ev_a_log[0].reshape(-1))
    dtb = jnp.zeros((1, 128), F32).at[0, :2 * a_heads].set(ev_dt_bias[0].reshape(-1))

    tm0 = nctx
    qkv_raw, sza, qb, kb, vb, szb, gb = _inproj0(x, ctx, mod[0], pre_norm_g[0], w2, secs, cs_tab, sn_tab, qg, kg,
                                                 alog, dtb, tm0)

    qkvn = _gdnconv(qkv_raw, ev_short_conv_w[0], tm0, t // tm0, a_heads)
    tall = t + nctx
    n_chunks = tall // GDN_CHUNK
    ncp = ((n_chunks + 7) // 8) * 8
    grow = gb.reshape(bsz, n_chunks, GDN_CHUNK, 4, a_heads).transpose(0, 4, 3, 1, 2)
    grow = jnp.pad(grow, ((0, 0), (0, 0), (0, 0), (0, ncp - n_chunks), (0, 0)))
    ya = _gdn(qkvn, grow, sza, ev_gdn_norm_g[0].reshape(1, A_DV), a_heads, t)

    yb = _attn(qb, kb, vb, szb, t, 256)

    w_out0 = ev_w_out[0].astype(BF16)
    xl1, u1, sz1 = _mid(ya, yb, x, w_out0[:a_width], w_out0[a_width:], post_norm_g[0], mod[0], mod[1],
                        pre_norm_g[1], od_w_in[0].astype(BF16), od_b_in[0], 256)
    return _tail(u1, sz1, xl1, od_dw_w[0], od_dw_b[0], od_ln_g[0], od_ln_b[0], od_w_out[0].astype(BF16),
                 od_b_out[0], post_norm_g[1], mod[1], 256)
```

```python
import functools
import math

import numpy as np
import jax
import jax.numpy as jnp
from jax import lax
from jax.experimental import pallas as pl
from jax.experimental.pallas import tpu as pltpu

F32 = jnp.float32
BF16 = jnp.bfloat16
HIGHEST = lax.Precision.HIGHEST

GRID_W = 64
A_DK = 128
A_DV = 128
SHORT_CONV = 5
GDN_CHUNK = 64
INV_BLOCK = 16
B_HD = 128
ROPE_THETA = 10000.0
CONV_K = 31
NORM_EPS = 1e-6
LN_EPS = 1e-5
NEG_BIG = -1e30

VMEM_LIMIT = 56 * 1024 * 1024


def _sigmoid(x):
    return 1.0 / (1.0 + jnp.exp(-x))


def _silu(x):
    return x * _sigmoid(x)


def _softplus(x):
    return jnp.maximum(x, 0.0) + jnp.log(1.0 + jnp.exp(-jnp.abs(x)))


def _cparams(sem):
    return pltpu.CompilerParams(dimension_semantics=sem, vmem_limit_bytes=VMEM_LIMIT)


def _ada_kernel(c_ref, w_ref, b_ref, o_ref):
    s = _silu(c_ref[...])
    o_ref[0] = jnp.dot(s, w_ref[0], preferred_element_type=F32, precision=HIGHEST) + b_ref[0]


def _ada(cond, ada_w, ada_b):
    depth, d, d3 = ada_w.shape
    rows = cond.shape[0]
    nt = d3 // d
    return pl.pallas_call(
        _ada_kernel,
        out_shape=jax.ShapeDtypeStruct((depth, rows, d3), F32),
        grid=(depth, nt),
        in_specs=[
            pl.BlockSpec((rows, d), lambda l, j: (0, 0)),
            pl.BlockSpec((1, d, d), lambda l, j: (l, 0, j)),
            pl.BlockSpec((1, 1, d), lambda l, j: (l, 0, j)),
        ],
        out_specs=pl.BlockSpec((1, rows, d), lambda l, j: (l, 0, j)),
        compiler_params=_cparams(("parallel", "parallel")),
        name="ada",
    )(cond, ada_w, ada_b.reshape(depth, 1, d3))


def _inproj0_kernel(x_ref, ctx_ref, sh_ref, sc_ref, g_ref, w_ref, cs_ref, sn_ref, qg_ref, kg_ref, alog_ref, dtb_ref,
                    qkv_o, za_o, qb_o, kb_o, vb_o, zb_o, gb_o, h_scr, *, n_lat_tiles, secs):
    i = pl.program_id(1)
    mult = g_ref[...] * (1.0 + sc_ref[0])
    shift = sh_ref[0]

    def modulate(src):
        xf = src[0]
        rs = lax.rsqrt(jnp.mean(xf * xf, axis=-1, keepdims=True) + NORM_EPS)
        h_scr[...] = (xf * rs * mult + shift).astype(BF16)

    @pl.when(i < n_lat_tiles)
    def _():
        modulate(x_ref)

    @pl.when(i >= n_lat_tiles)
    def _():
        modulate(ctx_ref)

    h = h_scr[...]

    def proj(name):
        c0, c1 = secs[name]
        return jnp.dot(h, w_ref[:, c0:c1], preferred_element_type=F32)

    qkv_o[0] = proj("qkv").astype(BF16)
    za_o[0] = _silu(proj("za")).astype(BF16)
    zb_o[0] = _silu(proj("zb")).astype(BF16)
    vb_o[0] = proj("vb").astype(BF16)

    cs = cs_ref[...]
    sn = sn_ref[...]

    def norm_rope(y, g, out_scale):
        ms = jnp.mean(y * y, axis=-1, keepdims=True)
        yn = y * lax.rsqrt(ms + NORM_EPS) * g
        return (yn * cs + pltpu.roll(yn, B_HD // 2, 1) * sn) * out_scale

    yq = proj("qb")
    qg = qg_ref[...]
    for hh in range(yq.shape[1] // B_HD):
        sl = slice(hh * B_HD, (hh + 1) * B_HD)
        qb_o[0, :, sl] = norm_rope(yq[:, sl], qg, B_HD ** -0.5).astype(BF16)
    yk = proj("kb")
    kg = kg_ref[...]
    for hh in range(yk.shape[1] // B_HD):
        sl = slice(hh * B_HD, (hh + 1) * B_HD)
        kb_o[0, :, sl] = norm_rope(yk[:, sl], kg, 1.0).astype(BF16)

    ab = proj("ab")
    nab = gb_o.shape[2]
    gval = -jnp.exp(alog_ref[...]) * _softplus(ab + dtb_ref[...])
    bval = _sigmoid(ab)
    lane = lax.broadcasted_iota(jnp.int32, ab.shape, 1)
    gb_o[0] = jnp.where(lane < nab // 2, gval, bval)[:, :nab]


def _inproj0(x, ctx, mod0, pre_g, w2, secs, cs_tab, sn_tab, qg, kg, alog, dtb, tm):
    bsz, t, d = x.shape
    nctx = ctx.shape[1]
    tall = t + nctx
    assert t % tm == 0 and nctx == tm
    n_lat = t // tm
    ntile = n_lat + 1
    nab = 16

    def widths(name):
        return secs[name][1] - secs[name][0]

    def row_spec(c):
        return pl.BlockSpec((1, tm, c), lambda b, i: (b, i, 0))

    mod_rows = mod0.shape[0]
    mod3 = mod0.reshape(mod_rows, 1, 3 * d)
    ctx_row = bsz
    outs = [
        jax.ShapeDtypeStruct((bsz, tall, widths("qkv")), BF16),
        jax.ShapeDtypeStruct((bsz, tall, widths("za")), BF16),
        jax.ShapeDtypeStruct((bsz, tall, widths("qb")), BF16),
        jax.ShapeDtypeStruct((bsz, tall, widths("kb")), BF16),
        jax.ShapeDtypeStruct((bsz, tall, widths("vb")), BF16),
        jax.ShapeDtypeStruct((bsz, tall, widths("zb")), BF16),
        jax.ShapeDtypeStruct((bsz, tall, nab), F32),
    ]
    kern = functools.partial(_inproj0_kernel, n_lat_tiles=n_lat, secs=secs)
    return pl.pallas_call(
        kern,
        out_shape=outs,
        grid=(bsz, ntile),
        in_specs=[
            pl.BlockSpec((1, tm, d), lambda b, i: (b, jnp.minimum(i, n_lat - 1), 0)),
            pl.BlockSpec((1, tm, d), lambda b, i: (b, 0, 0)),
            pl.BlockSpec((1, 1, d), lambda b, i: (jnp.where(i < n_lat, b, ctx_row), 0, 0)),
            pl.BlockSpec((1, 1, d), lambda b, i: (jnp.where(i < n_lat, b, ctx_row), 0, 1)),
            pl.BlockSpec((1, d), lambda b, i: (0, 0)),
            pl.BlockSpec(w2.shape, lambda b, i: (0, 0)),
            pl.BlockSpec((tm, B_HD), lambda b, i: (i, 0)),
            pl.BlockSpec((tm, B_HD), lambda b, i: (i, 0)),
            pl.BlockSpec((1, B_HD), lambda b, i: (0, 0)),
            pl.BlockSpec((1, B_HD), lambda b, i: (0, 0)),
            pl.BlockSpec((1, 128), lambda b, i: (0, 0)),
            pl.BlockSpec((1, 128), lambda b, i: (0, 0)),
        ],
        out_specs=[row_spec(widths("qkv")), row_spec(widths("za")), row_spec(widths("qb")), row_spec(widths("kb")),
                   row_spec(widths("vb")), row_spec(widths("zb")), row_spec(nab)],
        scratch_shapes=[pltpu.VMEM((tm, d), BF16)],
        compiler_params=_cparams(("parallel", "arbitrary")),
        name="inproj0",
    )(x, ctx, mod3, mod3, pre_g.reshape(1, d), w2, cs_tab, sn_tab, qg, kg, alog, dtb)


def _gdnconv_kernel(cur_ref, prev_ref, next_ref, w_ref, o_ref, buf, *, n_lat_tiles, halo, n_qk_blocks):
    i = pl.program_id(1)
    tm = cur_ref.shape[1]
    prev_ok = jnp.logical_and(i > 0, i < n_lat_tiles)
    next_ok = i < n_lat_tiles - 1
    buf[0:halo, :] = jnp.where(prev_ok, prev_ref[0].astype(F32), 0.0)
    buf[halo:halo + tm, :] = cur_ref[0].astype(F32)
    buf[halo + tm:2 * halo + tm, :] = jnp.where(next_ok, next_ref[0].astype(F32), 0.0)
    pad = SHORT_CONV // 2
    ncb = cur_ref.shape[2] // 128
    for cb in range(ncb):
        cs = slice(cb * 128, (cb + 1) * 128)
        acc = None
        for j in range(SHORT_CONV):
            term = buf[halo - pad + j:halo - pad + j + tm, cs] * w_ref[j:j + 1, cs]
            acc = term if acc is None else acc + term
        u = _silu(acc)
        if cb < n_qk_blocks:
            u = u * lax.rsqrt(jnp.sum(u * u, axis=-1, keepdims=True) + NORM_EPS)
            if cb < n_qk_blocks // 2:
                u = u * (A_DK ** -0.5)
        o_ref[0, :, cs] = u.astype(BF16)


def _gdnconv(qkv_raw, conv_w, tm, n_lat, n_heads):
    bsz, tall, c = qkv_raw.shape
    halo = 16
    nhb = tm // halo
    last_blk = tall // halo - 1
    kern = functools.partial(_gdnconv_kernel, n_lat_tiles=n_lat, halo=halo, n_qk_blocks=2 * n_heads)
    return pl.pallas_call(
        kern,
        out_shape=jax.ShapeDtypeStruct((bsz, tall, c), BF16),
        grid=(bsz, tall // tm),
        in_specs=[
            pl.BlockSpec((1, tm, c), lambda b, i: (b, i, 0)),
            pl.BlockSpec((1, halo, c), lambda b, i: (b, jnp.maximum(i * nhb - 1, 0), 0)),
            pl.BlockSpec((1, halo, c), lambda b, i: (b, jnp.minimum((i + 1) * nhb, last_blk), 0)),
            pl.BlockSpec(conv_w.shape, lambda b, i: (0, 0)),
        ],
        out_specs=pl.BlockSpec((1, tm, c), lambda b, i: (b, i, 0)),
        scratch_shapes=[pltpu.VMEM((tm + 2 * halo, c), F32)],
        compiler_params=_cparams(("parallel", "parallel")),
        name="gdnconv",
    )(qkv_raw, qkv_raw, qkv_raw, conv_w)


def _gdn_kernel(q_ref, k_ref, v_ref, gr_ref, sz_ref, ng_ref, o_ref,
                nq_scr, z_scr, au_scr, gt_scr, cr_scr, s_scr, oacc, *, n_lat_chunks, n_chunks, group, ctx_group):
    C = GDN_CHUNK
    ri = lax.broadcasted_iota(jnp.int32, (C, C), 0)
    ci = lax.broadcasted_iota(jnp.int32, (C, C), 1)
    incl = (ri >= ci, ri <= ci)
    strict = (ri > ci, ri < ci)
    eye = (ri == ci).astype(F32)
    tri_row = ((ri <= ci).astype(F32), (ri >= ci).astype(F32))
    tri_col = (incl[0].astype(F32), incl[1].astype(F32))

    for d in range(2):
        cr_scr[d] = jnp.dot(gr_ref[0, 0, d], tri_row[d], preferred_element_type=F32, precision=HIGHEST)

    nt = (((1,), (1,)), ((), ()))
    nsq = int(math.log2(INV_BLOCK)) - 1
    diag_blk = (ri // INV_BLOCK) == (ci // INV_BLOCK)
    off_blks = []
    bs = INV_BLOCK
    while bs < C:
        off_blks.append(jnp.logical_and((ri // (2 * bs)) == (ci // (2 * bs)), (ri // bs) != (ci // bs)))
        bs *= 2

    def mm(a, b):
        return jnp.dot(a, b, preferred_element_type=F32)

    tl = (((0,), (0,)), ((), ()))

    def prep(chunk_ids):
        ch = []
        for n in chunk_ids:
            rows = pl.ds(pl.multiple_of(n * C, C), C)
            kb = k_ref[0, rows, :]
            qb = q_ref[0, rows, :]
            k32 = kb.astype(F32)
            q32 = qb.astype(F32)
            v32 = v_ref[0, rows, :].astype(F32)
            kk = lax.dot_general(kb, kb, nt, preferred_element_type=F32)
            qk = lax.dot_general(qb, kb, nt, preferred_element_type=F32)
            for d in range(2):
                g_row = gr_ref[0, 0, d, pl.ds(n, 1), :]
                beta_row = gr_ref[0, 0, 2 + d, pl.ds(n, 1), :]
                c_row = cr_scr[d, pl.ds(n, 1), :]
                c_col = jnp.sum(tri_col[d] * g_row, axis=-1, keepdims=True)
                beta_col = jnp.sum(eye * beta_row, axis=-1, keepdims=True)
                tot = jnp.sum(g_row, axis=-1, keepdims=True)
                decay = jnp.exp(jnp.where(incl[d], c_col - c_row, NEG_BIG))
                x = jnp.where(strict[d], kk * decay, 0.0) * (-beta_col)
                e_col = jnp.exp(c_col)
                rhs = jnp.concatenate([v32 * beta_col, k32 * (beta_col * e_col)], axis=1)
                qd = q32 * e_col
                kd = (k32 * jnp.exp(tot - c_col)).astype(BF16)
                am = jnp.where(incl[d], qk * decay, 0.0).astype(BF16)
                gt_scr[d, n] = jnp.broadcast_to(jnp.exp(tot), (8, 128))
                ch.append((d, n, rows, x, rhs, qd, kd, am))
        xs = [c[3] for c in ch]
        ys = [jnp.where(diag_blk, x, 0.0) for x in xs]
        xbs = [y.astype(BF16) for y in ys]
        ps = [mm(xb, xb) for xb in xbs]
        for m in range(nsq):
            pbs = [p.astype(BF16) for p in ps]
            ybs = [y.astype(BF16) for y in ys]
            if m < nsq - 1:
                rs = [mm(jnp.concatenate([pb, yb], axis=0), pb) for pb, yb in zip(pbs, ybs)]
                ys = [y + p + r[C:] for y, p, r in zip(ys, ps, rs)]
                ps = [r[:C] for r in rs]
            else:
                ys = [y + p + mm(yb, pb) for y, p, yb, pb in zip(ys, ps, ybs, pbs)]
        for off in off_blks:
            xos = [jnp.where(off, x, 0.0) for x in xs]
            ybs = [y.astype(BF16) for y in ys]
            ts = [xo + mm(yb, xo.astype(BF16)) for xo, yb in zip(xos, ybs)]
            ys = [y + t + mm(t.astype(BF16), yb) for y, t, yb in zip(ys, ts, ybs)]
        ybs = [y.astype(BF16) for y in ys]
        uws = [c[4] + mm(yb, c[4].astype(BF16)) for c, yb in zip(ch, ybs)]
        uwbs = [uw.astype(BF16) for uw in uws]
        kzs = [lax.dot_general(c[6], uwb, tl, preferred_element_type=F32) for c, uwb in zip(ch, uwbs)]
        azs = [mm(c[7], uwb) for c, uwb in zip(ch, uwbs)]
        for c, kz, az in zip(ch, kzs, azs):
            d, n, rows = c[0], c[1], c[2]
            z_scr[d, n] = kz[:, :A_DV]
            nq_scr[d, n, 0:A_DK, :] = kz[:, A_DV:].astype(BF16)
            nq_scr[d, n, A_DK:A_DK + C, :] = (c[5] - az[:, A_DV:]).astype(BF16)
            au_scr[d, rows, :] = az[:, :A_DV]

    n_ctx_chunks = n_chunks - n_lat_chunks

    def prep_lat(gi, carry):
        prep([gi * group + i for i in range(group)])
        return carry

    def prep_ctx(gi, carry):
        prep([n_lat_chunks + gi * ctx_group + i for i in range(ctx_group)])
        return carry

    lax.fori_loop(0, n_lat_chunks // group, prep_lat, 0)
    lax.fori_loop(0, n_ctx_chunks // ctx_group, prep_ctx, 0)

    s_scr[...] = jnp.zeros_like(s_scr)

    def steps(ns, with_out):
        nrow = A_DK + C if with_out else A_DK
        ss = [s_scr[d] for d in range(2)]
        rs = [mm(nq_scr[d, ns[d], 0:nrow, :], ss[d].astype(BF16)) for d in range(2)]
        for d in range(2):
            s_scr[d] = ss[d] * gt_scr[d, ns[d]][0:1, :] - rs[d][:A_DK] + z_scr[d, ns[d]]
        if with_out:
            return [rs[d][A_DK:] + au_scr[d, pl.ds(pl.multiple_of(ns[d] * C, C), C), :] for d in range(2)]
        return None

    def ctx_body(sidx, carry):
        steps((n_lat_chunks + sidx, n_chunks - 1 - sidx), False)
        return carry

    lax.fori_loop(0, n_ctx_chunks, ctx_body, 0)

    def lat_body(sidx, carry):
        nf = sidx
        nb = n_lat_chunks - 1 - sidx
        of, ob = steps((nf, nb), True)
        rf = pl.ds(pl.multiple_of(nf * C, C), C)
        rb = pl.ds(pl.multiple_of(nb * C, C), C)

        @pl.when(nf < nb)
        def _():
            oacc[rf, :] = of
            oacc[rb, :] = ob

        @pl.when(nf > nb)
        def _():
            oacc[rf, :] = oacc[rf, :] + of
            oacc[rb, :] = oacc[rb, :] + ob

        return carry

    lax.fori_loop(0, n_lat_chunks, lat_body, 0)

    ng = ng_ref[...]
    blk = 512

    def fin(j, carry):
        rows = pl.ds(pl.multiple_of(j * blk, blk), blk)
        o = oacc[rows, :]
        on = o * lax.rsqrt(jnp.mean(o * o, axis=-1, keepdims=True) + NORM_EPS) * ng
        o_ref[0, rows, :] = (on * sz_ref[0, rows, :].astype(F32)).astype(BF16)
        return carry

    lax.fori_loop(0, (n_lat_chunks * C) // blk, fin, 0)


def _gdn(qkvn, grow, sza, norm_g, n_heads, t):
    bsz, tall, _ = qkvn.shape
    C = GDN_CHUNK
    n_chunks = tall // C
    n_lat_chunks = t // C
    ncp = grow.shape[3]
    group, ctx_group = 8, 4
    assert n_lat_chunks % group == 0 and (n_chunks - n_lat_chunks) % ctx_group == 0 and n_lat_chunks % 2 == 0
    kern = functools.partial(_gdn_kernel, n_lat_chunks=n_lat_chunks, n_chunks=n_chunks, group=group,
                             ctx_group=ctx_group)
    return pl.pallas_call(
        kern,
        out_shape=jax.ShapeDtypeStruct((bsz, t, n_heads * A_DV), BF16),
        grid=(bsz, n_heads),
        in_specs=[
            pl.BlockSpec((1, tall, A_DK), lambda b, h: (b, 0, h)),
            pl.BlockSpec((1, tall, A_DK), lambda b, h: (b, 0, n_heads + h)),
            pl.BlockSpec((1, tall, A_DV), lambda b, h: (b, 0, 2 * n_heads + h)),
            pl.BlockSpec((1, 1, 4, ncp, C), lambda b, h: (b, h, 0, 0, 0)),
            pl.BlockSpec((1, t, A_DV), lambda b, h: (b, 0, h)),
            pl.BlockSpec((1, A_DV), lambda b, h: (0, 0)),
        ],
        out_specs=pl.BlockSpec((1, t, A_DV), lambda b, h: (b, 0, h)),
        scratch_shapes=[
            pltpu.VMEM((2, n_chunks, A_DK + C, A_DK), BF16),
            pltpu.VMEM((2, n_chunks, A_DK, A_DV), F32),
            pltpu.VMEM((2, tall, A_DV), F32),
            pltpu.VMEM((2, n_chunks, 8, 128), F32),
            pltpu.VMEM((2, ncp, C), F32),
            pltpu.VMEM((2, A_DK, A_DV), F32),
            pltpu.VMEM((t, A_DV), F32),
        ],
        compiler_params=_cparams(("parallel", "parallel")),
        name="gdn",
    )(qkvn, qkvn, qkvn, grow, sza, norm_g)


def _attn_kernel(q_ref, k_ref, v_ref, sz_ref, o_ref, *, kv_blocks):
    nt = (((1,), (1,)), ((), ()))
    n_heads = q_ref.shape[2] // B_HD
    items = [(hh, blk) for hh in range(n_heads) for blk in kv_blocks]

    def scores(item):
        hh, (k0, k1) = item
        q = q_ref[0, :, hh * B_HD:(hh + 1) * B_HD]
        return lax.dot_general(q, k_ref[0, k0:k1, :], nt, preferred_element_type=F32)

    s_next = scores(items[0])
    m = l = acc = None
    for idx, (hh, (k0, k1)) in enumerate(items):
        s = s_next
        if idx + 1 < len(items):
            s_next = scores(items[idx + 1])
        bm = jnp.max(s, axis=-1, keepdims=True)
        if k0 == kv_blocks[0][0]:
            m = bm
            p = jnp.exp(s - m)
            l = jnp.sum(p, axis=-1, keepdims=True)
            acc = jnp.dot(p.astype(BF16), v_ref[0, k0:k1, :], preferred_element_type=F32)
        else:
            m_new = jnp.maximum(m, bm)
            alpha = jnp.exp(m - m_new)
            p = jnp.exp(s - m_new)
            l = alpha * l + jnp.sum(p, axis=-1, keepdims=True)
            acc = alpha * acc + jnp.dot(p.astype(BF16), v_ref[0, k0:k1, :], preferred_element_type=F32)
            m = m_new
        if k1 == kv_blocks[-1][1]:
            sl = slice(hh * B_HD, (hh + 1) * B_HD)
            o_ref[0, :, sl] = (acc * (1.0 / l) * sz_ref[0, :, sl].astype(F32)).astype(BF16)


def _attn(qb, kb, vb, szb, t, tq):
    bsz, tall, qw = qb.shape
    n_kv = kb.shape[2] // B_HD
    gw = qw // n_kv
    kvb = 1024
    edges = list(range(0, tall, kvb)) + [tall]
    kv_blocks = tuple((edges[j], edges[j + 1]) for j in range(len(edges) - 1))
    return pl.pallas_call(
        functools.partial(_attn_kernel, kv_blocks=kv_blocks),
        out_shape=jax.ShapeDtypeStruct((bsz, t, qw), BF16),
        grid=(bsz, n_kv, t // tq),
        in_specs=[
            pl.BlockSpec((1, tq, gw), lambda b, g, i: (b, i, g)),
            pl.BlockSpec((1, tall, B_HD), lambda b, g, i: (b, 0, g)),
            pl.BlockSpec((1, tall, B_HD), lambda b, g, i: (b, 0, g)),
            pl.BlockSpec((1, tq, gw), lambda b, g, i: (b, i, g)),
        ],
        out_specs=pl.BlockSpec((1, tq, gw), lambda b, g, i: (b, i, g)),
        compiler_params=_cparams(("parallel", "parallel", "arbitrary")),
        name="attn",
    )(qb, kb, vb, szb)


def _mid_kernel(ya_ref, yb_ref, x_ref, wa_ref, wb_ref, pg_ref, gate_ref, sh_ref, sc_ref, g1_ref, w1_ref, b1_ref,
                xl_o, u_o, sz_o, h_scr):
    d = x_ref.shape[2]
    out = (jnp.dot(ya_ref[0], wa_ref[...], preferred_element_type=F32)
           + jnp.dot(yb_ref[0], wb_ref[...], preferred_element_type=F32))
    on = out * lax.rsqrt(jnp.mean(out * out, axis=-1, keepdims=True) + NORM_EPS) * pg_ref[...]
    xl = x_ref[0] + gate_ref[0] * on
    xl_o[0] = xl
    rs = lax.rsqrt(jnp.mean(xl * xl, axis=-1, keepdims=True) + NORM_EPS)
    h_scr[...] = (xl * rs * (g1_ref[...] * (1.0 + sc_ref[0])) + sh_ref[0]).astype(BF16)
    h = h_scr[...]
    a = jnp.dot(h, w1_ref[:, 0:d], preferred_element_type=F32) + b1_ref[:, 0:d]
    gl = jnp.dot(h, w1_ref[:, d:2 * d], preferred_element_type=F32) + b1_ref[:, d:2 * d]
    u_o[0] = (a * _sigmoid(gl)).astype(BF16)
    z = jnp.dot(h, w1_ref[:, 2 * d:3 * d], preferred_element_type=F32) + b1_ref[:, 2 * d:3 * d]
    sz_o[0] = _silu(z).astype(BF16)


def _mid(ya, yb, x, wa, wb, post_g, mod0, mod1, pre_g1, w1, b1, tm):
    bsz, t, d = x.shape
    aw = ya.shape[2]
    bw = yb.shape[2]
    rows = mod0.shape[0]
    m0 = mod0.reshape(rows, 1, 3 * d)
    m1 = mod1.reshape(rows, 1, 3 * d)
    tok = lambda c: pl.BlockSpec((1, tm, c), lambda b, i: (b, i, 0))
    const2 = lambda a: pl.BlockSpec(a.shape, lambda b, i: (0, 0))
    return pl.pallas_call(
        _mid_kernel,
        out_shape=[jax.ShapeDtypeStruct((bsz, t, d), F32),
                   jax.ShapeDtypeStruct((bsz, t, d), BF16),
                   jax.ShapeDtypeStruct((bsz, t, d), BF16)],
        grid=(bsz, t // tm),
        in_specs=[
            tok(aw), tok(bw), tok(d),
            const2(wa), const2(wb),
            pl.BlockSpec((1, d), lambda b, i: (0, 0)),
            pl.BlockSpec((1, 1, d), lambda b, i: (b, 0, 2)),
            pl.BlockSpec((1, 1, d), lambda b, i: (b, 0, 0)),
            pl.BlockSpec((1, 1, d), lambda b, i: (b, 0, 1)),
            pl.BlockSpec((1, d), lambda b, i: (0, 0)),
            const2(w1),
            pl.BlockSpec((1, 3 * d), lambda b, i: (0, 0)),
        ],
        out_specs=[tok(d), tok(d), tok(d)],
        scratch_shapes=[pltpu.VMEM((tm, d), BF16)],
        compiler_params=_cparams(("parallel", "parallel")),
        name="mid",
    )(ya, yb, x, wa, wb, post_g.reshape(1, d), m0, m1, m1, pre_g1.reshape(1, d), w1, b1.reshape(1, 3 * d))


def _tail_kernel(cur_ref, prev_ref, next_ref, sz_ref, xl_ref, dw_ref, dwb_ref, lng_ref, lnb_ref, wo_ref, bo_ref,
                 pg_ref, gate_ref, o_ref, buf, act, *, halo):
    i = pl.program_id(1)
    nt = pl.num_programs(1)
    tm = cur_ref.shape[1]
    d = cur_ref.shape[2]
    buf[0:halo, :] = jnp.where(i > 0, prev_ref[0].astype(F32), 0.0)
    buf[halo:halo + tm, :] = cur_ref[0].astype(F32)
    buf[halo + tm:2 * halo + tm, :] = jnp.where(i < nt - 1, next_ref[0].astype(F32), 0.0)
    pad = CONV_K // 2
    cw = 128
    sub = 8
    for cb in range(d // cw):
        cs = slice(cb * cw, (cb + 1) * cw)
        acc = None
        for s in range(sub):
            part = None
            for a in range((halo + pad) // sub + 1):
                j = sub * a + s - (halo - pad)
                if 0 <= j < CONV_K:
                    term = buf[sub * a:sub * a + tm + sub, cs] * dw_ref[j:j + 1, cs]
                    part = term if part is None else part + term
            if part is not None:
                acc = part[s:s + tm] if acc is None else acc + part[s:s + tm]
        act[:, cs] = acc + dwb_ref[:, cs]
    u = act[...]
    mu = jnp.mean(u, axis=-1, keepdims=True)
    uc = u - mu
    var = jnp.mean(uc * uc, axis=-1, keepdims=True)
    un = uc * lax.rsqrt(var + LN_EPS) * lng_ref[...] + lnb_ref[...]
    hact = (_silu(un) * sz_ref[0].astype(F32)).astype(BF16)
    out = jnp.dot(hact, wo_ref[...], preferred_element_type=F32) + bo_ref[...]
    on = out * lax.rsqrt(jnp.mean(out * out, axis=-1, keepdims=True) + NORM_EPS) * pg_ref[...]
    o_ref[0] = xl_ref[0] + gate_ref[0] * on


def _tail(u, sz, xl, dw_w, dw_b, ln_g, ln_b, w_out, b_out, post_g, mod1, tm):
    bsz, t, d = xl.shape
    halo = 16
    nhb = tm // halo
    last_blk = t // halo - 1
    rows = mod1.shape[0]
    m1 = mod1.reshape(rows, 1, 3 * d)
    tok = lambda c: pl.BlockSpec((1, tm, c), lambda b, i: (b, i, 0))
    vec = lambda: pl.BlockSpec((1, d), lambda b, i: (0, 0))
    kern = functools.partial(_tail_kernel, halo=halo)
    return pl.pallas_call(
        kern,
        out_shape=jax.ShapeDtypeStruct((bsz, t, d), F32),
        grid=(bsz, t // tm),
        in_specs=[
            tok(d),
            pl.BlockSpec((1, halo, d), lambda b, i: (b, jnp.maximum(i * nhb - 1, 0), 0)),
            pl.BlockSpec((1, halo, d), lambda b, i: (b, jnp.minimum((i + 1) * nhb, last_blk), 0)),
            tok(d), tok(d),
            pl.BlockSpec(dw_w.shape, lambda b, i: (0, 0)),
            vec(), vec(), vec(),
            pl.BlockSpec(w_out.shape, lambda b, i: (0, 0)),
            vec(), vec(),
            pl.BlockSpec((1, 1, d), lambda b, i: (b, 0, 2)),
        ],
        out_specs=tok(d),
        scratch_shapes=[pltpu.VMEM((tm + 2 * halo, d), F32), pltpu.VMEM((tm, d), F32)],
        compiler_params=_cparams(("parallel", "parallel")),
        name="tail",
    )(u, u, u, sz, xl, dw_w, dw_b.reshape(1, d), ln_g.reshape(1, d), ln_b.reshape(1, d), w_out,
      b_out.reshape(1, d), post_g.reshape(1, d), m1)


def _rope_tables(t, nctx):
    rows = t // GRID_W
    row = jnp.repeat(jnp.arange(rows, dtype=F32), GRID_W)
    col = jnp.tile(jnp.arange(GRID_W, dtype=F32), rows)
    axis_dim = B_HD // 2
    inv_freq = ROPE_THETA ** (-jnp.arange(0, axis_dim, 2, dtype=F32) / axis_dim)
    ang = jnp.concatenate([row[:, None] * inv_freq, col[:, None] * inv_freq], axis=1)
    cs = jnp.concatenate([jnp.cos(ang), jnp.cos(ang)], axis=1)
    sn = jnp.concatenate([-jnp.sin(ang), jnp.sin(ang)], axis=1)
    cs = jnp.concatenate([cs, jnp.ones((nctx, B_HD), F32)], axis=0)
    sn = jnp.concatenate([sn, jnp.zeros((nctx, B_HD), F32)], axis=0)
    return cs, sn


def _head_perm():
    q = B_HD // 4
    return np.concatenate([np.arange(0, q), np.arange(2 * q, 3 * q), np.arange(q, 2 * q), np.arange(3 * q, 4 * q)])


def kernel(x, c, ctx, c_ctx, ada_w, ada_b, pre_norm_g, post_norm_g, ev_w_in, ev_short_conv_w, ev_a_log,
           ev_dt_bias, ev_gdn_norm_g, ev_q_norm_g, ev_k_norm_g, ev_w_out, od_w_in, od_b_in, od_dw_w, od_dw_b,
           od_ln_g, od_ln_b, od_w_out, od_b_out):
    bsz, t, d = x.shape
    nctx = ctx.shape[1]
    a_width = d // 2
    a_heads = a_width // A_DV
    a_qkv = a_heads * (2 * A_DK + A_DV)
    b_width = d - a_width
    b_heads = b_width // B_HD
    b_kv = b_heads // 2
    assert ada_w.shape[0] == 2 and ev_w_in.shape[0] == 1 and od_w_in.shape[0] == 1

    mod_rows = ((bsz + 1 + 7) // 8) * 8
    cond = jnp.zeros((mod_rows, d), F32).at[:bsz].set(c).at[bsz].set(c_ctx)
    mod = _ada(cond, ada_w, ada_b)

    w_in = ev_w_in[0]
    splits = np.cumsum([0, a_qkv, a_width, 2 * a_heads, 2 * a_heads, b_heads * B_HD, b_kv * B_HD, b_kv * B_HD,
                        b_width])
    w_qkv, w_za, w_a, w_b, w_qb, w_kb, w_vb, w_zb = [w_in[:, splits[j]:splits[j + 1]] for j in range(8)]
    perm = _head_perm()
    perm_q = np.concatenate([h * B_HD + perm for h in range(b_heads)])
    perm_k = np.concatenate([h * B_HD + perm for h in range(b_kv)])
    ab_pad = 128 - 4 * a_heads
    pieces = [("qkv", w_qkv), ("za", w_za), ("qb", w_qb[:, perm_q]), ("kb", w_kb[:, perm_k]), ("vb", w_vb),
              ("zb", w_zb), ("ab", jnp.concatenate([w_a, w_b, jnp.zeros((d, ab_pad), F32)], axis=1))]
    secs, off = {}, 0
    for name, wpart in pieces:
        secs[name] = (off, off + wpart.shape[1])
        off += wpart.shape[1]
    w2 = jnp.concatenate([p[1] for p in pieces], axis=1).astype(BF16)
    cs_tab, sn_tab = _rope_tables(t, nctx)
    qg = ev_q_norm_g[0][perm].reshape(1, B_HD)
    kg = ev_k_norm_g[0][perm].reshape(1, B_HD)
    alog = jnp.zeros((1, 128), F32).at[0, :2 * a_heads].set(ev_a_log[0].reshape(-1))
    dtb = jnp.zeros((1, 128), F32).at[0, :2 * a_heads].set(ev_dt_bias[0].reshape(-1))

    tm0 = nctx
    qkv_raw, sza, qb, kb, vb, szb, gb = _inproj0(x, ctx, mod[0], pre_norm_g[0], w2, secs, cs_tab, sn_tab, qg, kg,
                                                 alog, dtb, tm0)

    qkvn = _gdnconv(qkv_raw, ev_short_conv_w[0], tm0, t // tm0, a_heads)
    tall = t + nctx
    n_chunks = tall // GDN_CHUNK
    ncp = ((n_chunks + 7) // 8) * 8
    grow = gb.reshape(bsz, n_chunks, GDN_CHUNK, 4, a_heads).transpose(0, 4, 3, 1, 2)
    grow = jnp.pad(grow, ((0, 0), (0, 0), (0, 0), (0, ncp - n_chunks), (0, 0)))
    ya = _gdn(qkvn, grow, sza, ev_gdn_norm_g[0].reshape(1, A_DV), a_heads, t)

    yb = _attn(qb, kb, vb, szb, t, 512)

    w_out0 = ev_w_out[0].astype(BF16)
    xl1, u1, sz1 = _mid(ya, yb, x, w_out0[:a_width], w_out0[a_width:], post_norm_g[0], mod[0], mod[1],
                        pre_norm_g[1], od_w_in[0].astype(BF16), od_b_in[0], 512)
    return _tail(u1, sz1, xl1, od_dw_w[0], od_dw_b[0], od_ln_g[0], od_ln_b[0], od_w_out[0].astype(BF16),
                 od_b_out[0], post_norm_g[1], mod[1], 256)
```

```python
import functools
import math

import numpy as np
import jax
import jax.numpy as jnp
from jax import lax
from jax.experimental import pallas as pl
from jax.experimental.pallas import tpu as pltpu

F32 = jnp.float32
BF16 = jnp.bfloat16
HIGHEST = lax.Precision.HIGHEST

GRID_W = 64
A_DK = 128
A_DV = 128
SHORT_CONV = 5
GDN_CHUNK = 64
INV_BLOCK = 16
GDN_HEADS_PER_PROGRAM = 2
B_HD = 128
ROPE_THETA = 10000.0
CONV_K = 31
NORM_EPS = 1e-6
LN_EPS = 1e-5
NEG_BIG = -1e30

VMEM_LIMIT = 56 * 1024 * 1024


def _sigmoid(x):
    return 1.0 / (1.0 + jnp.exp(-x))


def _silu(x):
    return x * _sigmoid(x)


def _softplus(x):
    return jnp.maximum(x, 0.0) + jnp.log(1.0 + jnp.exp(-jnp.abs(x)))


def _cparams(sem):
    return pltpu.CompilerParams(dimension_semantics=sem, vmem_limit_bytes=VMEM_LIMIT)


def _ada_kernel(c_ref, w_ref, b_ref, o_ref):
    s = _silu(c_ref[...])
    o_ref[0] = jnp.dot(s, w_ref[0], preferred_element_type=F32, precision=HIGHEST) + b_ref[0]


def _ada(cond, ada_w, ada_b):
    depth, d, d3 = ada_w.shape
    rows = cond.shape[0]
    nt = d3 // d
    return pl.pallas_call(
        _ada_kernel,
        out_shape=jax.ShapeDtypeStruct((depth, rows, d3), F32),
        grid=(depth, nt),
        in_specs=[
            pl.BlockSpec((rows, d), lambda l, j: (0, 0)),
            pl.BlockSpec((1, d, d), lambda l, j: (l, 0, j)),
            pl.BlockSpec((1, 1, d), lambda l, j: (l, 0, j)),
        ],
        out_specs=pl.BlockSpec((1, rows, d), lambda l, j: (l, 0, j)),
        compiler_params=_cparams(("parallel", "parallel")),
        name="ada",
    )(cond, ada_w, ada_b.reshape(depth, 1, d3))


def _inproj0_kernel(x_ref, xp_ref, xn_ref, ctx_ref, sh_ref, sc_ref, g_ref, w_ref, cw_ref, cs_ref, sn_ref, qg_ref,
                    kg_ref, alog_ref, dtb_ref,
                    qkv_o, za_o, qb_o, kb_o, vb_o, zb_o, gb_o, h_scr, *, n_lat_tiles, secs, halo, n_gdn_heads):
    i = pl.program_id(1)
    tm = x_ref.shape[1]
    mult = g_ref[...] * (1.0 + sc_ref[0])
    shift = sh_ref[0]

    def modulated(xf):
        rs = lax.rsqrt(jnp.mean(xf * xf, axis=-1, keepdims=True) + NORM_EPS)
        return xf * rs * mult + shift

    prev_ok = jnp.logical_and(i > 0, i < n_lat_tiles)
    next_ok = i < n_lat_tiles - 1
    h_scr[0:halo, :] = jnp.where(prev_ok, modulated(xp_ref[0]), 0.0).astype(BF16)
    h_scr[halo + tm:2 * halo + tm, :] = jnp.where(next_ok, modulated(xn_ref[0]), 0.0).astype(BF16)

    @pl.when(i < n_lat_tiles)
    def _():
        h_scr[halo:halo + tm, :] = modulated(x_ref[0]).astype(BF16)

    @pl.when(i >= n_lat_tiles)
    def _():
        h_scr[halo:halo + tm, :] = modulated(ctx_ref[0]).astype(BF16)

    h = h_scr[halo:halo + tm, :]

    def proj(name):
        c0, c1 = secs[name]
        return lambda: jnp.dot(h, w_ref[:, c0:c1], preferred_element_type=F32)

    pad = SHORT_CONV // 2
    pw = 2 * A_DK
    hpb = pw // A_DK

    def qkv_mm(pb):
        c0 = secs["qkv"][0] + pb * pw
        return lambda: jnp.dot(h_scr[...], w_ref[:, c0:c0 + pw], preferred_element_type=F32)

    def qkv_epi(pb):
        def epi(y):
            for sb in range(hpb):
                hb = pb * hpb + sb
                cl = slice(hb * A_DK, (hb + 1) * A_DK)
                acc = None
                for j in range(SHORT_CONV):
                    term = y[halo - pad + j:halo - pad + j + tm, sb * A_DK:(sb + 1) * A_DK] * cw_ref[j:j + 1, cl]
                    acc = term if acc is None else acc + term
                u = _silu(acc)
                if hb < 2 * n_gdn_heads:
                    u = u * lax.rsqrt(jnp.sum(u * u, axis=-1, keepdims=True) + NORM_EPS)
                    if hb < n_gdn_heads:
                        u = u * (A_DK ** -0.5)
                qkv_o[0, :, cl] = u.astype(BF16)
        return epi

    def gate_epi(o_ref):
        def epi(y):
            o_ref[0] = _silu(y).astype(BF16)
        return epi

    def plain_epi(y):
        vb_o[0] = y.astype(BF16)

    def norm_rope_epi(o_ref, g_ref_, out_scale):
        def epi(y):
            cs = cs_ref[...]
            sn = sn_ref[...]
            g = g_ref_[...]
            for hh in range(y.shape[1] // B_HD):
                sl = slice(hh * B_HD, (hh + 1) * B_HD)
                yh = y[:, sl]
                ms = jnp.mean(yh * yh, axis=-1, keepdims=True)
                yn = yh * lax.rsqrt(ms + NORM_EPS) * g
                o_ref[0, :, sl] = ((yn * cs + pltpu.roll(yn, B_HD // 2, 1) * sn) * out_scale).astype(BF16)
        return epi

    def decay_epi(ab):
        nab = gb_o.shape[2]
        gval = -jnp.exp(alog_ref[...]) * _softplus(ab + dtb_ref[...])
        bval = _sigmoid(ab)
        lane = lax.broadcasted_iota(jnp.int32, ab.shape, 1)
        gb_o[0] = jnp.where(lane < nab // 2, gval, bval)[:, :nab]

    n_qkv_blocks = (secs["qkv"][1] - secs["qkv"][0]) // pw
    heavy = [(qkv_mm(pb), qkv_epi(pb)) for pb in range(n_qkv_blocks)]
    light = [(proj("za"), gate_epi(za_o)), (proj("zb"), gate_epi(zb_o)), (proj("vb"), plain_epi),
             (proj("qb"), norm_rope_epi(qb_o, qg_ref, B_HD ** -0.5)), (proj("kb"), norm_rope_epi(kb_o, kg_ref, 1.0)),
             (proj("ab"), decay_epi)]
    stages = []
    for k in range(max(len(heavy), len(light))):
        stages += heavy[k:k + 1] + light[k:k + 1]
    y_next = stages[0][0]()
    for k, (_, epi) in enumerate(stages):
        y = y_next
        if k + 1 < len(stages):
            y_next = stages[k + 1][0]()
        epi(y)


def _inproj0(x, ctx, mod0, pre_g, w2, secs, conv_w, cs_tab, sn_tab, qg, kg, alog, dtb, tm, n_gdn_heads):
    bsz, t, d = x.shape
    nctx = ctx.shape[1]
    tall = t + nctx
    assert t % tm == 0 and nctx == tm
    n_lat = t // tm
    ntile = n_lat + 1
    nab = 16
    halo = 16
    nhb = tm // halo
    last_hblk = t // halo - 1

    def widths(name):
        return secs[name][1] - secs[name][0]

    def row_spec(c):
        return pl.BlockSpec((1, tm, c), lambda b, i: (b, i, 0))

    mod_rows = mod0.shape[0]
    mod3 = mod0.reshape(mod_rows, 1, 3 * d)
    ctx_row = bsz
    outs = [
        jax.ShapeDtypeStruct((bsz, tall, widths("qkv")), BF16),
        jax.ShapeDtypeStruct((bsz, tall, widths("za")), BF16),
        jax.ShapeDtypeStruct((bsz, tall, widths("qb")), BF16),
        jax.ShapeDtypeStruct((bsz, tall, widths("kb")), BF16),
        jax.ShapeDtypeStruct((bsz, tall, widths("vb")), BF16),
        jax.ShapeDtypeStruct((bsz, tall, widths("zb")), BF16),
        jax.ShapeDtypeStruct((bsz, tall, nab), F32),
    ]
    kern = functools.partial(_inproj0_kernel, n_lat_tiles=n_lat, secs=secs, halo=halo, n_gdn_heads=n_gdn_heads)
    return pl.pallas_call(
        kern,
        out_shape=outs,
        grid=(bsz, ntile),
        in_specs=[
            pl.BlockSpec((1, tm, d), lambda b, i: (b, jnp.minimum(i, n_lat - 1), 0)),
            pl.BlockSpec((1, halo, d), lambda b, i: (b, jnp.clip(i * nhb - 1, 0, last_hblk), 0)),
            pl.BlockSpec((1, halo, d), lambda b, i: (b, jnp.clip((i + 1) * nhb, 0, last_hblk), 0)),
            pl.BlockSpec((1, tm, d), lambda b, i: (b, 0, 0)),
            pl.BlockSpec((1, 1, d), lambda b, i: (jnp.where(i < n_lat, b, ctx_row), 0, 0)),
            pl.BlockSpec((1, 1, d), lambda b, i: (jnp.where(i < n_lat, b, ctx_row), 0, 1)),
            pl.BlockSpec((1, d), lambda b, i: (0, 0)),
            pl.BlockSpec(w2.shape, lambda b, i: (0, 0)),
            pl.BlockSpec(conv_w.shape, lambda b, i: (0, 0)),
            pl.BlockSpec((tm, B_HD), lambda b, i: (i, 0)),
            pl.BlockSpec((tm, B_HD), lambda b, i: (i, 0)),
            pl.BlockSpec((1, B_HD), lambda b, i: (0, 0)),
            pl.BlockSpec((1, B_HD), lambda b, i: (0, 0)),
            pl.BlockSpec((1, 128), lambda b, i: (0, 0)),
            pl.BlockSpec((1, 128), lambda b, i: (0, 0)),
        ],
        out_specs=[row_spec(widths("qkv")), row_spec(widths("za")), row_spec(widths("qb")), row_spec(widths("kb")),
                   row_spec(widths("vb")), row_spec(widths("zb")), row_spec(nab)],
        scratch_shapes=[pltpu.VMEM((tm + 2 * halo, d), BF16)],
        compiler_params=_cparams(("parallel", "arbitrary")),
        name="inproj0",
    )(x, x, x, ctx, mod3, mod3, pre_g.reshape(1, d), w2, conv_w, cs_tab, sn_tab, qg, kg, alog, dtb)


def _gdn_kernel(q_ref, k_ref, v_ref, gr_ref, sz_ref, ng_ref, o_ref,
                nq_scr, z_scr, au_scr, gt_scr, cr_scr, s_scr, oacc, *, n_lat_chunks, n_chunks, group, ctx_group):
    C = GDN_CHUNK
    hp = q_ref.shape[2] // A_DK
    ri = lax.broadcasted_iota(jnp.int32, (C, C), 0)
    ci = lax.broadcasted_iota(jnp.int32, (C, C), 1)
    incl = (ri >= ci, ri <= ci)
    strict = (ri > ci, ri < ci)
    eye = (ri == ci).astype(F32)
    tri_row = ((ri <= ci).astype(F32), (ri >= ci).astype(F32))
    tri_col = (incl[0].astype(F32), incl[1].astype(F32))

    for hh in range(hp):
        for d in range(2):
            cr_scr[2 * hh + d] = jnp.dot(gr_ref[0, hh, d], tri_row[d], preferred_element_type=F32,
                                         precision=HIGHEST)

    nt = (((1,), (1,)), ((), ()))
    nsq = int(math.log2(INV_BLOCK)) - 1
    diag_blk = (ri // INV_BLOCK) == (ci // INV_BLOCK)
    off_blks = []
    bs = INV_BLOCK
    while bs < C:
        off_blks.append(jnp.logical_and((ri // (2 * bs)) == (ci // (2 * bs)), (ri // bs) != (ci // bs)))
        bs *= 2

    def mm(a, b):
        return jnp.dot(a, b, preferred_element_type=F32)

    tl = (((0,), (0,)), ((), ()))

    def prep(hh, chunk_ids):
        hc = slice(hh * A_DK, (hh + 1) * A_DK)
        ch = []
        for n in chunk_ids:
            rows = pl.ds(pl.multiple_of(n * C, C), C)
            kb = k_ref[0, rows, hc]
            qb = q_ref[0, rows, hc]
            k32 = kb.astype(F32)
            q32 = qb.astype(F32)
            v32 = v_ref[0, rows, hc].astype(F32)
            kk = lax.dot_general(kb, kb, nt, preferred_element_type=F32)
            qk = lax.dot_general(qb, kb, nt, preferred_element_type=F32)
            for dd in range(2):
                d = dd
                g_row = gr_ref[0, hh, d, pl.ds(n, 1), :]
                beta_row = gr_ref[0, hh, 2 + d, pl.ds(n, 1), :]
                c_row = cr_scr[2 * hh + d, pl.ds(n, 1), :]
                c_col = jnp.sum(tri_col[d] * g_row, axis=-1, keepdims=True)
                beta_col = jnp.sum(eye * beta_row, axis=-1, keepdims=True)
                tot = jnp.sum(g_row, axis=-1, keepdims=True)
                decay = jnp.exp(jnp.where(incl[d], c_col - c_row, NEG_BIG))
                x = jnp.where(strict[d], kk * decay, 0.0) * (-beta_col)
                e_col = jnp.exp(c_col)
                rhs = jnp.concatenate([v32 * beta_col, k32 * (beta_col * e_col)], axis=1)
                qd = q32 * e_col
                kd = (k32 * jnp.exp(tot - c_col)).astype(BF16)
                am = jnp.where(incl[d], qk * decay, 0.0).astype(BF16)
                gt_scr[2 * hh + d, n] = jnp.broadcast_to(jnp.exp(tot), (8, 128))
                ch.append((2 * hh + d, n, rows, x, rhs, qd, kd, am))
        xs = [c[3] for c in ch]
        ys = [jnp.where(diag_blk, x, 0.0) for x in xs]
        xbs = [y.astype(BF16) for y in ys]
        ps = [mm(xb, xb) for xb in xbs]
        for m in range(nsq):
            pbs = [p.astype(BF16) for p in ps]
            ybs = [y.astype(BF16) for y in ys]
            if m < nsq - 1:
                rs = [mm(jnp.concatenate([pb, yb], axis=0), pb) for pb, yb in zip(pbs, ybs)]
                ys = [y + p + r[C:] for y, p, r in zip(ys, ps, rs)]
                ps = [r[:C] for r in rs]
            else:
                ys = [y + p + mm(yb, pb) for y, p, yb, pb in zip(ys, ps, ybs, pbs)]
        for off in off_blks:
            xos = [jnp.where(off, x, 0.0) for x in xs]
            ybs = [y.astype(BF16) for y in ys]
            ts = [xo + mm(yb, xo.astype(BF16)) for xo, yb in zip(xos, ybs)]
            ys = [y + t + mm(t.astype(BF16), yb) for y, t, yb in zip(ys, ts, ybs)]
        ybs = [y.astype(BF16) for y in ys]
        uws = [c[4] + mm(yb, c[4].astype(BF16)) for c, yb in zip(ch, ybs)]
        uwbs = [uw.astype(BF16) for uw in uws]
        kzs = [lax.dot_general(c[6], uwb, tl, preferred_element_type=F32) for c, uwb in zip(ch, uwbs)]
        azs = [mm(c[7], uwb) for c, uwb in zip(ch, uwbs)]
        for c, kz, az in zip(ch, kzs, azs):
            cd, n, rows = c[0], c[1], c[2]
            z_scr[cd, n] = kz[:, :A_DV].astype(z_scr.dtype)
            nq_scr[cd, n, 0:A_DK, :] = kz[:, A_DV:].astype(BF16)
            nq_scr[cd, n, A_DK:A_DK + C, :] = (c[5] - az[:, A_DV:]).astype(BF16)
            au_scr[cd, rows, :] = az[:, :A_DV].astype(au_scr.dtype)

    n_ctx_chunks = n_chunks - n_lat_chunks
    for hh in range(hp):
        def prep_lat(gi, carry, hh=hh):
            prep(hh, [gi * group + i for i in range(group)])
            return carry

        def prep_ctx(gi, carry, hh=hh):
            prep(hh, [n_lat_chunks + gi * ctx_group + i for i in range(ctx_group)])
            return carry

        lax.fori_loop(0, n_lat_chunks // group, prep_lat, 0)
        lax.fori_loop(0, n_ctx_chunks // ctx_group, prep_ctx, 0)

    s_scr[...] = jnp.zeros_like(s_scr)
    chains = range(2 * hp)

    def steps(ns, with_out):
        nrow = A_DK + C if with_out else A_DK
        ss = [s_scr[cd] for cd in chains]
        rs = [mm(nq_scr[cd, ns[cd % 2], 0:nrow, :], ss[cd].astype(BF16)) for cd in chains]
        for cd in chains:
            n = ns[cd % 2]
            s_scr[cd] = ss[cd] * gt_scr[cd, n][0:1, :] - rs[cd][:A_DK] + z_scr[cd, n].astype(F32)
        if with_out:
            return [rs[cd][A_DK:] + au_scr[cd, pl.ds(pl.multiple_of(ns[cd % 2] * C, C), C), :].astype(F32)
                    for cd in chains]
        return None

    def ctx_body(sidx, carry):
        steps((n_lat_chunks + sidx, n_chunks - 1 - sidx), False)
        return carry

    lax.fori_loop(0, n_ctx_chunks, ctx_body, 0)

    def lat_body(sidx, carry):
        nf = sidx
        nb = n_lat_chunks - 1 - sidx
        outs = steps((nf, nb), True)
        rr = (pl.ds(pl.multiple_of(nf * C, C), C), pl.ds(pl.multiple_of(nb * C, C), C))

        @pl.when(nf < nb)
        def _():
            for cd in chains:
                oacc[rr[cd % 2], (cd // 2) * A_DV:(cd // 2 + 1) * A_DV] = outs[cd]

        @pl.when(nf > nb)
        def _():
            for cd in chains:
                hc = slice((cd // 2) * A_DV, (cd // 2 + 1) * A_DV)
                oacc[rr[cd % 2], hc] = oacc[rr[cd % 2], hc] + outs[cd]

        return carry

    lax.fori_loop(0, n_lat_chunks, lat_body, 0)

    ng = ng_ref[...]
    blk = 512

    def fin(j, carry):
        rows = pl.ds(pl.multiple_of(j * blk, blk), blk)
        for hh in range(hp):
            hc = slice(hh * A_DV, (hh + 1) * A_DV)
            o = oacc[rows, hc]
            on = o * lax.rsqrt(jnp.mean(o * o, axis=-1, keepdims=True) + NORM_EPS) * ng
            o_ref[0, rows, hc] = (on * sz_ref[0, rows, hc].astype(F32)).astype(BF16)
        return carry

    lax.fori_loop(0, (n_lat_chunks * C) // blk, fin, 0)


def _gdn(qkvn, grow, sza, norm_g, n_heads, t):
    bsz, tall, _ = qkvn.shape
    C = GDN_CHUNK
    n_chunks = tall // C
    n_lat_chunks = t // C
    ncp = grow.shape[3]
    group, ctx_group = 8, 4
    assert n_lat_chunks % group == 0 and (n_chunks - n_lat_chunks) % ctx_group == 0 and n_lat_chunks % 2 == 0
    kern = functools.partial(_gdn_kernel, n_lat_chunks=n_lat_chunks, n_chunks=n_chunks, group=group,
                             ctx_group=ctx_group)
    hp = GDN_HEADS_PER_PROGRAM
    assert n_heads % hp == 0
    ng = n_heads // hp
    nch = 2 * hp
    once = pl.Buffered(1)
    return pl.pallas_call(
        kern,
        out_shape=jax.ShapeDtypeStruct((bsz, t, n_heads * A_DV), BF16),
        grid=(bsz, ng),
        in_specs=[
            pl.BlockSpec((1, tall, hp * A_DK), lambda b, h: (b, 0, h), pipeline_mode=once),
            pl.BlockSpec((1, tall, hp * A_DK), lambda b, h: (b, 0, ng + h), pipeline_mode=once),
            pl.BlockSpec((1, tall, hp * A_DV), lambda b, h: (b, 0, 2 * ng + h), pipeline_mode=once),
            pl.BlockSpec((1, hp, 4, ncp, C), lambda b, h: (b, h, 0, 0, 0)),
            pl.BlockSpec((1, t, hp * A_DV), lambda b, h: (b, 0, h), pipeline_mode=once),
            pl.BlockSpec((1, A_DV), lambda b, h: (0, 0)),
        ],
        out_specs=pl.BlockSpec((1, t, hp * A_DV), lambda b, h: (b, 0, h)),
        scratch_shapes=[
            pltpu.VMEM((nch, n_chunks, A_DK + C, A_DK), BF16),
            pltpu.VMEM((nch, n_chunks, A_DK, A_DV), BF16),
            pltpu.VMEM((nch, tall, A_DV), BF16),
            pltpu.VMEM((nch, n_chunks, 8, 128), F32),
            pltpu.VMEM((nch, ncp, C), F32),
            pltpu.VMEM((nch, A_DK, A_DV), F32),
            pltpu.VMEM((t, hp * A_DV), F32),
        ],
        compiler_params=_cparams(("parallel", "parallel")),
        name="gdn",
    )(qkvn, qkvn, qkvn, grow, sza, norm_g)


def _attn_kernel(q_ref, k_ref, v_ref, sz_ref, o_ref, *, kv_blocks):
    nt = (((1,), (1,)), ((), ()))
    n_heads = q_ref.shape[2] // B_HD
    items = [(hh, blk) for hh in range(n_heads) for blk in kv_blocks]

    def scores(item):
        hh, (k0, k1) = item
        q = q_ref[0, :, hh * B_HD:(hh + 1) * B_HD]
        return lax.dot_general(q, k_ref[0, k0:k1, :], nt, preferred_element_type=F32)

    s_next = scores(items[0])
    m = l = acc = None
    for idx, (hh, (k0, k1)) in enumerate(items):
        s = s_next
        if idx + 1 < len(items):
            s_next = scores(items[idx + 1])
        bm = jnp.max(s, axis=-1, keepdims=True)
        if k0 == kv_blocks[0][0]:
            m = bm
            p = jnp.exp(s - m)
            l = jnp.sum(p, axis=-1, keepdims=True)
            acc = jnp.dot(p.astype(BF16), v_ref[0, k0:k1, :], preferred_element_type=F32)
        else:
            m_new = jnp.maximum(m, bm)
            alpha = jnp.exp(m - m_new)
            p = jnp.exp(s - m_new)
            l = alpha * l + jnp.sum(p, axis=-1, keepdims=True)
            acc = alpha * acc + jnp.dot(p.astype(BF16), v_ref[0, k0:k1, :], preferred_element_type=F32)
            m = m_new
        if k1 == kv_blocks[-1][1]:
            sl = slice(hh * B_HD, (hh + 1) * B_HD)
            o_ref[0, :, sl] = (acc * (1.0 / l) * sz_ref[0, :, sl].astype(F32)).astype(BF16)


def _attn(qb, kb, vb, szb, t, tq):
    bsz, tall, qw = qb.shape
    n_kv = kb.shape[2] // B_HD
    gw = qw // n_kv
    kvb = 1024
    edges = list(range(0, tall, kvb)) + [tall]
    kv_blocks = tuple((edges[j], edges[j + 1]) for j in range(len(edges) - 1))
    return pl.pallas_call(
        functools.partial(_attn_kernel, kv_blocks=kv_blocks),
        out_shape=jax.ShapeDtypeStruct((bsz, t, qw), BF16),
        grid=(bsz, n_kv, t // tq),
        in_specs=[
            pl.BlockSpec((1, tq, gw), lambda b, g, i: (b, i, g)),
            pl.BlockSpec((1, tall, B_HD), lambda b, g, i: (b, 0, g)),
            pl.BlockSpec((1, tall, B_HD), lambda b, g, i: (b, 0, g)),
            pl.BlockSpec((1, tq, gw), lambda b, g, i: (b, i, g)),
        ],
        out_specs=pl.BlockSpec((1, tq, gw), lambda b, g, i: (b, i, g)),
        compiler_params=_cparams(("parallel", "parallel", "arbitrary")),
        name="attn",
    )(qb, kb, vb, szb)


def _mid_kernel(ya_ref, yb_ref, x_ref, wa_ref, wb_ref, pg_ref, gate_ref, sh_ref, sc_ref, g1_ref, w1_ref, b1_ref,
                xl_o, u_o, sz_o, h_scr):
    d = x_ref.shape[2]
    out = (jnp.dot(ya_ref[0], wa_ref[...], preferred_element_type=F32)
           + jnp.dot(yb_ref[0], wb_ref[...], preferred_element_type=F32))
    on = out * lax.rsqrt(jnp.mean(out * out, axis=-1, keepdims=True) + NORM_EPS) * pg_ref[...]
    xl = x_ref[0] + gate_ref[0] * on
    xl_o[0] = xl
    rs = lax.rsqrt(jnp.mean(xl * xl, axis=-1, keepdims=True) + NORM_EPS)
    h_scr[...] = (xl * rs * (g1_ref[...] * (1.0 + sc_ref[0])) + sh_ref[0]).astype(BF16)
    h = h_scr[...]
    a = jnp.dot(h, w1_ref[:, 0:d], preferred_element_type=F32) + b1_ref[:, 0:d]
    gl = jnp.dot(h, w1_ref[:, d:2 * d], preferred_element_type=F32) + b1_ref[:, d:2 * d]
    u_o[0] = (a * _sigmoid(gl)).astype(BF16)
    z = jnp.dot(h, w1_ref[:, 2 * d:3 * d], preferred_element_type=F32) + b1_ref[:, 2 * d:3 * d]
    sz_o[0] = _silu(z).astype(BF16)


def _mid(ya, yb, x, wa, wb, post_g, mod0, mod1, pre_g1, w1, b1, tm):
    bsz, t, d = x.shape
    aw = ya.shape[2]
    bw = yb.shape[2]
    rows = mod0.shape[0]
    m0 = mod0.reshape(rows, 1, 3 * d)
    m1 = mod1.reshape(rows, 1, 3 * d)
    tok = lambda c: pl.BlockSpec((1, tm, c), lambda b, i: (b, i, 0))
    const2 = lambda a: pl.BlockSpec(a.shape, lambda b, i: (0, 0))
    return pl.pallas_call(
        _mid_kernel,
        out_shape=[jax.ShapeDtypeStruct((bsz, t, d), F32),
                   jax.ShapeDtypeStruct((bsz, t, d), BF16),
                   jax.ShapeDtypeStruct((bsz, t, d), BF16)],
        grid=(bsz, t // tm),
        in_specs=[
            tok(aw), tok(bw), tok(d),
            const2(wa), const2(wb),
            pl.BlockSpec((1, d), lambda b, i: (0, 0)),
            pl.BlockSpec((1, 1, d), lambda b, i: (b, 0, 2)),
            pl.BlockSpec((1, 1, d), lambda b, i: (b, 0, 0)),
            pl.BlockSpec((1, 1, d), lambda b, i: (b, 0, 1)),
            pl.BlockSpec((1, d), lambda b, i: (0, 0)),
            const2(w1),
            pl.BlockSpec((1, 3 * d), lambda b, i: (0, 0)),
        ],
        out_specs=[tok(d), tok(d), tok(d)],
        scratch_shapes=[pltpu.VMEM((tm, d), BF16)],
        compiler_params=_cparams(("parallel", "parallel")),
        name="mid",
    )(ya, yb, x, wa, wb, post_g.reshape(1, d), m0, m1, m1, pre_g1.reshape(1, d), w1, b1.reshape(1, 3 * d))


def _tail_kernel(cur_ref, prev_ref, next_ref, sz_ref, xl_ref, dw_ref, dwb_ref, lng_ref, lnb_ref, wo_ref, bo_ref,
                 pg_ref, gate_ref, o_ref, buf, act, *, halo):
    i = pl.program_id(1)
    nt = pl.num_programs(1)
    tm = cur_ref.shape[1]
    d = cur_ref.shape[2]
    buf[0:halo, :] = jnp.where(i > 0, prev_ref[0].astype(F32), 0.0)
    buf[halo:halo + tm, :] = cur_ref[0].astype(F32)
    buf[halo + tm:2 * halo + tm, :] = jnp.where(i < nt - 1, next_ref[0].astype(F32), 0.0)
    pad = CONV_K // 2
    cw = 128
    sub = 8
    for cb in range(d // cw):
        cs = slice(cb * cw, (cb + 1) * cw)
        acc = None
        for s in range(sub):
            part = None
            for a in range((halo + pad) // sub + 1):
                j = sub * a + s - (halo - pad)
                if 0 <= j < CONV_K:
                    term = buf[sub * a:sub * a + tm + sub, cs] * dw_ref[j:j + 1, cs]
                    part = term if part is None else part + term
            if part is not None:
                acc = part[s:s + tm] if acc is None else acc + part[s:s + tm]
        act[:, cs] = acc + dwb_ref[:, cs]
    u = act[...]
    mu = jnp.mean(u, axis=-1, keepdims=True)
    uc = u - mu
    var = jnp.mean(uc * uc, axis=-1, keepdims=True)
    un = uc * lax.rsqrt(var + LN_EPS) * lng_ref[...] + lnb_ref[...]
    hact = (_silu(un) * sz_ref[0].astype(F32)).astype(BF16)
    out = jnp.dot(hact, wo_ref[...], preferred_element_type=F32) + bo_ref[...]
    on = out * lax.rsqrt(jnp.mean(out * out, axis=-1, keepdims=True) + NORM_EPS) * pg_ref[...]
    o_ref[0] = xl_ref[0] + gate_ref[0] * on


def _tail(u, sz, xl, dw_w, dw_b, ln_g, ln_b, w_out, b_out, post_g, mod1, tm):
    bsz, t, d = xl.shape
    halo = 16
    nhb = tm // halo
    last_blk = t // halo - 1
    rows = mod1.shape[0]
    m1 = mod1.reshape(rows, 1, 3 * d)
    tok = lambda c: pl.BlockSpec((1, tm, c), lambda b, i: (b, i, 0))
    vec = lambda: pl.BlockSpec((1, d), lambda b, i: (0, 0))
    kern = functools.partial(_tail_kernel, halo=halo)
    return pl.pallas_call(
        kern,
        out_shape=jax.ShapeDtypeStruct((bsz, t, d), F32),
        grid=(bsz, t // tm),
        in_specs=[
            tok(d),
            pl.BlockSpec((1, halo, d), lambda b, i: (b, jnp.maximum(i * nhb - 1, 0), 0)),
            pl.BlockSpec((1, halo, d), lambda b, i: (b, jnp.minimum((i + 1) * nhb, last_blk), 0)),
            tok(d), tok(d),
            pl.BlockSpec(dw_w.shape, lambda b, i: (0, 0)),
            vec(), vec(), vec(),
            pl.BlockSpec(w_out.shape, lambda b, i: (0, 0)),
            vec(), vec(),
            pl.BlockSpec((1, 1, d), lambda b, i: (b, 0, 2)),
        ],
        out_specs=tok(d),
        scratch_shapes=[pltpu.VMEM((tm + 2 * halo, d), F32), pltpu.VMEM((tm, d), F32)],
        compiler_params=_cparams(("parallel", "parallel")),
        name="tail",
    )(u, u, u, sz, xl, dw_w, dw_b.reshape(1, d), ln_g.reshape(1, d), ln_b.reshape(1, d), w_out,
      b_out.reshape(1, d), post_g.reshape(1, d), m1)


def _rope_tables(t, nctx):
    rows = t // GRID_W
    row = jnp.repeat(jnp.arange(rows, dtype=F32), GRID_W)
    col = jnp.tile(jnp.arange(GRID_W, dtype=F32), rows)
    axis_dim = B_HD // 2
    inv_freq = ROPE_THETA ** (-jnp.arange(0, axis_dim, 2, dtype=F32) / axis_dim)
    ang = jnp.concatenate([row[:, None] * inv_freq, col[:, None] * inv_freq], axis=1)
    cs = jnp.concatenate([jnp.cos(ang), jnp.cos(ang)], axis=1)
    sn = jnp.concatenate([-jnp.sin(ang), jnp.sin(ang)], axis=1)
    cs = jnp.concatenate([cs, jnp.ones((nctx, B_HD), F32)], axis=0)
    sn = jnp.concatenate([sn, jnp.zeros((nctx, B_HD), F32)], axis=0)
    return cs, sn


def _head_perm():
    q = B_HD // 4
    return np.concatenate([np.arange(0, q), np.arange(2 * q, 3 * q), np.arange(q, 2 * q), np.arange(3 * q, 4 * q)])


def kernel(x, c, ctx, c_ctx, ada_w, ada_b, pre_norm_g, post_norm_g, ev_w_in, ev_short_conv_w, ev_a_log,
           ev_dt_bias, ev_gdn_norm_g, ev_q_norm_g, ev_k_norm_g, ev_w_out, od_w_in, od_b_in, od_dw_w, od_dw_b,
           od_ln_g, od_ln_b, od_w_out, od_b_out):
    bsz, t, d = x.shape
    nctx = ctx.shape[1]
    a_width = d // 2
    a_heads = a_width // A_DV
    a_qkv = a_heads * (2 * A_DK + A_DV)
    b_width = d - a_width
    b_heads = b_width // B_HD
    b_kv = b_heads // 2
    assert ada_w.shape[0] == 2 and ev_w_in.shape[0] == 1 and od_w_in.shape[0] == 1

    mod_rows = ((bsz + 1 + 7) // 8) * 8
    cond = jnp.zeros((mod_rows, d), F32).at[:bsz].set(c).at[bsz].set(c_ctx)
    mod = _ada(cond, ada_w, ada_b)

    w_in = ev_w_in[0]
    splits = np.cumsum([0, a_qkv, a_width, 2 * a_heads, 2 * a_heads, b_heads * B_HD, b_kv * B_HD, b_kv * B_HD,
                        b_width])
    w_qkv, w_za, w_a, w_b, w_qb, w_kb, w_vb, w_zb = [w_in[:, splits[j]:splits[j + 1]] for j in range(8)]
    perm = _head_perm()
    perm_q = np.concatenate([h * B_HD + perm for h in range(b_heads)])
    perm_k = np.concatenate([h * B_HD + perm for h in range(b_kv)])
    ab_pad = 128 - 4 * a_heads
    pieces = [("qkv", w_qkv), ("za", w_za), ("qb", w_qb[:, perm_q]), ("kb", w_kb[:, perm_k]), ("vb", w_vb),
              ("zb", w_zb), ("ab", jnp.concatenate([w_a, w_b, jnp.zeros((d, ab_pad), F32)], axis=1))]
    secs, off = {}, 0
    for name, wpart in pieces:
        secs[name] = (off, off + wpart.shape[1])
        off += wpart.shape[1]
    w2 = jnp.concatenate([p[1] for p in pieces], axis=1).astype(BF16)
    cs_tab, sn_tab = _rope_tables(t, nctx)
    qg = ev_q_norm_g[0][perm].reshape(1, B_HD)
    kg = ev_k_norm_g[0][perm].reshape(1, B_HD)
    alog = jnp.zeros((1, 128), F32).at[0, :2 * a_heads].set(ev_a_log[0].reshape(-1))
    dtb = jnp.zeros((1, 128), F32).at[0, :2 * a_heads].set(ev_dt_bias[0].reshape(-1))

    tm0 = nctx
    qkvn, sza, qb, kb, vb, szb, gb = _inproj0(x, ctx, mod[0], pre_norm_g[0], w2, secs, ev_short_conv_w[0], cs_tab,
                                              sn_tab, qg, kg, alog, dtb, tm0, a_heads)

    tall = t + nctx
    n_chunks = tall // GDN_CHUNK
    ncp = ((n_chunks + 7) // 8) * 8
    grow = gb.reshape(bsz, n_chunks, GDN_CHUNK, 4, a_heads).transpose(0, 4, 3, 1, 2)
    grow = jnp.pad(grow, ((0, 0), (0, 0), (0, 0), (0, ncp - n_chunks), (0, 0)))
    ya = _gdn(qkvn, grow, sza, ev_gdn_norm_g[0].reshape(1, A_DV), a_heads, t)

    yb = _attn(qb, kb, vb, szb, t, 512)

    w_out0 = ev_w_out[0].astype(BF16)
    xl1, u1, sz1 = _mid(ya, yb, x, w_out0[:a_width], w_out0[a_width:], post_norm_g[0], mod[0], mod[1],
                        pre_norm_g[1], od_w_in[0].astype(BF16), od_b_in[0], 512)
    return _tail(u1, sz1, xl1, od_dw_w[0], od_dw_b[0], od_ln_g[0], od_ln_b[0], od_w_out[0].astype(BF16),
                 od_b_out[0], post_norm_g[1], mod[1], 256)
```

```python
import functools
import math

import numpy as np
import jax
import jax.numpy as jnp
from jax import lax
from jax.experimental import pallas as pl
from jax.experimental.pallas import tpu as pltpu

F32 = jnp.float32
BF16 = jnp.bfloat16
HIGHEST = lax.Precision.HIGHEST

GRID_W = 64
A_DK = 128
A_DV = 128
SHORT_CONV = 5
GDN_CHUNK = 64
INV_BLOCK = 16
B_HD = 128
ROPE_THETA = 10000.0
CONV_K = 31
NORM_EPS = 1e-6
LN_EPS = 1e-5
NEG_BIG = -1e30

VMEM_LIMIT = 56 * 1024 * 1024


def _sigmoid(x):
    return 1.0 / (1.0 + jnp.exp(-x))


def _silu(x):
    return x * _sigmoid(x)


def _softplus(x):
    return jnp.maximum(x, 0.0) + jnp.log(1.0 + jnp.exp(-jnp.abs(x)))


def _cparams(sem):
    return pltpu.CompilerParams(dimension_semantics=sem, vmem_limit_bytes=VMEM_LIMIT)


def _ada_kernel(c_ref, w_ref, b_ref, o_ref):
    s = _silu(c_ref[...])
    o_ref[0] = jnp.dot(s, w_ref[0], preferred_element_type=F32, precision=HIGHEST) + b_ref[0]


def _ada(cond, ada_w, ada_b):
    depth, d, d3 = ada_w.shape
    rows = cond.shape[0]
    nt = d3 // d
    return pl.pallas_call(
        _ada_kernel,
        out_shape=jax.ShapeDtypeStruct((depth, rows, d3), F32),
        grid=(depth, nt),
        in_specs=[
            pl.BlockSpec((rows, d), lambda l, j: (0, 0)),
            pl.BlockSpec((1, d, d), lambda l, j: (l, 0, j)),
            pl.BlockSpec((1, 1, d), lambda l, j: (l, 0, j)),
        ],
        out_specs=pl.BlockSpec((1, rows, d), lambda l, j: (l, 0, j)),
        compiler_params=_cparams(("parallel", "parallel")),
        name="ada",
    )(cond, ada_w, ada_b.reshape(depth, 1, d3))


def _inproj0_kernel(x_ref, xp_ref, xn_ref, ctx_ref, sh_ref, sc_ref, g_ref, w_ref, cw_ref, cs_ref, sn_ref, qg_ref,
                    kg_ref, alog_ref, dtb_ref,
                    qkv_o, za_o, qb_o, kb_o, vb_o, zb_o, gb_o, h_scr, *, n_lat_tiles, secs, halo, n_gdn_heads):
    i = pl.program_id(1)
    tm = x_ref.shape[1]
    mult = g_ref[...] * (1.0 + sc_ref[0])
    shift = sh_ref[0]

    def modulated(xf):
        rs = lax.rsqrt(jnp.mean(xf * xf, axis=-1, keepdims=True) + NORM_EPS)
        return xf * rs * mult + shift

    prev_ok = jnp.logical_and(i > 0, i < n_lat_tiles)
    next_ok = i < n_lat_tiles - 1
    h_scr[0:halo, :] = jnp.where(prev_ok, modulated(xp_ref[0]), 0.0).astype(BF16)
    h_scr[halo + tm:2 * halo + tm, :] = jnp.where(next_ok, modulated(xn_ref[0]), 0.0).astype(BF16)

    @pl.when(i < n_lat_tiles)
    def _():
        h_scr[halo:halo + tm, :] = modulated(x_ref[0]).astype(BF16)

    @pl.when(i >= n_lat_tiles)
    def _():
        h_scr[halo:halo + tm, :] = modulated(ctx_ref[0]).astype(BF16)

    h = h_scr[halo:halo + tm, :]

    def proj(name):
        c0, c1 = secs[name]
        return lambda: jnp.dot(h, w_ref[:, c0:c1], preferred_element_type=F32)

    pad = SHORT_CONV // 2
    pw = 2 * A_DK
    hpb = pw // A_DK

    def qkv_mm(pb):
        c0 = secs["qkv"][0] + pb * pw
        return lambda: jnp.dot(h_scr[...], w_ref[:, c0:c0 + pw], preferred_element_type=F32)

    def qkv_epi(pb):
        def epi(y):
            for sb in range(hpb):
                hb = pb * hpb + sb
                cl = slice(hb * A_DK, (hb + 1) * A_DK)
                acc = None
                for j in range(SHORT_CONV):
                    term = y[halo - pad + j:halo - pad + j + tm, sb * A_DK:(sb + 1) * A_DK] * cw_ref[j:j + 1, cl]
                    acc = term if acc is None else acc + term
                u = _silu(acc)
                if hb < 2 * n_gdn_heads:
                    u = u * lax.rsqrt(jnp.sum(u * u, axis=-1, keepdims=True) + NORM_EPS)
                    if hb < n_gdn_heads:
                        u = u * (A_DK ** -0.5)
                qkv_o[0, :, cl] = u.astype(BF16)
        return epi

    def gate_epi(o_ref):
        def epi(y):
            o_ref[0] = _silu(y).astype(BF16)
        return epi

    def plain_epi(y):
        vb_o[0] = y.astype(BF16)

    def norm_rope_epi(o_ref, g_ref_, out_scale):
        def epi(y):
            cs = cs_ref[...]
            sn = sn_ref[...]
            g = g_ref_[...]
            for hh in range(y.shape[1] // B_HD):
                sl = slice(hh * B_HD, (hh + 1) * B_HD)
                yh = y[:, sl]
                ms = jnp.mean(yh * yh, axis=-1, keepdims=True)
                yn = yh * lax.rsqrt(ms + NORM_EPS) * g
                o_ref[0, :, sl] = ((yn * cs + pltpu.roll(yn, B_HD // 2, 1) * sn) * out_scale).astype(BF16)
        return epi

    def decay_epi(ab):
        nab = gb_o.shape[2]
        gval = -jnp.exp(alog_ref[...]) * _softplus(ab + dtb_ref[...])
        bval = _sigmoid(ab)
        lane = lax.broadcasted_iota(jnp.int32, ab.shape, 1)
        gb_o[0] = jnp.where(lane < nab // 2, gval, bval)[:, :nab]

    n_qkv_blocks = (secs["qkv"][1] - secs["qkv"][0]) // pw
    heavy = [(qkv_mm(pb), qkv_epi(pb)) for pb in range(n_qkv_blocks)]
    light = [(proj("za"), gate_epi(za_o)), (proj("zb"), gate_epi(zb_o)), (proj("vb"), plain_epi),
             (proj("qb"), norm_rope_epi(qb_o, qg_ref, B_HD ** -0.5)), (proj("kb"), norm_rope_epi(kb_o, kg_ref, 1.0)),
             (proj("ab"), decay_epi)]
    stages = []
    for k in range(max(len(heavy), len(light))):
        stages += heavy[k:k + 1] + light[k:k + 1]
    y_next = stages[0][0]()
    for k, (_, epi) in enumerate(stages):
        y = y_next
        if k + 1 < len(stages):
            y_next = stages[k + 1][0]()
        epi(y)


def _inproj0(x, ctx, mod0, pre_g, w2, secs, conv_w, cs_tab, sn_tab, qg, kg, alog, dtb, tm, n_gdn_heads):
    bsz, t, d = x.shape
    nctx = ctx.shape[1]
    tall = t + nctx
    assert t % tm == 0 and nctx == tm
    n_lat = t // tm
    ntile = n_lat + 1
    nab = 16
    halo = 16
    nhb = tm // halo
    last_hblk = t // halo - 1

    def widths(name):
        return secs[name][1] - secs[name][0]

    def row_spec(c):
        return pl.BlockSpec((1, tm, c), lambda b, i: (b, i, 0))

    mod_rows = mod0.shape[0]
    mod3 = mod0.reshape(mod_rows, 1, 3 * d)
    ctx_row = bsz
    outs = [
        jax.ShapeDtypeStruct((bsz, tall, widths("qkv")), BF16),
        jax.ShapeDtypeStruct((bsz, tall, widths("za")), BF16),
        jax.ShapeDtypeStruct((bsz, tall, widths("qb")), BF16),
        jax.ShapeDtypeStruct((bsz, tall, widths("kb")), BF16),
        jax.ShapeDtypeStruct((bsz, tall, widths("vb")), BF16),
        jax.ShapeDtypeStruct((bsz, tall, widths("zb")), BF16),
        jax.ShapeDtypeStruct((bsz, tall, nab), F32),
    ]
    kern = functools.partial(_inproj0_kernel, n_lat_tiles=n_lat, secs=secs, halo=halo, n_gdn_heads=n_gdn_heads)
    return pl.pallas_call(
        kern,
        out_shape=outs,
        grid=(bsz, ntile),
        in_specs=[
            pl.BlockSpec((1, tm, d), lambda b, i: (b, jnp.minimum(i, n_lat - 1), 0)),
            pl.BlockSpec((1, halo, d), lambda b, i: (b, jnp.clip(i * nhb - 1, 0, last_hblk), 0)),
            pl.BlockSpec((1, halo, d), lambda b, i: (b, jnp.clip((i + 1) * nhb, 0, last_hblk), 0)),
            pl.BlockSpec((1, tm, d), lambda b, i: (b, 0, 0)),
            pl.BlockSpec((1, 1, d), lambda b, i: (jnp.where(i < n_lat, b, ctx_row), 0, 0)),
            pl.BlockSpec((1, 1, d), lambda b, i: (jnp.where(i < n_lat, b, ctx_row), 0, 1)),
            pl.BlockSpec((1, d), lambda b, i: (0, 0)),
            pl.BlockSpec(w2.shape, lambda b, i: (0, 0)),
            pl.BlockSpec(conv_w.shape, lambda b, i: (0, 0)),
            pl.BlockSpec((tm, B_HD), lambda b, i: (i, 0)),
            pl.BlockSpec((tm, B_HD), lambda b, i: (i, 0)),
            pl.BlockSpec((1, B_HD), lambda b, i: (0, 0)),
            pl.BlockSpec((1, B_HD), lambda b, i: (0, 0)),
            pl.BlockSpec((1, 128), lambda b, i: (0, 0)),
            pl.BlockSpec((1, 128), lambda b, i: (0, 0)),
        ],
        out_specs=[row_spec(widths("qkv")), row_spec(widths("za")), row_spec(widths("qb")), row_spec(widths("kb")),
                   row_spec(widths("vb")), row_spec(widths("zb")), row_spec(nab)],
        scratch_shapes=[pltpu.VMEM((tm + 2 * halo, d), BF16)],
        compiler_params=_cparams(("parallel", "arbitrary")),
        name="inproj0",
    )(x, x, x, ctx, mod3, mod3, pre_g.reshape(1, d), w2, conv_w, cs_tab, sn_tab, qg, kg, alog, dtb)


def _gdn_kernel(q_ref, k_ref, v_ref, gr_ref, sz_ref, ng_ref, o_ref,
                nq_scr, z_scr, au_scr, gt_scr, cr_scr, s_scr, oacc, *, n_lat_chunks, n_chunks, group):
    C = GDN_CHUNK
    step_id = pl.program_id(0)
    wset = step_id % 2
    rset = 1 - wset
    ri = lax.broadcasted_iota(jnp.int32, (C, C), 0)
    ci = lax.broadcasted_iota(jnp.int32, (C, C), 1)
    incl = (ri >= ci, ri <= ci)
    strict = (ri > ci, ri < ci)
    eye = (ri == ci).astype(F32)
    tri_row = ((ri <= ci).astype(F32), (ri >= ci).astype(F32))
    tri_col = (incl[0].astype(F32), incl[1].astype(F32))

    @pl.when(step_id == 0)
    def _():
        nq_scr[...] = jnp.zeros_like(nq_scr)
        z_scr[...] = jnp.zeros_like(z_scr)
        au_scr[...] = jnp.zeros_like(au_scr)
        gt_scr[...] = jnp.zeros_like(gt_scr)

    for d in range(2):
        cr_scr[d] = jnp.dot(gr_ref[0, 0, d], tri_row[d], preferred_element_type=F32, precision=HIGHEST)
    s_scr[...] = jnp.zeros_like(s_scr)
    oacc[...] = jnp.zeros_like(oacc)

    nt = (((1,), (1,)), ((), ()))
    nsq = int(math.log2(INV_BLOCK)) - 1
    diag_blk = (ri // INV_BLOCK) == (ci // INV_BLOCK)
    off_blks = []
    bs = INV_BLOCK
    while bs < C:
        off_blks.append(jnp.logical_and((ri // (2 * bs)) == (ci // (2 * bs)), (ri // bs) != (ci // bs)))
        bs *= 2

    def mm(a, b):
        return jnp.dot(a, b, preferred_element_type=F32)

    tl = (((0,), (0,)), ((), ()))

    def prep(chunk_ids):
        ch = []
        for n in chunk_ids:
            rows = pl.ds(pl.multiple_of(n * C, C), C)
            kb = k_ref[0, rows, :]
            qb = q_ref[0, rows, :]
            k32 = kb.astype(F32)
            q32 = qb.astype(F32)
            v32 = v_ref[0, rows, :].astype(F32)
            kk = lax.dot_general(kb, kb, nt, preferred_element_type=F32)
            qk = lax.dot_general(qb, kb, nt, preferred_element_type=F32)
            for d in range(2):
                g_row = gr_ref[0, 0, d, pl.ds(n, 1), :]
                beta_row = gr_ref[0, 0, 2 + d, pl.ds(n, 1), :]
                c_row = cr_scr[d, pl.ds(n, 1), :]
                c_col = jnp.sum(tri_col[d] * g_row, axis=-1, keepdims=True)
                beta_col = jnp.sum(eye * beta_row, axis=-1, keepdims=True)
                tot = jnp.sum(g_row, axis=-1, keepdims=True)
                decay = jnp.exp(jnp.where(incl[d], c_col - c_row, NEG_BIG))
                x = jnp.where(strict[d], kk * decay, 0.0) * (-beta_col)
                e_col = jnp.exp(c_col)
                rhs = jnp.concatenate([v32 * beta_col, k32 * (beta_col * e_col)], axis=1)
                qd = q32 * e_col
                kd = (k32 * jnp.exp(tot - c_col)).astype(BF16)
                am = jnp.where(incl[d], qk * decay, 0.0).astype(BF16)
                gt_scr[2 * wset + d, n] = jnp.broadcast_to(jnp.exp(tot), (8, 128))
                ch.append((d, n, rows, x, rhs, qd, kd, am))
        yield
        xs = [c[3] for c in ch]
        ys = [jnp.where(diag_blk, x, 0.0) for x in xs]
        xbs = [y.astype(BF16) for y in ys]
        ps = [mm(xb, xb) for xb in xbs]
        yield
        for m in range(nsq):
            pbs = [p.astype(BF16) for p in ps]
            ybs = [y.astype(BF16) for y in ys]
            if m < nsq - 1:
                rs = [mm(jnp.concatenate([pb, yb], axis=0), pb) for pb, yb in zip(pbs, ybs)]
                ys = [y + p + r[C:] for y, p, r in zip(ys, ps, rs)]
                ps = [r[:C] for r in rs]
            else:
                ys = [y + p + mm(yb, pb) for y, p, yb, pb in zip(ys, ps, ybs, pbs)]
            yield
        for off in off_blks:
            xos = [jnp.where(off, x, 0.0) for x in xs]
            ybs = [y.astype(BF16) for y in ys]
            ts = [xo + mm(yb, xo.astype(BF16)) for xo, yb in zip(xos, ybs)]
            yield
            ys = [y + t + mm(t.astype(BF16), yb) for y, t, yb in zip(ys, ts, ybs)]
            yield
        ybs = [y.astype(BF16) for y in ys]
        uws = [c[4] + mm(yb, c[4].astype(BF16)) for c, yb in zip(ch, ybs)]
        uwbs = [uw.astype(BF16) for uw in uws]
        yield
        kzs = [lax.dot_general(c[6], uwb, tl, preferred_element_type=F32) for c, uwb in zip(ch, uwbs)]
        azs = [mm(c[7], uwb) for c, uwb in zip(ch, uwbs)]
        for c, kz, az in zip(ch, kzs, azs):
            cd, n, rows = 2 * wset + c[0], c[1], c[2]
            z_scr[cd, n] = kz[:, :A_DV].astype(z_scr.dtype)
            nq_scr[cd, n, 0:A_DK, :] = kz[:, A_DV:].astype(BF16)
            nq_scr[cd, n, A_DK:A_DK + C, :] = (c[5] - az[:, A_DV:]).astype(BF16)
            au_scr[cd, rows, :] = az[:, :A_DV].astype(au_scr.dtype)
        yield

    def scan_step(ns, with_out):
        nrow = A_DK + C if with_out else A_DK
        ss = [s_scr[d] for d in range(2)]
        rs = [mm(nq_scr[2 * rset + d, ns[d], 0:nrow, :], ss[d].astype(BF16)) for d in range(2)]
        for d in range(2):
            cd, n = 2 * rset + d, ns[d]
            s_scr[d] = ss[d] * gt_scr[cd, n][0:1, :] - rs[d][:A_DK] + z_scr[cd, n].astype(F32)
            if with_out:
                rows = pl.ds(pl.multiple_of(n * C, C), C)
                oacc[rows, :] = oacc[rows, :] + rs[d][A_DK:] + au_scr[cd, rows, :].astype(F32)

    def interleave(stages, scan_steps):
        pending = list(scan_steps)
        for _ in stages:
            if pending:
                pending.pop(0)()
        for rest in pending:
            rest()

    n_ctx_chunks = n_chunks - n_lat_chunks
    interleave(prep([n_lat_chunks + i for i in range(n_ctx_chunks)]),
               [functools.partial(scan_step, (n_lat_chunks + j, n_chunks - 1 - j), False)
                for j in range(n_ctx_chunks)])

    def body(gi, carry):
        base = gi * group
        interleave(prep([base + i for i in range(group)]),
                   [functools.partial(scan_step, (base + j, n_lat_chunks - 1 - base - j), True)
                    for j in range(group)])
        return carry

    lax.fori_loop(0, n_lat_chunks // group, body, 0)

    ng = ng_ref[...]
    blk = 512

    def fin(j, carry):
        rows = pl.ds(pl.multiple_of(j * blk, blk), blk)
        o = oacc[rows, :]
        on = o * lax.rsqrt(jnp.mean(o * o, axis=-1, keepdims=True) + NORM_EPS) * ng
        o_ref[0, rows, :] = (on * sz_ref[0, rows, :].astype(F32)).astype(BF16)
        return carry

    lax.fori_loop(0, (n_lat_chunks * C) // blk, fin, 0)


def _gdn(qkvn, grow, sza, norm_g, n_heads, t):
    bsz, tall, _ = qkvn.shape
    C = GDN_CHUNK
    n_chunks = tall // C
    n_lat_chunks = t // C
    ncp = grow.shape[3]
    group = 8
    assert n_lat_chunks % group == 0
    kern = functools.partial(_gdn_kernel, n_lat_chunks=n_lat_chunks, n_chunks=n_chunks, group=group)
    n_pairs = bsz * n_heads

    def cur(s):
        p = jnp.minimum(s, n_pairs - 1)
        return p // n_heads, p % n_heads

    def prev(s):
        p = jnp.maximum(s - 1, 0)
        return p // n_heads, p % n_heads

    return pl.pallas_call(
        kern,
        out_shape=jax.ShapeDtypeStruct((bsz, t, n_heads * A_DV), BF16),
        grid=(n_pairs + 1,),
        in_specs=[
            pl.BlockSpec((1, tall, A_DK), lambda s: (cur(s)[0], 0, cur(s)[1])),
            pl.BlockSpec((1, tall, A_DK), lambda s: (cur(s)[0], 0, n_heads + cur(s)[1])),
            pl.BlockSpec((1, tall, A_DV), lambda s: (cur(s)[0], 0, 2 * n_heads + cur(s)[1])),
            pl.BlockSpec((1, 1, 4, ncp, C), lambda s: (cur(s)[0], cur(s)[1], 0, 0, 0)),
            pl.BlockSpec((1, t, A_DV), lambda s: (prev(s)[0], 0, prev(s)[1])),
            pl.BlockSpec((1, A_DV), lambda s: (0, 0)),
        ],
        out_specs=pl.BlockSpec((1, t, A_DV), lambda s: (prev(s)[0], 0, prev(s)[1])),
        scratch_shapes=[
            pltpu.VMEM((4, n_chunks, A_DK + C, A_DK), BF16),
            pltpu.VMEM((4, n_chunks, A_DK, A_DV), BF16),
            pltpu.VMEM((4, tall, A_DV), BF16),
            pltpu.VMEM((4, n_chunks, 8, 128), F32),
            pltpu.VMEM((2, ncp, C), F32),
            pltpu.VMEM((2, A_DK, A_DV), F32),
            pltpu.VMEM((t, A_DV), F32),
        ],
        compiler_params=_cparams(("arbitrary",)),
        name="gdn",
    )(qkvn, qkvn, qkvn, grow, sza, norm_g)


def _attn_kernel(q_ref, k_ref, v_ref, sz_ref, o_ref, *, kv_blocks):
    nt = (((1,), (1,)), ((), ()))
    n_heads = q_ref.shape[2] // B_HD
    items = [(hh, blk) for hh in range(n_heads) for blk in kv_blocks]

    def scores(item):
        hh, (k0, k1) = item
        q = q_ref[0, :, hh * B_HD:(hh + 1) * B_HD]
        return lax.dot_general(q, k_ref[0, k0:k1, :], nt, preferred_element_type=F32)

    s_next = scores(items[0])
    m = l = acc = None
    for idx, (hh, (k0, k1)) in enumerate(items):
        s = s_next
        if idx + 1 < len(items):
            s_next = scores(items[idx + 1])
        bm = jnp.max(s, axis=-1, keepdims=True)
        if k0 == kv_blocks[0][0]:
            m = bm
            p = jnp.exp(s - m)
            l = jnp.sum(p, axis=-1, keepdims=True)
            acc = jnp.dot(p.astype(BF16), v_ref[0, k0:k1, :], preferred_element_type=F32)
        else:
            m_new = jnp.maximum(m, bm)
            alpha = jnp.exp(m - m_new)
            p = jnp.exp(s - m_new)
            l = alpha * l + jnp.sum(p, axis=-1, keepdims=True)
            acc = alpha * acc + jnp.dot(p.astype(BF16), v_ref[0, k0:k1, :], preferred_element_type=F32)
            m = m_new
        if k1 == kv_blocks[-1][1]:
            sl = slice(hh * B_HD, (hh + 1) * B_HD)
            o_ref[0, :, sl] = (acc * (1.0 / l) * sz_ref[0, :, sl].astype(F32)).astype(BF16)


def _attn(qb, kb, vb, szb, t, tq):
    bsz, tall, qw = qb.shape
    n_kv = kb.shape[2] // B_HD
    gw = qw // n_kv
    kvb = 1024
    edges = list(range(0, tall, kvb)) + [tall]
    kv_blocks = tuple((edges[j], edges[j + 1]) for j in range(len(edges) - 1))
    return pl.pallas_call(
        functools.partial(_attn_kernel, kv_blocks=kv_blocks),
        out_shape=jax.ShapeDtypeStruct((bsz, t, qw), BF16),
        grid=(bsz, n_kv, t // tq),
        in_specs=[
            pl.BlockSpec((1, tq, gw), lambda b, g, i: (b, i, g)),
            pl.BlockSpec((1, tall, B_HD), lambda b, g, i: (b, 0, g)),
            pl.BlockSpec((1, tall, B_HD), lambda b, g, i: (b, 0, g)),
            pl.BlockSpec((1, tq, gw), lambda b, g, i: (b, i, g)),
        ],
        out_specs=pl.BlockSpec((1, tq, gw), lambda b, g, i: (b, i, g)),
        compiler_params=_cparams(("parallel", "parallel", "arbitrary")),
        name="attn",
    )(qb, kb, vb, szb)


def _mid_kernel(ya_ref, yb_ref, x_ref, wa_ref, wb_ref, pg_ref, gate_ref, sh_ref, sc_ref, g1_ref, w1_ref, b1_ref,
                xl_o, u_o, sz_o, h_scr):
    d = x_ref.shape[2]
    out = (jnp.dot(ya_ref[0], wa_ref[...], preferred_element_type=F32)
           + jnp.dot(yb_ref[0], wb_ref[...], preferred_element_type=F32))
    on = out * lax.rsqrt(jnp.mean(out * out, axis=-1, keepdims=True) + NORM_EPS) * pg_ref[...]
    xl = x_ref[0] + gate_ref[0] * on
    xl_o[0] = xl
    rs = lax.rsqrt(jnp.mean(xl * xl, axis=-1, keepdims=True) + NORM_EPS)
    h_scr[...] = (xl * rs * (g1_ref[...] * (1.0 + sc_ref[0])) + sh_ref[0]).astype(BF16)
    h = h_scr[...]
    a = jnp.dot(h, w1_ref[:, 0:d], preferred_element_type=F32) + b1_ref[:, 0:d]
    gl = jnp.dot(h, w1_ref[:, d:2 * d], preferred_element_type=F32) + b1_ref[:, d:2 * d]
    u_o[0] = (a * _sigmoid(gl)).astype(BF16)
    z = jnp.dot(h, w1_ref[:, 2 * d:3 * d], preferred_element_type=F32) + b1_ref[:, 2 * d:3 * d]
    sz_o[0] = _silu(z).astype(BF16)


def _mid(ya, yb, x, wa, wb, post_g, mod0, mod1, pre_g1, w1, b1, tm):
    bsz, t, d = x.shape
    aw = ya.shape[2]
    bw = yb.shape[2]
    rows = mod0.shape[0]
    m0 = mod0.reshape(rows, 1, 3 * d)
    m1 = mod1.reshape(rows, 1, 3 * d)
    tok = lambda c: pl.BlockSpec((1, tm, c), lambda b, i: (b, i, 0))
    const2 = lambda a: pl.BlockSpec(a.shape, lambda b, i: (0, 0))
    return pl.pallas_call(
        _mid_kernel,
        out_shape=[jax.ShapeDtypeStruct((bsz, t, d), F32),
                   jax.ShapeDtypeStruct((bsz, t, d), BF16),
                   jax.ShapeDtypeStruct((bsz, t, d), BF16)],
        grid=(bsz, t // tm),
        in_specs=[
            tok(aw), tok(bw), tok(d),
            const2(wa), const2(wb),
            pl.BlockSpec((1, d), lambda b, i: (0, 0)),
            pl.BlockSpec((1, 1, d), lambda b, i: (b, 0, 2)),
            pl.BlockSpec((1, 1, d), lambda b, i: (b, 0, 0)),
            pl.BlockSpec((1, 1, d), lambda b, i: (b, 0, 1)),
            pl.BlockSpec((1, d), lambda b, i: (0, 0)),
            const2(w1),
            pl.BlockSpec((1, 3 * d), lambda b, i: (0, 0)),
        ],
        out_specs=[tok(d), tok(d), tok(d)],
        scratch_shapes=[pltpu.VMEM((tm, d), BF16)],
        compiler_params=_cparams(("parallel", "parallel")),
        name="mid",
    )(ya, yb, x, wa, wb, post_g.reshape(1, d), m0, m1, m1, pre_g1.reshape(1, d), w1, b1.reshape(1, 3 * d))


def _tail_kernel(cur_ref, prev_ref, next_ref, sz_ref, xl_ref, dw_ref, dwb_ref, lng_ref, lnb_ref, wo_ref, bo_ref,
                 pg_ref, gate_ref, o_ref, buf, act, *, halo):
    i = pl.program_id(1)
    nt = pl.num_programs(1)
    tm = cur_ref.shape[1]
    d = cur_ref.shape[2]
    buf[0:halo, :] = jnp.where(i > 0, prev_ref[0].astype(F32), 0.0)
    buf[halo:halo + tm, :] = cur_ref[0].astype(F32)
    buf[halo + tm:2 * halo + tm, :] = jnp.where(i < nt - 1, next_ref[0].astype(F32), 0.0)
    pad = CONV_K // 2
    cw = 128
    sub = 8
    for cb in range(d // cw):
        cs = slice(cb * cw, (cb + 1) * cw)
        acc = None
        for s in range(sub):
            part = None
            for a in range((halo + pad) // sub + 1):
                j = sub * a + s - (halo - pad)
                if 0 <= j < CONV_K:
                    term = buf[sub * a:sub * a + tm + sub, cs] * dw_ref[j:j + 1, cs]
                    part = term if part is None else part + term
            if part is not None:
                acc = part[s:s + tm] if acc is None else acc + part[s:s + tm]
        act[:, cs] = acc + dwb_ref[:, cs]
    u = act[...]
    mu = jnp.mean(u, axis=-1, keepdims=True)
    uc = u - mu
    var = jnp.mean(uc * uc, axis=-1, keepdims=True)
    un = uc * lax.rsqrt(var + LN_EPS) * lng_ref[...] + lnb_ref[...]
    hact = (_silu(un) * sz_ref[0].astype(F32)).astype(BF16)
    out = jnp.dot(hact, wo_ref[...], preferred_element_type=F32) + bo_ref[...]
    on = out * lax.rsqrt(jnp.mean(out * out, axis=-1, keepdims=True) + NORM_EPS) * pg_ref[...]
    o_ref[0] = xl_ref[0] + gate_ref[0] * on


def _tail(u, sz, xl, dw_w, dw_b, ln_g, ln_b, w_out, b_out, post_g, mod1, tm):
    bsz, t, d = xl.shape
    halo = 16
    nhb = tm // halo
    last_blk = t // halo - 1
    rows = mod1.shape[0]
    m1 = mod1.reshape(rows, 1, 3 * d)
    tok = lambda c: pl.BlockSpec((1, tm, c), lambda b, i: (b, i, 0))
    vec = lambda: pl.BlockSpec((1, d), lambda b, i: (0, 0))
    kern = functools.partial(_tail_kernel, halo=halo)
    return pl.pallas_call(
        kern,
        out_shape=jax.ShapeDtypeStruct((bsz, t, d), F32),
        grid=(bsz, t // tm),
        in_specs=[
            tok(d),
            pl.BlockSpec((1, halo, d), lambda b, i: (b, jnp.maximum(i * nhb - 1, 0), 0)),
            pl.BlockSpec((1, halo, d), lambda b, i: (b, jnp.minimum((i + 1) * nhb, last_blk), 0)),
            tok(d), tok(d),
            pl.BlockSpec(dw_w.shape, lambda b, i: (0, 0)),
            vec(), vec(), vec(),
            pl.BlockSpec(w_out.shape, lambda b, i: (0, 0)),
            vec(), vec(),
            pl.BlockSpec((1, 1, d), lambda b, i: (b, 0, 2)),
        ],
        out_specs=tok(d),
        scratch_shapes=[pltpu.VMEM((tm + 2 * halo, d), F32), pltpu.VMEM((tm, d), F32)],
        compiler_params=_cparams(("parallel", "parallel")),
        name="tail",
    )(u, u, u, sz, xl, dw_w, dw_b.reshape(1, d), ln_g.reshape(1, d), ln_b.reshape(1, d), w_out,
      b_out.reshape(1, d), post_g.reshape(1, d), m1)


def _rope_tables(t, nctx):
    rows = t // GRID_W
    row = jnp.repeat(jnp.arange(rows, dtype=F32), GRID_W)
    col = jnp.tile(jnp.arange(GRID_W, dtype=F32), rows)
    axis_dim = B_HD // 2
    inv_freq = ROPE_THETA ** (-jnp.arange(0, axis_dim, 2, dtype=F32) / axis_dim)
    ang = jnp.concatenate([row[:, None] * inv_freq, col[:, None] * inv_freq], axis=1)
    cs = jnp.concatenate([jnp.cos(ang), jnp.cos(ang)], axis=1)
    sn = jnp.concatenate([-jnp.sin(ang), jnp.sin(ang)], axis=1)
    cs = jnp.concatenate([cs, jnp.ones((nctx, B_HD), F32)], axis=0)
    sn = jnp.concatenate([sn, jnp.zeros((nctx, B_HD), F32)], axis=0)
    return cs, sn


def _head_perm():
    q = B_HD // 4
    return np.concatenate([np.arange(0, q), np.arange(2 * q, 3 * q), np.arange(q, 2 * q), np.arange(3 * q, 4 * q)])


def kernel(x, c, ctx, c_ctx, ada_w, ada_b, pre_norm_g, post_norm_g, ev_w_in, ev_short_conv_w, ev_a_log,
           ev_dt_bias, ev_gdn_norm_g, ev_q_norm_g, ev_k_norm_g, ev_w_out, od_w_in, od_b_in, od_dw_w, od_dw_b,
           od_ln_g, od_ln_b, od_w_out, od_b_out):
    bsz, t, d = x.shape
    nctx = ctx.shape[1]
    a_width = d // 2
    a_heads = a_width // A_DV
    a_qkv = a_heads * (2 * A_DK + A_DV)
    b_width = d - a_width
    b_heads = b_width // B_HD
    b_kv = b_heads // 2
    assert ada_w.shape[0] == 2 and ev_w_in.shape[0] == 1 and od_w_in.shape[0] == 1

    mod_rows = ((bsz + 1 + 7) // 8) * 8
    cond = jnp.zeros((mod_rows, d), F32).at[:bsz].set(c).at[bsz].set(c_ctx)
    mod = _ada(cond, ada_w, ada_b)

    w_in = ev_w_in[0]
    splits = np.cumsum([0, a_qkv, a_width, 2 * a_heads, 2 * a_heads, b_heads * B_HD, b_kv * B_HD, b_kv * B_HD,
                        b_width])
    w_qkv, w_za, w_a, w_b, w_qb, w_kb, w_vb, w_zb = [w_in[:, splits[j]:splits[j + 1]] for j in range(8)]
    perm = _head_perm()
    perm_q = np.concatenate([h * B_HD + perm for h in range(b_heads)])
    perm_k = np.concatenate([h * B_HD + perm for h in range(b_kv)])
    ab_pad = 128 - 4 * a_heads
    pieces = [("qkv", w_qkv), ("za", w_za), ("qb", w_qb[:, perm_q]), ("kb", w_kb[:, perm_k]), ("vb", w_vb),
              ("zb", w_zb), ("ab", jnp.concatenate([w_a, w_b, jnp.zeros((d, ab_pad), F32)], axis=1))]
    secs, off = {}, 0
    for name, wpart in pieces:
        secs[name] = (off, off + wpart.shape[1])
        off += wpart.shape[1]
    w2 = jnp.concatenate([p[1] for p in pieces], axis=1).astype(BF16)
    cs_tab, sn_tab = _rope_tables(t, nctx)
    qg = ev_q_norm_g[0][perm].reshape(1, B_HD)
    kg = ev_k_norm_g[0][perm].reshape(1, B_HD)
    alog = jnp.zeros((1, 128), F32).at[0, :2 * a_heads].set(ev_a_log[0].reshape(-1))
    dtb = jnp.zeros((1, 128), F32).at[0, :2 * a_heads].set(ev_dt_bias[0].reshape(-1))

    tm0 = nctx
    qkvn, sza, qb, kb, vb, szb, gb = _inproj0(x, ctx, mod[0], pre_norm_g[0], w2, secs, ev_short_conv_w[0], cs_tab,
                                              sn_tab, qg, kg, alog, dtb, tm0, a_heads)

    tall = t + nctx
    n_chunks = tall // GDN_CHUNK
    ncp = ((n_chunks + 7) // 8) * 8
    grow = gb.reshape(bsz, n_chunks, GDN_CHUNK, 4, a_heads).transpose(0, 4, 3, 1, 2)
    grow = jnp.pad(grow, ((0, 0), (0, 0), (0, 0), (0, ncp - n_chunks), (0, 0)))
    ya = _gdn(qkvn, grow, sza, ev_gdn_norm_g[0].reshape(1, A_DV), a_heads, t)

    yb = _attn(qb, kb, vb, szb, t, 512)

    w_out0 = ev_w_out[0].astype(BF16)
    xl1, u1, sz1 = _mid(ya, yb, x, w_out0[:a_width], w_out0[a_width:], post_norm_g[0], mod[0], mod[1],
                        pre_norm_g[1], od_w_in[0].astype(BF16), od_b_in[0], 512)
    return _tail(u1, sz1, xl1, od_dw_w[0], od_dw_b[0], od_ln_g[0], od_ln_b[0], od_w_out[0].astype(BF16),
                 od_b_out[0], post_norm_g[1], mod[1], 256)
```

```python
import functools
import math

import numpy as np
import jax
import jax.numpy as jnp
from jax import lax
from jax.experimental import pallas as pl
from jax.experimental.pallas import tpu as pltpu

F32 = jnp.float32
BF16 = jnp.bfloat16
HIGHEST = lax.Precision.HIGHEST

GRID_W = 64
A_DK = 128
A_DV = 128
SHORT_CONV = 5
GDN_CHUNK = 64
INV_BLOCK = 16
B_HD = 128
ROPE_THETA = 10000.0
CONV_K = 31
NORM_EPS = 1e-6
LN_EPS = 1e-5
NEG_BIG = -1e30

VMEM_LIMIT = 56 * 1024 * 1024


def _sigmoid(x):
    return 1.0 / (1.0 + jnp.exp(-x))


def _silu(x):
    return x * _sigmoid(x)


def _softplus(x):
    return jnp.maximum(x, 0.0) + jnp.log(1.0 + jnp.exp(-jnp.abs(x)))


def _cparams(sem):
    return pltpu.CompilerParams(dimension_semantics=sem, vmem_limit_bytes=VMEM_LIMIT)


def _ada_kernel(c_ref, w_ref, b_ref, o_ref):
    s = _silu(c_ref[...])
    o_ref[0] = jnp.dot(s, w_ref[0], preferred_element_type=F32, precision=HIGHEST) + b_ref[0]


def _ada(cond, ada_w, ada_b):
    depth, d, d3 = ada_w.shape
    rows = cond.shape[0]
    nt = d3 // d
    return pl.pallas_call(
        _ada_kernel,
        out_shape=jax.ShapeDtypeStruct((depth, rows, d3), F32),
        grid=(depth, nt),
        in_specs=[
            pl.BlockSpec((rows, d), lambda l, j: (0, 0)),
            pl.BlockSpec((1, d, d), lambda l, j: (l, 0, j)),
            pl.BlockSpec((1, 1, d), lambda l, j: (l, 0, j)),
        ],
        out_specs=pl.BlockSpec((1, rows, d), lambda l, j: (l, 0, j)),
        compiler_params=_cparams(("parallel", "parallel")),
        name="ada",
    )(cond, ada_w, ada_b.reshape(depth, 1, d3))


def _inproj0_kernel(x_ref, xp_ref, xn_ref, ctx_ref, sh_ref, sc_ref, g_ref, w_ref, cw_ref, cs_ref, sn_ref, qg_ref,
                    kg_ref, alog_ref, dtb_ref,
                    qkv_o, za_o, qb_o, kb_o, vb_o, zb_o, gb_o, h_scr, *, n_lat_tiles, n_tiles, secs, halo,
                    n_gdn_heads):
    step_id = pl.program_id(0)
    wslot = step_id % 2
    rslot = 1 - wslot
    tm = x_ref.shape[1]
    last = pl.num_programs(0) - 2
    i = jnp.minimum(step_id, last) % n_tiles

    @pl.when(step_id == 0)
    def _():
        h_scr[...] = jnp.zeros_like(h_scr)

    def modulate_tile():
        mult = g_ref[...] * (1.0 + sc_ref[0])
        shift = sh_ref[0]

        def modulated(xf):
            rs = lax.rsqrt(jnp.mean(xf * xf, axis=-1, keepdims=True) + NORM_EPS)
            return xf * rs * mult + shift

        prev_ok = jnp.logical_and(i > 0, i < n_lat_tiles)
        next_ok = i < n_lat_tiles - 1
        h_scr[wslot, 0:halo, :] = jnp.where(prev_ok, modulated(xp_ref[0]), 0.0).astype(BF16)
        h_scr[wslot, halo + tm:2 * halo + tm, :] = jnp.where(next_ok, modulated(xn_ref[0]), 0.0).astype(BF16)
        src = jnp.where(i < n_lat_tiles, x_ref[0], ctx_ref[0])
        h_scr[wslot, halo:halo + tm, :] = modulated(src).astype(BF16)

    def proj(name):
        c0, c1 = secs[name]
        return lambda: jnp.dot(h_scr[rslot, halo:halo + tm, :], w_ref[:, c0:c1], preferred_element_type=F32)

    pad = SHORT_CONV // 2
    pw = 2 * A_DK
    hpb = pw // A_DK

    def qkv_mm(pb):
        c0 = secs["qkv"][0] + pb * pw
        return lambda: jnp.dot(h_scr[rslot], w_ref[:, c0:c0 + pw], preferred_element_type=F32)

    def qkv_epi(pb):
        def epi(y):
            for sb in range(hpb):
                hb = pb * hpb + sb
                cl = slice(hb * A_DK, (hb + 1) * A_DK)
                acc = None
                for j in range(SHORT_CONV):
                    term = y[halo - pad + j:halo - pad + j + tm, sb * A_DK:(sb + 1) * A_DK] * cw_ref[j:j + 1, cl]
                    acc = term if acc is None else acc + term
                u = _silu(acc)
                if hb < 2 * n_gdn_heads:
                    u = u * lax.rsqrt(jnp.sum(u * u, axis=-1, keepdims=True) + NORM_EPS)
                    if hb < n_gdn_heads:
                        u = u * (A_DK ** -0.5)
                qkv_o[0, :, cl] = u.astype(BF16)
        return epi

    def gate_epi(o_ref):
        def epi(y):
            o_ref[0] = _silu(y).astype(BF16)
        return epi

    def plain_epi(y):
        vb_o[0] = y.astype(BF16)

    def norm_rope_epi(o_ref, g_ref_, out_scale):
        def epi(y):
            cs = cs_ref[...]
            sn = sn_ref[...]
            g = g_ref_[...]
            for hh in range(y.shape[1] // B_HD):
                sl = slice(hh * B_HD, (hh + 1) * B_HD)
                yh = y[:, sl]
                ms = jnp.mean(yh * yh, axis=-1, keepdims=True)
                yn = yh * lax.rsqrt(ms + NORM_EPS) * g
                o_ref[0, :, sl] = ((yn * cs + pltpu.roll(yn, B_HD // 2, 1) * sn) * out_scale).astype(BF16)
        return epi

    def decay_epi(ab):
        nab = gb_o.shape[2]
        gval = -jnp.exp(alog_ref[...]) * _softplus(ab + dtb_ref[...])
        bval = _sigmoid(ab)
        lane = lax.broadcasted_iota(jnp.int32, ab.shape, 1)
        gb_o[0] = jnp.where(lane < nab // 2, gval, bval)[:, :nab]

    n_qkv_blocks = (secs["qkv"][1] - secs["qkv"][0]) // pw
    heavy = [(qkv_mm(pb), qkv_epi(pb)) for pb in range(n_qkv_blocks)]
    light = [(proj("za"), gate_epi(za_o)), (proj("zb"), gate_epi(zb_o)), (proj("vb"), plain_epi),
             (proj("qb"), norm_rope_epi(qb_o, qg_ref, B_HD ** -0.5)), (proj("kb"), norm_rope_epi(kb_o, kg_ref, 1.0)),
             (proj("ab"), decay_epi)]
    stages = []
    for k in range(max(len(heavy), len(light))):
        stages += heavy[k:k + 1] + light[k:k + 1]
    y_next = stages[0][0]()
    modulate_tile()
    for k, (_, epi) in enumerate(stages):
        y = y_next
        if k + 1 < len(stages):
            y_next = stages[k + 1][0]()
        epi(y)


def _inproj0(x, ctx, mod0, pre_g, w2, secs, conv_w, cs_tab, sn_tab, qg, kg, alog, dtb, tm, n_gdn_heads):
    bsz, t, d = x.shape
    nctx = ctx.shape[1]
    tall = t + nctx
    assert t % tm == 0 and nctx == tm
    n_lat = t // tm
    ntile = n_lat + 1
    nab = 16
    halo = 16
    nhb = tm // halo
    last_hblk = t // halo - 1

    def widths(name):
        return secs[name][1] - secs[name][0]

    n_steps = bsz * ntile

    def cur(s):
        p = jnp.minimum(s, n_steps - 1)
        return p // ntile, p % ntile

    def prev(s):
        p = jnp.maximum(s - 1, 0)
        return p // ntile, p % ntile

    def row_spec(c):
        return pl.BlockSpec((1, tm, c), lambda s: (prev(s)[0], prev(s)[1], 0))

    def mod_row(s):
        b, i = cur(s)
        return jnp.where(i < n_lat, b, ctx_row)

    mod_rows = mod0.shape[0]
    mod3 = mod0.reshape(mod_rows, 1, 3 * d)
    ctx_row = bsz
    outs = [
        jax.ShapeDtypeStruct((bsz, tall, widths("qkv")), BF16),
        jax.ShapeDtypeStruct((bsz, tall, widths("za")), BF16),
        jax.ShapeDtypeStruct((bsz, tall, widths("qb")), BF16),
        jax.ShapeDtypeStruct((bsz, tall, widths("kb")), BF16),
        jax.ShapeDtypeStruct((bsz, tall, widths("vb")), BF16),
        jax.ShapeDtypeStruct((bsz, tall, widths("zb")), BF16),
        jax.ShapeDtypeStruct((bsz, tall, nab), F32),
    ]
    kern = functools.partial(_inproj0_kernel, n_lat_tiles=n_lat, n_tiles=ntile, secs=secs, halo=halo,
                             n_gdn_heads=n_gdn_heads)
    return pl.pallas_call(
        kern,
        out_shape=outs,
        grid=(n_steps + 1,),
        in_specs=[
            pl.BlockSpec((1, tm, d), lambda s: (cur(s)[0], jnp.minimum(cur(s)[1], n_lat - 1), 0)),
            pl.BlockSpec((1, halo, d), lambda s: (cur(s)[0], jnp.clip(cur(s)[1] * nhb - 1, 0, last_hblk), 0)),
            pl.BlockSpec((1, halo, d), lambda s: (cur(s)[0], jnp.clip((cur(s)[1] + 1) * nhb, 0, last_hblk), 0)),
            pl.BlockSpec((1, tm, d), lambda s: (cur(s)[0], 0, 0)),
            pl.BlockSpec((1, 1, d), lambda s: (mod_row(s), 0, 0)),
            pl.BlockSpec((1, 1, d), lambda s: (mod_row(s), 0, 1)),
            pl.BlockSpec((1, d), lambda s: (0, 0)),
            pl.BlockSpec(w2.shape, lambda s: (0, 0)),
            pl.BlockSpec(conv_w.shape, lambda s: (0, 0)),
            pl.BlockSpec((tm, B_HD), lambda s: (prev(s)[1], 0)),
            pl.BlockSpec((tm, B_HD), lambda s: (prev(s)[1], 0)),
            pl.BlockSpec((1, B_HD), lambda s: (0, 0)),
            pl.BlockSpec((1, B_HD), lambda s: (0, 0)),
            pl.BlockSpec((1, 128), lambda s: (0, 0)),
            pl.BlockSpec((1, 128), lambda s: (0, 0)),
        ],
        out_specs=[row_spec(widths("qkv")), row_spec(widths("za")), row_spec(widths("qb")), row_spec(widths("kb")),
                   row_spec(widths("vb")), row_spec(widths("zb")), row_spec(nab)],
        scratch_shapes=[pltpu.VMEM((2, tm + 2 * halo, d), BF16)],
        compiler_params=_cparams(("arbitrary",)),
        name="inproj0",
    )(x, x, x, ctx, mod3, mod3, pre_g.reshape(1, d), w2, conv_w, cs_tab, sn_tab, qg, kg, alog, dtb)


def _gdn_kernel(q_ref, k_ref, v_ref, gr_ref, sz_ref, ng_ref, o_ref,
                nq_scr, z_scr, au_scr, gt_scr, cr_scr, s_scr, oacc, *, n_lat_chunks, n_chunks, group):
    C = GDN_CHUNK
    step_id = pl.program_id(0)
    wset = step_id % 2
    rset = 1 - wset
    ri = lax.broadcasted_iota(jnp.int32, (C, C), 0)
    ci = lax.broadcasted_iota(jnp.int32, (C, C), 1)
    incl = (ri >= ci, ri <= ci)
    strict = (ri > ci, ri < ci)
    eye = (ri == ci).astype(F32)
    tri_row = ((ri <= ci).astype(F32), (ri >= ci).astype(F32))
    tri_col = (incl[0].astype(F32), incl[1].astype(F32))

    @pl.when(step_id == 0)
    def _():
        nq_scr[...] = jnp.zeros_like(nq_scr)
        z_scr[...] = jnp.zeros_like(z_scr)
        au_scr[...] = jnp.zeros_like(au_scr)
        gt_scr[...] = jnp.zeros_like(gt_scr)

    for d in range(2):
        cr_scr[d] = jnp.dot(gr_ref[0, 0, d], tri_row[d], preferred_element_type=F32, precision=HIGHEST)
    s_scr[...] = jnp.zeros_like(s_scr)
    oacc[...] = jnp.zeros_like(oacc)

    nt = (((1,), (1,)), ((), ()))
    nsq = int(math.log2(INV_BLOCK)) - 1
    diag_blk = (ri // INV_BLOCK) == (ci // INV_BLOCK)
    off_blks = []
    bs = INV_BLOCK
    while bs < C:
        off_blks.append(jnp.logical_and((ri // (2 * bs)) == (ci // (2 * bs)), (ri // bs) != (ci // bs)))
        bs *= 2

    def mm(a, b):
        return jnp.dot(a, b, preferred_element_type=F32)

    tl = (((0,), (0,)), ((), ()))

    def prep(chunk_ids):
        ch = []
        for n in chunk_ids:
            rows = pl.ds(pl.multiple_of(n * C, C), C)
            kb = k_ref[0, rows, :]
            qb = q_ref[0, rows, :]
            k32 = kb.astype(F32)
            q32 = qb.astype(F32)
            v32 = v_ref[0, rows, :].astype(F32)
            kk = lax.dot_general(kb, kb, nt, preferred_element_type=F32)
            qk = lax.dot_general(qb, kb, nt, preferred_element_type=F32)
            for d in range(2):
                g_row = gr_ref[0, 0, d, pl.ds(n, 1), :]
                beta_row = gr_ref[0, 0, 2 + d, pl.ds(n, 1), :]
                c_row = cr_scr[d, pl.ds(n, 1), :]
                c_col = jnp.sum(tri_col[d] * g_row, axis=-1, keepdims=True)
                beta_col = jnp.sum(eye * beta_row, axis=-1, keepdims=True)
                tot = jnp.sum(g_row, axis=-1, keepdims=True)
                decay = jnp.exp(jnp.where(incl[d], c_col - c_row, NEG_BIG))
                x = jnp.where(strict[d], kk * decay, 0.0) * (-beta_col)
                e_col = jnp.exp(c_col)
                rhs = jnp.concatenate([v32 * beta_col, k32 * (beta_col * e_col)], axis=1)
                qd = q32 * e_col
                kd = (k32 * jnp.exp(tot - c_col)).astype(BF16)
                am = jnp.where(incl[d], qk * decay, 0.0).astype(BF16)
                gt_scr[2 * wset + d, n] = jnp.broadcast_to(jnp.exp(tot), (8, 128))
                ch.append((d, n, rows, x, rhs, qd, kd, am))
        yield
        xs = [c[3] for c in ch]
        ys = [jnp.where(diag_blk, x, 0.0) for x in xs]
        xbs = [y.astype(BF16) for y in ys]
        ps = [mm(xb, xb) for xb in xbs]
        yield
        for m in range(nsq):
            pbs = [p.astype(BF16) for p in ps]
            ybs = [y.astype(BF16) for y in ys]
            if m < nsq - 1:
                rs = [mm(jnp.concatenate([pb, yb], axis=0), pb) for pb, yb in zip(pbs, ybs)]
                ys = [y + p + r[C:] for y, p, r in zip(ys, ps, rs)]
                ps = [r[:C] for r in rs]
            else:
                ys = [y + p + mm(yb, pb) for y, p, yb, pb in zip(ys, ps, ybs, pbs)]
            yield
        for off in off_blks:
            xos = [jnp.where(off, x, 0.0) for x in xs]
            ybs = [y.astype(BF16) for y in ys]
            ts = [xo + mm(yb, xo.astype(BF16)) for xo, yb in zip(xos, ybs)]
            yield
            ys = [y + t + mm(t.astype(BF16), yb) for y, t, yb in zip(ys, ts, ybs)]
            yield
        ybs = [y.astype(BF16) for y in ys]
        uws = [c[4] + mm(yb, c[4].astype(BF16)) for c, yb in zip(ch, ybs)]
        uwbs = [uw.astype(BF16) for uw in uws]
        yield
        kzs = [lax.dot_general(c[6], uwb, tl, preferred_element_type=F32) for c, uwb in zip(ch, uwbs)]
        azs = [mm(c[7], uwb) for c, uwb in zip(ch, uwbs)]
        for c, kz, az in zip(ch, kzs, azs):
            cd, n, rows = 2 * wset + c[0], c[1], c[2]
            z_scr[cd, n] = kz[:, :A_DV].astype(z_scr.dtype)
            nq_scr[cd, n, 0:A_DK, :] = kz[:, A_DV:].astype(BF16)
            nq_scr[cd, n, A_DK:A_DK + C, :] = (c[5] - az[:, A_DV:]).astype(BF16)
            au_scr[cd, rows, :] = az[:, :A_DV].astype(au_scr.dtype)
        yield

    def scan_step(ns, with_out):
        nrow = A_DK + C if with_out else A_DK
        ss = [s_scr[d] for d in range(2)]
        rs = [mm(nq_scr[2 * rset + d, ns[d], 0:nrow, :], ss[d].astype(BF16)) for d in range(2)]
        for d in range(2):
            cd, n = 2 * rset + d, ns[d]
            s_scr[d] = ss[d] * gt_scr[cd, n][0:1, :] - rs[d][:A_DK] + z_scr[cd, n].astype(F32)
            if with_out:
                rows = pl.ds(pl.multiple_of(n * C, C), C)
                oacc[rows, :] = oacc[rows, :] + rs[d][A_DK:] + au_scr[cd, rows, :].astype(F32)

    def interleave(stages, scan_steps):
        pending = list(scan_steps)
        for _ in stages:
            if pending:
                pending.pop(0)()
        for rest in pending:
            rest()

    n_ctx_chunks = n_chunks - n_lat_chunks
    interleave(prep([n_lat_chunks + i for i in range(n_ctx_chunks)]),
               [functools.partial(scan_step, (n_lat_chunks + j, n_chunks - 1 - j), False)
                for j in range(n_ctx_chunks)])

    def body(gi, carry):
        base = gi * group
        interleave(prep([base + i for i in range(group)]),
                   [functools.partial(scan_step, (base + j, n_lat_chunks - 1 - base - j), True)
                    for j in range(group)])
        return carry

    lax.fori_loop(0, n_lat_chunks // group, body, 0)

    ng = ng_ref[...]
    blk = 512

    def fin(j, carry):
        rows = pl.ds(pl.multiple_of(j * blk, blk), blk)
        o = oacc[rows, :]
        on = o * lax.rsqrt(jnp.mean(o * o, axis=-1, keepdims=True) + NORM_EPS) * ng
        o_ref[0, rows, :] = (on * sz_ref[0, rows, :].astype(F32)).astype(BF16)
        return carry

    lax.fori_loop(0, (n_lat_chunks * C) // blk, fin, 0)


def _gdn(qkvn, grow, sza, norm_g, n_heads, t):
    bsz, tall, _ = qkvn.shape
    C = GDN_CHUNK
    n_chunks = tall // C
    n_lat_chunks = t // C
    ncp = grow.shape[3]
    group = 8
    assert n_lat_chunks % group == 0
    kern = functools.partial(_gdn_kernel, n_lat_chunks=n_lat_chunks, n_chunks=n_chunks, group=group)
    n_pairs = bsz * n_heads

    def cur(s):
        p = jnp.minimum(s, n_pairs - 1)
        return p // n_heads, p % n_heads

    def prev(s):
        p = jnp.maximum(s - 1, 0)
        return p // n_heads, p % n_heads

    return pl.pallas_call(
        kern,
        out_shape=jax.ShapeDtypeStruct((bsz, t, n_heads * A_DV), BF16),
        grid=(n_pairs + 1,),
        in_specs=[
            pl.BlockSpec((1, tall, A_DK), lambda s: (cur(s)[0], 0, cur(s)[1])),
            pl.BlockSpec((1, tall, A_DK), lambda s: (cur(s)[0], 0, n_heads + cur(s)[1])),
            pl.BlockSpec((1, tall, A_DV), lambda s: (cur(s)[0], 0, 2 * n_heads + cur(s)[1])),
            pl.BlockSpec((1, 1, 4, ncp, C), lambda s: (cur(s)[0], cur(s)[1], 0, 0, 0)),
            pl.BlockSpec((1, t, A_DV), lambda s: (prev(s)[0], 0, prev(s)[1])),
            pl.BlockSpec((1, A_DV), lambda s: (0, 0)),
        ],
        out_specs=pl.BlockSpec((1, t, A_DV), lambda s: (prev(s)[0], 0, prev(s)[1])),
        scratch_shapes=[
            pltpu.VMEM((4, n_chunks, A_DK + C, A_DK), BF16),
            pltpu.VMEM((4, n_chunks, A_DK, A_DV), BF16),
            pltpu.VMEM((4, tall, A_DV), BF16),
            pltpu.VMEM((4, n_chunks, 8, 128), F32),
            pltpu.VMEM((2, ncp, C), F32),
            pltpu.VMEM((2, A_DK, A_DV), F32),
            pltpu.VMEM((t, A_DV), F32),
        ],
        compiler_params=_cparams(("arbitrary",)),
        name="gdn",
    )(qkvn, qkvn, qkvn, grow, sza, norm_g)


def _attn_kernel(q_ref, k_ref, v_ref, sz_ref, o_ref, *, kv_blocks):
    nt = (((1,), (1,)), ((), ()))
    n_heads = q_ref.shape[2] // B_HD
    items = [(hh, blk) for hh in range(n_heads) for blk in kv_blocks]

    def scores(item):
        hh, (k0, k1) = item
        q = q_ref[0, :, hh * B_HD:(hh + 1) * B_HD]
        return lax.dot_general(q, k_ref[0, k0:k1, :], nt, preferred_element_type=F32)

    s_next = scores(items[0])
    m = l = acc = None
    for idx, (hh, (k0, k1)) in enumerate(items):
        s = s_next
        if idx + 1 < len(items):
            s_next = scores(items[idx + 1])
        bm = jnp.max(s, axis=-1, keepdims=True)
        if k0 == kv_blocks[0][0]:
            m = bm
            p = jnp.exp(s - m)
            l = jnp.sum(p, axis=-1, keepdims=True)
            acc = jnp.dot(p.astype(BF16), v_ref[0, k0:k1, :], preferred_element_type=F32)
        else:
            m_new = jnp.maximum(m, bm)
            alpha = jnp.exp(m - m_new)
            p = jnp.exp(s - m_new)
            l = alpha * l + jnp.sum(p, axis=-1, keepdims=True)
            acc = alpha * acc + jnp.dot(p.astype(BF16), v_ref[0, k0:k1, :], preferred_element_type=F32)
            m = m_new
        if k1 == kv_blocks[-1][1]:
            sl = slice(hh * B_HD, (hh + 1) * B_HD)
            o_ref[0, :, sl] = (acc * (1.0 / l) * sz_ref[0, :, sl].astype(F32)).astype(BF16)


def _attn(qb, kb, vb, szb, t, tq):
    bsz, tall, qw = qb.shape
    n_kv = kb.shape[2] // B_HD
    gw = qw // n_kv
    kvb = 1024
    edges = list(range(0, tall, kvb)) + [tall]
    kv_blocks = tuple((edges[j], edges[j + 1]) for j in range(len(edges) - 1))
    return pl.pallas_call(
        functools.partial(_attn_kernel, kv_blocks=kv_blocks),
        out_shape=jax.ShapeDtypeStruct((bsz, t, qw), BF16),
        grid=(bsz, n_kv, t // tq),
        in_specs=[
            pl.BlockSpec((1, tq, gw), lambda b, g, i: (b, i, g)),
            pl.BlockSpec((1, tall, B_HD), lambda b, g, i: (b, 0, g)),
            pl.BlockSpec((1, tall, B_HD), lambda b, g, i: (b, 0, g)),
            pl.BlockSpec((1, tq, gw), lambda b, g, i: (b, i, g)),
        ],
        out_specs=pl.BlockSpec((1, tq, gw), lambda b, g, i: (b, i, g)),
        compiler_params=_cparams(("parallel", "parallel", "arbitrary")),
        name="attn",
    )(qb, kb, vb, szb)


def _mid_kernel(ya_ref, yb_ref, x_ref, wa_ref, wb_ref, pg_ref, gate_ref, sh_ref, sc_ref, g1_ref, w1_ref, b1_ref,
                xl_o, u_o, sz_o, h_scr):
    d = x_ref.shape[2]
    tm = x_ref.shape[1]
    nsub = 2
    ts = tm // nsub
    mult1 = g1_ref[...] * (1.0 + sc_ref[0])

    def out_proj(r):
        return (jnp.dot(ya_ref[0, r, :], wa_ref[...], preferred_element_type=F32)
                + jnp.dot(yb_ref[0, r, :], wb_ref[...], preferred_element_type=F32))

    def residual_and_modulate(r, out):
        on = out * lax.rsqrt(jnp.mean(out * out, axis=-1, keepdims=True) + NORM_EPS) * pg_ref[...]
        xl = x_ref[0, r, :] + gate_ref[0] * on
        xl_o[0, r, :] = xl
        rs = lax.rsqrt(jnp.mean(xl * xl, axis=-1, keepdims=True) + NORM_EPS)
        h_scr[r, :] = (xl * rs * mult1 + sh_ref[0]).astype(BF16)

    def in_proj(r):
        h = h_scr[r, :]
        a = jnp.dot(h, w1_ref[:, 0:d], preferred_element_type=F32) + b1_ref[:, 0:d]
        gl = jnp.dot(h, w1_ref[:, d:2 * d], preferred_element_type=F32) + b1_ref[:, d:2 * d]
        u_o[0, r, :] = (a * _sigmoid(gl)).astype(BF16)
        z = jnp.dot(h, w1_ref[:, 2 * d:3 * d], preferred_element_type=F32) + b1_ref[:, 2 * d:3 * d]
        sz_o[0, r, :] = _silu(z).astype(BF16)

    subs = [slice(k * ts, (k + 1) * ts) for k in range(nsub)]
    outs = [out_proj(subs[0])]
    for k in range(nsub):
        if k + 1 < nsub:
            outs.append(out_proj(subs[k + 1]))
        residual_and_modulate(subs[k], outs[k])
        if k > 0:
            in_proj(subs[k - 1])
    in_proj(subs[nsub - 1])


def _mid(ya, yb, x, wa, wb, post_g, mod0, mod1, pre_g1, w1, b1, tm):
    bsz, t, d = x.shape
    aw = ya.shape[2]
    bw = yb.shape[2]
    rows = mod0.shape[0]
    m0 = mod0.reshape(rows, 1, 3 * d)
    m1 = mod1.reshape(rows, 1, 3 * d)
    tok = lambda c: pl.BlockSpec((1, tm, c), lambda b, i: (b, i, 0))
    const2 = lambda a: pl.BlockSpec(a.shape, lambda b, i: (0, 0))
    return pl.pallas_call(
        _mid_kernel,
        out_shape=[jax.ShapeDtypeStruct((bsz, t, d), F32),
                   jax.ShapeDtypeStruct((bsz, t, d), BF16),
                   jax.ShapeDtypeStruct((bsz, t, d), BF16)],
        grid=(bsz, t // tm),
        in_specs=[
            tok(aw), tok(bw), tok(d),
            const2(wa), const2(wb),
            pl.BlockSpec((1, d), lambda b, i: (0, 0)),
            pl.BlockSpec((1, 1, d), lambda b, i: (b, 0, 2)),
            pl.BlockSpec((1, 1, d), lambda b, i: (b, 0, 0)),
            pl.BlockSpec((1, 1, d), lambda b, i: (b, 0, 1)),
            pl.BlockSpec((1, d), lambda b, i: (0, 0)),
            const2(w1),
            pl.BlockSpec((1, 3 * d), lambda b, i: (0, 0)),
        ],
        out_specs=[tok(d), tok(d), tok(d)],
        scratch_shapes=[pltpu.VMEM((tm, d), BF16)],
        compiler_params=_cparams(("parallel", "parallel")),
        name="mid",
    )(ya, yb, x, wa, wb, post_g.reshape(1, d), m0, m1, m1, pre_g1.reshape(1, d), w1, b1.reshape(1, 3 * d))


def _tail_kernel(cur_ref, prev_ref, next_ref, sz_ref, xl_ref, dw_ref, dwb_ref, lng_ref, lnb_ref, wo_ref, bo_ref,
                 pg_ref, gate_ref, o_ref, buf, act, *, halo):
    i = pl.program_id(1)
    nt = pl.num_programs(1)
    tm = cur_ref.shape[1]
    d = cur_ref.shape[2]
    buf[0:halo, :] = jnp.where(i > 0, prev_ref[0].astype(F32), 0.0)
    buf[halo:halo + tm, :] = cur_ref[0].astype(F32)
    buf[halo + tm:2 * halo + tm, :] = jnp.where(i < nt - 1, next_ref[0].astype(F32), 0.0)
    pad = CONV_K // 2
    cw = 128
    sub = 8

    def conv_rows(r0, nr):
        for cb in range(d // cw):
            cs = slice(cb * cw, (cb + 1) * cw)
            acc = None
            for s in range(sub):
                part = None
                for a in range((halo + pad) // sub + 1):
                    j = sub * a + s - (halo - pad)
                    if 0 <= j < CONV_K:
                        term = buf[r0 + sub * a:r0 + sub * a + nr + sub, cs] * dw_ref[j:j + 1, cs]
                        part = term if part is None else part + term
                if part is not None:
                    acc = part[s:s + nr] if acc is None else acc + part[s:s + nr]
            act[r0:r0 + nr, cs] = acc + dwb_ref[:, cs]

    def activate(r):
        u = act[r, :]
        mu = jnp.mean(u, axis=-1, keepdims=True)
        uc = u - mu
        var = jnp.mean(uc * uc, axis=-1, keepdims=True)
        un = uc * lax.rsqrt(var + LN_EPS) * lng_ref[...] + lnb_ref[...]
        hact = (_silu(un) * sz_ref[0, r, :].astype(F32)).astype(BF16)
        return jnp.dot(hact, wo_ref[...], preferred_element_type=F32) + bo_ref[...]

    def finish(r, out):
        on = out * lax.rsqrt(jnp.mean(out * out, axis=-1, keepdims=True) + NORM_EPS) * pg_ref[...]
        o_ref[0, r, :] = xl_ref[0, r, :] + gate_ref[0] * on

    nsub = 2
    ts = tm // nsub
    pending = None
    for k in range(nsub):
        r = slice(k * ts, (k + 1) * ts)
        conv_rows(k * ts, ts)
        out = activate(r)
        if pending is not None:
            finish(*pending)
        pending = (r, out)
    finish(*pending)


def _tail(u, sz, xl, dw_w, dw_b, ln_g, ln_b, w_out, b_out, post_g, mod1, tm):
    bsz, t, d = xl.shape
    halo = 16
    nhb = tm // halo
    last_blk = t // halo - 1
    rows = mod1.shape[0]
    m1 = mod1.reshape(rows, 1, 3 * d)
    tok = lambda c: pl.BlockSpec((1, tm, c), lambda b, i: (b, i, 0))
    vec = lambda: pl.BlockSpec((1, d), lambda b, i: (0, 0))
    kern = functools.partial(_tail_kernel, halo=halo)
    return pl.pallas_call(
        kern,
        out_shape=jax.ShapeDtypeStruct((bsz, t, d), F32),
        grid=(bsz, t // tm),
        in_specs=[
            tok(d),
            pl.BlockSpec((1, halo, d), lambda b, i: (b, jnp.maximum(i * nhb - 1, 0), 0)),
            pl.BlockSpec((1, halo, d), lambda b, i: (b, jnp.minimum((i + 1) * nhb, last_blk), 0)),
            tok(d), tok(d),
            pl.BlockSpec(dw_w.shape, lambda b, i: (0, 0)),
            vec(), vec(), vec(),
            pl.BlockSpec(w_out.shape, lambda b, i: (0, 0)),
            vec(), vec(),
            pl.BlockSpec((1, 1, d), lambda b, i: (b, 0, 2)),
        ],
        out_specs=tok(d),
        scratch_shapes=[pltpu.VMEM((tm + 2 * halo, d), F32), pltpu.VMEM((tm, d), F32)],
        compiler_params=_cparams(("parallel", "parallel")),
        name="tail",
    )(u, u, u, sz, xl, dw_w, dw_b.reshape(1, d), ln_g.reshape(1, d), ln_b.reshape(1, d), w_out,
      b_out.reshape(1, d), post_g.reshape(1, d), m1)


def _rope_tables(t, nctx):
    rows = t // GRID_W
    row = jnp.repeat(jnp.arange(rows, dtype=F32), GRID_W)
    col = jnp.tile(jnp.arange(GRID_W, dtype=F32), rows)
    axis_dim = B_HD // 2
    inv_freq = ROPE_THETA ** (-jnp.arange(0, axis_dim, 2, dtype=F32) / axis_dim)
    ang = jnp.concatenate([row[:, None] * inv_freq, col[:, None] * inv_freq], axis=1)
    cs = jnp.concatenate([jnp.cos(ang), jnp.cos(ang)], axis=1)
    sn = jnp.concatenate([-jnp.sin(ang), jnp.sin(ang)], axis=1)
    cs = jnp.concatenate([cs, jnp.ones((nctx, B_HD), F32)], axis=0)
    sn = jnp.concatenate([sn, jnp.zeros((nctx, B_HD), F32)], axis=0)
    return cs, sn


def _head_perm():
    q = B_HD // 4
    return np.concatenate([np.arange(0, q), np.arange(2 * q, 3 * q), np.arange(q, 2 * q), np.arange(3 * q, 4 * q)])


def kernel(x, c, ctx, c_ctx, ada_w, ada_b, pre_norm_g, post_norm_g, ev_w_in, ev_short_conv_w, ev_a_log,
           ev_dt_bias, ev_gdn_norm_g, ev_q_norm_g, ev_k_norm_g, ev_w_out, od_w_in, od_b_in, od_dw_w, od_dw_b,
           od_ln_g, od_ln_b, od_w_out, od_b_out):
    bsz, t, d = x.shape
    nctx = ctx.shape[1]
    a_width = d // 2
    a_heads = a_width // A_DV
    a_qkv = a_heads * (2 * A_DK + A_DV)
    b_width = d - a_width
    b_heads = b_width // B_HD
    b_kv = b_heads // 2
    assert ada_w.shape[0] == 2 and ev_w_in.shape[0] == 1 and od_w_in.shape[0] == 1

    mod_rows = ((bsz + 1 + 7) // 8) * 8
    cond = jnp.zeros((mod_rows, d), F32).at[:bsz].set(c).at[bsz].set(c_ctx)
    mod = _ada(cond, ada_w, ada_b)

    w_in = ev_w_in[0]
    splits = np.cumsum([0, a_qkv, a_width, 2 * a_heads, 2 * a_heads, b_heads * B_HD, b_kv * B_HD, b_kv * B_HD,
                        b_width])
    w_qkv, w_za, w_a, w_b, w_qb, w_kb, w_vb, w_zb = [w_in[:, splits[j]:splits[j + 1]] for j in range(8)]
    perm = _head_perm()
    perm_q = np.concatenate([h * B_HD + perm for h in range(b_heads)])
    perm_k = np.concatenate([h * B_HD + perm for h in range(b_kv)])
    ab_pad = 128 - 4 * a_heads
    pieces = [("qkv", w_qkv), ("za", w_za), ("qb", w_qb[:, perm_q]), ("kb", w_kb[:, perm_k]), ("vb", w_vb),
              ("zb", w_zb), ("ab", jnp.concatenate([w_a, w_b, jnp.zeros((d, ab_pad), F32)], axis=1))]
    secs, off = {}, 0
    for name, wpart in pieces:
        secs[name] = (off, off + wpart.shape[1])
        off += wpart.shape[1]
    w2 = jnp.concatenate([p[1] for p in pieces], axis=1).astype(BF16)
    cs_tab, sn_tab = _rope_tables(t, nctx)
    qg = ev_q_norm_g[0][perm].reshape(1, B_HD)
    kg = ev_k_norm_g[0][perm].reshape(1, B_HD)
    alog = jnp.zeros((1, 128), F32).at[0, :2 * a_heads].set(ev_a_log[0].reshape(-1))
    dtb = jnp.zeros((1, 128), F32).at[0, :2 * a_heads].set(ev_dt_bias[0].reshape(-1))

    tm0 = nctx
    qkvn, sza, qb, kb, vb, szb, gb = _inproj0(x, ctx, mod[0], pre_norm_g[0], w2, secs, ev_short_conv_w[0], cs_tab,
                                              sn_tab, qg, kg, alog, dtb, tm0, a_heads)

    tall = t + nctx
    n_chunks = tall // GDN_CHUNK
    ncp = ((n_chunks + 7) // 8) * 8
    grow = gb.reshape(bsz, n_chunks, GDN_CHUNK, 4, a_heads).transpose(0, 4, 3, 1, 2)
    grow = jnp.pad(grow, ((0, 0), (0, 0), (0, 0), (0, ncp - n_chunks), (0, 0)))
    ya = _gdn(qkvn, grow, sza, ev_gdn_norm_g[0].reshape(1, A_DV), a_heads, t)

    yb = _attn(qb, kb, vb, szb, t, 512)

    w_out0 = ev_w_out[0].astype(BF16)
    xl1, u1, sz1 = _mid(ya, yb, x, w_out0[:a_width], w_out0[a_width:], post_norm_g[0], mod[0], mod[1],
                        pre_norm_g[1], od_w_in[0].astype(BF16), od_b_in[0], 512)
    return _tail(u1, sz1, xl1, od_dw_w[0], od_dw_b[0], od_ln_g[0], od_ln_b[0], od_w_out[0].astype(BF16),
                 od_b_out[0], post_norm_g[1], mod[1], 512)
```

```python
import functools
import math

import numpy as np
import jax
import jax.numpy as jnp
from jax import lax
from jax.experimental import pallas as pl
from jax.experimental.pallas import tpu as pltpu

F32 = jnp.float32
BF16 = jnp.bfloat16
HIGHEST = lax.Precision.HIGHEST

GRID_W = 64
A_DK = 128
A_DV = 128
SHORT_CONV = 5
GDN_CHUNK = 64
INV_BLOCK = 16
B_HD = 128
ROPE_THETA = 10000.0
CONV_K = 31
NORM_EPS = 1e-6
LN_EPS = 1e-5
NEG_BIG = -1e30

VMEM_LIMIT = 56 * 1024 * 1024


def _sigmoid(x):
    return 1.0 / (1.0 + jnp.exp(-x))


def _silu(x):
    return x * _sigmoid(x)


def _softplus(x):
    return jnp.maximum(x, 0.0) + jnp.log(1.0 + jnp.exp(-jnp.abs(x)))


def _cparams(sem):
    return pltpu.CompilerParams(dimension_semantics=sem, vmem_limit_bytes=VMEM_LIMIT)


def _ada_kernel(c_ref, w_ref, b_ref, o_ref):
    s = _silu(c_ref[...])
    o_ref[0] = jnp.dot(s, w_ref[0], preferred_element_type=F32, precision=HIGHEST) + b_ref[0]


def _ada(cond, ada_w, ada_b):
    depth, d, d3 = ada_w.shape
    rows = cond.shape[0]
    nt = d3 // d
    return pl.pallas_call(
        _ada_kernel,
        out_shape=jax.ShapeDtypeStruct((depth, rows, d3), F32),
        grid=(depth, nt),
        in_specs=[
            pl.BlockSpec((rows, d), lambda l, j: (0, 0)),
            pl.BlockSpec((1, d, d), lambda l, j: (l, 0, j)),
            pl.BlockSpec((1, 1, d), lambda l, j: (l, 0, j)),
        ],
        out_specs=pl.BlockSpec((1, rows, d), lambda l, j: (l, 0, j)),
        compiler_params=_cparams(("parallel", "parallel")),
        name="ada",
    )(cond, ada_w, ada_b.reshape(depth, 1, d3))


def _inproj0_kernel(x_ref, xp_ref, xn_ref, ctx_ref, sh_ref, sc_ref, g_ref, w_ref, cw_ref, cs_ref, sn_ref, qg_ref,
                    kg_ref, alog_ref, dtb_ref,
                    qkv_o, za_o, qb_o, kb_o, vb_o, zb_o, gb_o, h_scr, *, n_lat_tiles, n_tiles, secs, halo,
                    n_gdn_heads):
    step_id = pl.program_id(0)
    wslot = step_id % 2
    rslot = 1 - wslot
    tm = x_ref.shape[1]
    last = pl.num_programs(0) - 2
    i = jnp.minimum(step_id, last) % n_tiles

    @pl.when(step_id == 0)
    def _():
        h_scr[...] = jnp.zeros_like(h_scr)

    def modulate_tile():
        mult = g_ref[...] * (1.0 + sc_ref[0])
        shift = sh_ref[0]

        def modulated(xf):
            rs = lax.rsqrt(jnp.mean(xf * xf, axis=-1, keepdims=True) + NORM_EPS)
            return xf * rs * mult + shift

        prev_ok = jnp.logical_and(i > 0, i < n_lat_tiles)
        next_ok = i < n_lat_tiles - 1
        h_scr[wslot, 0:halo, :] = jnp.where(prev_ok, modulated(xp_ref[0]), 0.0).astype(BF16)
        h_scr[wslot, halo + tm:2 * halo + tm, :] = jnp.where(next_ok, modulated(xn_ref[0]), 0.0).astype(BF16)
        src = jnp.where(i < n_lat_tiles, x_ref[0], ctx_ref[0])
        h_scr[wslot, halo:halo + tm, :] = modulated(src).astype(BF16)

    def proj(name):
        c0, c1 = secs[name]
        return lambda: jnp.dot(h_scr[rslot, halo:halo + tm, :], w_ref[:, c0:c1], preferred_element_type=F32)

    pad = SHORT_CONV // 2
    pw = 2 * A_DK
    hpb = pw // A_DK

    def qkv_mm(pb):
        c0 = secs["qkv"][0] + pb * pw
        return lambda: jnp.dot(h_scr[rslot], w_ref[:, c0:c0 + pw], preferred_element_type=F32)

    def qkv_epi(pb):
        def epi(y):
            for sb in range(hpb):
                hb = pb * hpb + sb
                cl = slice(hb * A_DK, (hb + 1) * A_DK)
                acc = None
                for j in range(SHORT_CONV):
                    term = y[halo - pad + j:halo - pad + j + tm, sb * A_DK:(sb + 1) * A_DK] * cw_ref[j:j + 1, cl]
                    acc = term if acc is None else acc + term
                u = _silu(acc)
                if hb < 2 * n_gdn_heads:
                    u = u * lax.rsqrt(jnp.sum(u * u, axis=-1, keepdims=True) + NORM_EPS)
                    if hb < n_gdn_heads:
                        u = u * (A_DK ** -0.5)
                qkv_o[0, :, cl] = u.astype(BF16)
        return epi

    def gate_epi(o_ref):
        def epi(y):
            o_ref[0] = _silu(y).astype(BF16)
        return epi

    def plain_epi(y):
        vb_o[0] = y.astype(BF16)

    def norm_rope_epi(o_ref, g_ref_, out_scale):
        def epi(y):
            cs = cs_ref[...]
            sn = sn_ref[...]
            g = g_ref_[...]
            for hh in range(y.shape[1] // B_HD):
                sl = slice(hh * B_HD, (hh + 1) * B_HD)
                yh = y[:, sl]
                ms = jnp.mean(yh * yh, axis=-1, keepdims=True)
                yn = yh * lax.rsqrt(ms + NORM_EPS) * g
                o_ref[0, :, sl] = ((yn * cs + pltpu.roll(yn, B_HD // 2, 1) * sn) * out_scale).astype(BF16)
        return epi

    def decay_epi(ab):
        nab = gb_o.shape[2]
        gval = -jnp.exp(alog_ref[...]) * _softplus(ab + dtb_ref[...])
        bval = _sigmoid(ab)
        lane = lax.broadcasted_iota(jnp.int32, ab.shape, 1)
        gb_o[0] = jnp.where(lane < nab // 2, gval, bval)[:, :nab]

    n_qkv_blocks = (secs["qkv"][1] - secs["qkv"][0]) // pw
    heavy = [(qkv_mm(pb), qkv_epi(pb)) for pb in range(n_qkv_blocks)]
    light = [(proj("za"), gate_epi(za_o)), (proj("zb"), gate_epi(zb_o)), (proj("vb"), plain_epi),
             (proj("qb"), norm_rope_epi(qb_o, qg_ref, B_HD ** -0.5)), (proj("kb"), norm_rope_epi(kb_o, kg_ref, 1.0)),
             (proj("ab"), decay_epi)]
    stages = []
    for k in range(max(len(heavy), len(light))):
        stages += heavy[k:k + 1] + light[k:k + 1]
    y_next = stages[0][0]()
    modulate_tile()
    for k, (_, epi) in enumerate(stages):
        y = y_next
        if k + 1 < len(stages):
            y_next = stages[k + 1][0]()
        epi(y)


def _inproj0(x, ctx, mod0, pre_g, w2, secs, conv_w, cs_tab, sn_tab, qg, kg, alog, dtb, tm, n_gdn_heads):
    bsz, t, d = x.shape
    nctx = ctx.shape[1]
    tall = t + nctx
    assert t % tm == 0 and nctx == tm
    n_lat = t // tm
    ntile = n_lat + 1
    nab = 16
    halo = 16
    nhb = tm // halo
    last_hblk = t // halo - 1

    def widths(name):
        return secs[name][1] - secs[name][0]

    n_steps = bsz * ntile

    def cur(s):
        p = jnp.minimum(s, n_steps - 1)
        return p // ntile, p % ntile

    def prev(s):
        p = jnp.maximum(s - 1, 0)
        return p // ntile, p % ntile

    def row_spec(c):
        return pl.BlockSpec((1, tm, c), lambda s: (prev(s)[0], prev(s)[1], 0))

    def mod_row(s):
        b, i = cur(s)
        return jnp.where(i < n_lat, b, ctx_row)

    mod_rows = mod0.shape[0]
    mod3 = mod0.reshape(mod_rows, 1, 3 * d)
    ctx_row = bsz
    outs = [
        jax.ShapeDtypeStruct((bsz, tall, widths("qkv")), BF16),
        jax.ShapeDtypeStruct((bsz, tall, widths("za")), BF16),
        jax.ShapeDtypeStruct((bsz, tall, widths("qb")), BF16),
        jax.ShapeDtypeStruct((bsz, tall, widths("kb")), BF16),
        jax.ShapeDtypeStruct((bsz, tall, widths("vb")), BF16),
        jax.ShapeDtypeStruct((bsz, tall, widths("zb")), BF16),
        jax.ShapeDtypeStruct((bsz, tall, nab), F32),
    ]
    kern = functools.partial(_inproj0_kernel, n_lat_tiles=n_lat, n_tiles=ntile, secs=secs, halo=halo,
                             n_gdn_heads=n_gdn_heads)
    return pl.pallas_call(
        kern,
        out_shape=outs,
        grid=(n_steps + 1,),
        in_specs=[
            pl.BlockSpec((1, tm, d), lambda s: (cur(s)[0], jnp.minimum(cur(s)[1], n_lat - 1), 0)),
            pl.BlockSpec((1, halo, d), lambda s: (cur(s)[0], jnp.clip(cur(s)[1] * nhb - 1, 0, last_hblk), 0)),
            pl.BlockSpec((1, halo, d), lambda s: (cur(s)[0], jnp.clip((cur(s)[1] + 1) * nhb, 0, last_hblk), 0)),
            pl.BlockSpec((1, tm, d), lambda s: (cur(s)[0], 0, 0)),
            pl.BlockSpec((1, 1, d), lambda s: (mod_row(s), 0, 0)),
            pl.BlockSpec((1, 1, d), lambda s: (mod_row(s), 0, 1)),
            pl.BlockSpec((1, d), lambda s: (0, 0)),
            pl.BlockSpec(w2.shape, lambda s: (0, 0)),
            pl.BlockSpec(conv_w.shape, lambda s: (0, 0)),
            pl.BlockSpec((tm, B_HD), lambda s: (prev(s)[1], 0)),
            pl.BlockSpec((tm, B_HD), lambda s: (prev(s)[1], 0)),
            pl.BlockSpec((1, B_HD), lambda s: (0, 0)),
            pl.BlockSpec((1, B_HD), lambda s: (0, 0)),
            pl.BlockSpec((1, 128), lambda s: (0, 0)),
            pl.BlockSpec((1, 128), lambda s: (0, 0)),
        ],
        out_specs=[row_spec(widths("qkv")), row_spec(widths("za")), row_spec(widths("qb")), row_spec(widths("kb")),
                   row_spec(widths("vb")), row_spec(widths("zb")), row_spec(nab)],
        scratch_shapes=[pltpu.VMEM((2, tm + 2 * halo, d), BF16)],
        compiler_params=_cparams(("arbitrary",)),
        name="inproj0",
    )(x, x, x, ctx, mod3, mod3, pre_g.reshape(1, d), w2, conv_w, cs_tab, sn_tab, qg, kg, alog, dtb)


def _gdn_kernel(q_ref, k_ref, v_ref, gr_ref, sz_ref, ng_ref, o_ref,
                nq_scr, z_scr, au_scr, gt_scr, cr_scr, s_scr, oacc, *, n_lat_chunks, n_chunks, group):
    C = GDN_CHUNK
    step_id = pl.program_id(0)
    wset = step_id % 2
    rset = 1 - wset
    ri = lax.broadcasted_iota(jnp.int32, (C, C), 0)
    ci = lax.broadcasted_iota(jnp.int32, (C, C), 1)
    incl = (ri >= ci, ri <= ci)
    strict = (ri > ci, ri < ci)
    eye = (ri == ci).astype(F32)
    tri_row = ((ri <= ci).astype(F32), (ri >= ci).astype(F32))
    tri_col = (incl[0].astype(F32), incl[1].astype(F32))

    @pl.when(step_id == 0)
    def _():
        nq_scr[...] = jnp.zeros_like(nq_scr)
        z_scr[...] = jnp.zeros_like(z_scr)
        au_scr[...] = jnp.zeros_like(au_scr)
        gt_scr[...] = jnp.zeros_like(gt_scr)

    for d in range(2):
        cr_scr[d] = jnp.dot(gr_ref[0, 0, d], tri_row[d], preferred_element_type=F32, precision=HIGHEST)
    s_scr[...] = jnp.zeros_like(s_scr)
    oacc[...] = jnp.zeros_like(oacc)

    nt = (((1,), (1,)), ((), ()))
    nsq = int(math.log2(INV_BLOCK)) - 1
    diag_blk = (ri // INV_BLOCK) == (ci // INV_BLOCK)
    off_blks = []
    bs = INV_BLOCK
    while bs < C:
        off_blks.append(jnp.logical_and((ri // (2 * bs)) == (ci // (2 * bs)), (ri // bs) != (ci // bs)))
        bs *= 2

    def mm(a, b):
        return jnp.dot(a, b, preferred_element_type=F32)

    tl = (((0,), (0,)), ((), ()))

    def prep(chunk_ids):
        ch = []
        for n in chunk_ids:
            rows = pl.ds(pl.multiple_of(n * C, C), C)
            kb = k_ref[0, rows, :]
            qb = q_ref[0, rows, :]
            k32 = kb.astype(F32)
            q32 = qb.astype(F32)
            v32 = v_ref[0, rows, :].astype(F32)
            kk = lax.dot_general(kb, kb, nt, preferred_element_type=F32)
            qk = lax.dot_general(qb, kb, nt, preferred_element_type=F32)
            for d in range(2):
                g_row = gr_ref[0, 0, d, pl.ds(n, 1), :]
                beta_row = gr_ref[0, 0, 2 + d, pl.ds(n, 1), :]
                c_row = cr_scr[d, pl.ds(n, 1), :]
                c_col = jnp.sum(tri_col[d] * g_row, axis=-1, keepdims=True)
                beta_col = jnp.sum(eye * beta_row, axis=-1, keepdims=True)
                tot = jnp.sum(g_row, axis=-1, keepdims=True)
                decay = jnp.exp(jnp.where(incl[d], c_col - c_row, NEG_BIG))
                x = jnp.where(strict[d], kk * decay, 0.0) * (-beta_col)
                e_col = jnp.exp(c_col)
                rhs = jnp.concatenate([v32 * beta_col, k32 * (beta_col * e_col)], axis=1)
                qd = q32 * e_col
                kd = (k32 * jnp.exp(tot - c_col)).astype(BF16)
                am = jnp.where(incl[d], qk * decay, 0.0).astype(BF16)
                gt_scr[2 * wset + d, n] = jnp.broadcast_to(jnp.exp(tot), (8, 128))
                ch.append((d, n, rows, x, rhs, qd, kd, am))
        yield
        xs = [c[3] for c in ch]
        ys = [jnp.where(diag_blk, x, 0.0) for x in xs]
        xbs = [y.astype(BF16) for y in ys]
        ps = [mm(xb, xb) for xb in xbs]
        yield
        for m in range(nsq):
            pbs = [p.astype(BF16) for p in ps]
            ybs = [y.astype(BF16) for y in ys]
            if m < nsq - 1:
                rs = [mm(jnp.concatenate([pb, yb], axis=0), pb) for pb, yb in zip(pbs, ybs)]
                ys = [y + p + r[C:] for y, p, r in zip(ys, ps, rs)]
                ps = [r[:C] for r in rs]
            else:
                ys = [y + p + mm(yb, pb) for y, p, yb, pb in zip(ys, ps, ybs, pbs)]
            yield
        for off in off_blks:
            xos = [jnp.where(off, x, 0.0) for x in xs]
            ybs = [y.astype(BF16) for y in ys]
            ts = [xo + mm(yb, xo.astype(BF16)) for xo, yb in zip(xos, ybs)]
            yield
            ys = [y + t + mm(t.astype(BF16), yb) for y, t, yb in zip(ys, ts, ybs)]
            yield
        ybs = [y.astype(BF16) for y in ys]
        uws = [c[4] + mm(yb, c[4].astype(BF16)) for c, yb in zip(ch, ybs)]
        uwbs = [uw.astype(BF16) for uw in uws]
        yield
        kzs = [lax.dot_general(c[6], uwb, tl, preferred_element_type=F32) for c, uwb in zip(ch, uwbs)]
        azs = [mm(c[7], uwb) for c, uwb in zip(ch, uwbs)]
        for c, kz, az in zip(ch, kzs, azs):
            cd, n, rows = 2 * wset + c[0], c[1], c[2]
            z_scr[cd, n] = kz[:, :A_DV].astype(z_scr.dtype)
            nq_scr[cd, n, 0:A_DK, :] = kz[:, A_DV:].astype(BF16)
            nq_scr[cd, n, A_DK:A_DK + C, :] = (c[5] - az[:, A_DV:]).astype(BF16)
            au_scr[cd, rows, :] = az[:, :A_DV].astype(au_scr.dtype)
        yield

    def scan_step(ns, with_out):
        nrow = A_DK + C if with_out else A_DK
        ss = [s_scr[d] for d in range(2)]
        rs = [mm(nq_scr[2 * rset + d, ns[d], 0:nrow, :], ss[d].astype(BF16)) for d in range(2)]
        for d in range(2):
            cd, n = 2 * rset + d, ns[d]
            s_scr[d] = ss[d] * gt_scr[cd, n][0:1, :] - rs[d][:A_DK] + z_scr[cd, n].astype(F32)
            if with_out:
                rows = pl.ds(pl.multiple_of(n * C, C), C)
                o = oacc[rows, :] + rs[d][A_DK:] + au_scr[cd, rows, :].astype(F32)
                oacc[rows, :] = o
                on = o * lax.rsqrt(jnp.mean(o * o, axis=-1, keepdims=True) + NORM_EPS) * ng_ref[...]
                o_ref[0, rows, :] = (on * sz_ref[0, rows, :].astype(F32)).astype(BF16)

    def interleave(stages, scan_steps):
        pending = list(scan_steps)
        for _ in stages:
            if pending:
                pending.pop(0)()
        for rest in pending:
            rest()

    n_ctx_chunks = n_chunks - n_lat_chunks
    interleave(prep([n_lat_chunks + i for i in range(n_ctx_chunks)]),
               [functools.partial(scan_step, (n_lat_chunks + j, n_chunks - 1 - j), False)
                for j in range(n_ctx_chunks)])

    def body(gi, carry):
        base = gi * group
        interleave(prep([base + i for i in range(group)]),
                   [functools.partial(scan_step, (base + j, n_lat_chunks - 1 - base - j), True)
                    for j in range(group)])
        return carry

    lax.fori_loop(0, n_lat_chunks // group, body, 0)


def _gdn(qkvn, grow, sza, norm_g, n_heads, t):
    bsz, tall, _ = qkvn.shape
    C = GDN_CHUNK
    n_chunks = tall // C
    n_lat_chunks = t // C
    ncp = grow.shape[3]
    group = 8
    assert n_lat_chunks % group == 0
    kern = functools.partial(_gdn_kernel, n_lat_chunks=n_lat_chunks, n_chunks=n_chunks, group=group)
    n_pairs = bsz * n_heads

    def cur(s):
        p = jnp.minimum(s, n_pairs - 1)
        return p // n_heads, p % n_heads

    def prev(s):
        p = jnp.maximum(s - 1, 0)
        return p // n_heads, p % n_heads

    return pl.pallas_call(
        kern,
        out_shape=jax.ShapeDtypeStruct((bsz, t, n_heads * A_DV), BF16),
        grid=(n_pairs + 1,),
        in_specs=[
            pl.BlockSpec((1, tall, A_DK), lambda s: (cur(s)[0], 0, cur(s)[1])),
            pl.BlockSpec((1, tall, A_DK), lambda s: (cur(s)[0], 0, n_heads + cur(s)[1])),
            pl.BlockSpec((1, tall, A_DV), lambda s: (cur(s)[0], 0, 2 * n_heads + cur(s)[1])),
            pl.BlockSpec((1, 1, 4, ncp, C), lambda s: (cur(s)[0], cur(s)[1], 0, 0, 0)),
            pl.BlockSpec((1, t, A_DV), lambda s: (prev(s)[0], 0, prev(s)[1])),
            pl.BlockSpec((1, A_DV), lambda s: (0, 0)),
        ],
        out_specs=pl.BlockSpec((1, t, A_DV), lambda s: (prev(s)[0], 0, prev(s)[1])),
        scratch_shapes=[
            pltpu.VMEM((4, n_chunks, A_DK + C, A_DK), BF16),
            pltpu.VMEM((4, n_chunks, A_DK, A_DV), BF16),
            pltpu.VMEM((4, tall, A_DV), BF16),
            pltpu.VMEM((4, n_chunks, 8, 128), F32),
            pltpu.VMEM((2, ncp, C), F32),
            pltpu.VMEM((2, A_DK, A_DV), F32),
            pltpu.VMEM((t, A_DV), F32),
        ],
        compiler_params=_cparams(("arbitrary",)),
        name="gdn",
    )(qkvn, qkvn, qkvn, grow, sza, norm_g)


def _attn_kernel(q_ref, k_ref, v_ref, sz_ref, o_ref, *, kv_blocks):
    nt = (((1,), (1,)), ((), ()))
    n_heads = q_ref.shape[2] // B_HD
    items = [(hh, blk) for hh in range(n_heads) for blk in kv_blocks]

    def scores(item):
        hh, (k0, k1) = item
        q = q_ref[0, :, hh * B_HD:(hh + 1) * B_HD]
        return lax.dot_general(q, k_ref[0, k0:k1, :], nt, preferred_element_type=F32)

    s_next = scores(items[0])
    m = l = acc = None
    for idx, (hh, (k0, k1)) in enumerate(items):
        s = s_next
        if idx + 1 < len(items):
            s_next = scores(items[idx + 1])
        bm = jnp.max(s, axis=-1, keepdims=True)
        if k0 == kv_blocks[0][0]:
            m = bm
            p = jnp.exp(s - m)
            l = jnp.sum(p, axis=-1, keepdims=True)
            acc = jnp.dot(p.astype(BF16), v_ref[0, k0:k1, :], preferred_element_type=F32)
        else:
            m_new = jnp.maximum(m, bm)
            alpha = jnp.exp(m - m_new)
            p = jnp.exp(s - m_new)
            l = alpha * l + jnp.sum(p, axis=-1, keepdims=True)
            acc = alpha * acc + jnp.dot(p.astype(BF16), v_ref[0, k0:k1, :], preferred_element_type=F32)
            m = m_new
        if k1 == kv_blocks[-1][1]:
            sl = slice(hh * B_HD, (hh + 1) * B_HD)
            o_ref[0, :, sl] = (acc * (1.0 / l) * sz_ref[0, :, sl].astype(F32)).astype(BF16)


def _attn(qb, kb, vb, szb, t, tq):
    bsz, tall, qw = qb.shape
    n_kv = kb.shape[2] // B_HD
    gw = qw // n_kv
    kvb = 1024
    edges = list(range(0, tall, kvb)) + [tall]
    kv_blocks = tuple((edges[j], edges[j + 1]) for j in range(len(edges) - 1))
    return pl.pallas_call(
        functools.partial(_attn_kernel, kv_blocks=kv_blocks),
        out_shape=jax.ShapeDtypeStruct((bsz, t, qw), BF16),
        grid=(bsz, n_kv, t // tq),
        in_specs=[
            pl.BlockSpec((1, tq, gw), lambda b, g, i: (b, i, g)),
            pl.BlockSpec((1, tall, B_HD), lambda b, g, i: (b, 0, g)),
            pl.BlockSpec((1, tall, B_HD), lambda b, g, i: (b, 0, g)),
            pl.BlockSpec((1, tq, gw), lambda b, g, i: (b, i, g)),
        ],
        out_specs=pl.BlockSpec((1, tq, gw), lambda b, g, i: (b, i, g)),
        compiler_params=_cparams(("parallel", "parallel", "arbitrary")),
        name="attn",
    )(qb, kb, vb, szb)


def _mid_kernel(ya_ref, yb_ref, x_ref, wa_ref, wb_ref, pg_ref, gate_ref, sh_ref, sc_ref, g1_ref, w1_ref, b1_ref,
                xl_o, u_o, sz_o, h_scr):
    d = x_ref.shape[2]
    tm = x_ref.shape[1]
    nsub = 2
    ts = tm // nsub
    mult1 = g1_ref[...] * (1.0 + sc_ref[0])

    def out_proj(r):
        return (jnp.dot(ya_ref[0, r, :], wa_ref[...], preferred_element_type=F32)
                + jnp.dot(yb_ref[0, r, :], wb_ref[...], preferred_element_type=F32))

    def residual_and_modulate(r, out):
        on = out * lax.rsqrt(jnp.mean(out * out, axis=-1, keepdims=True) + NORM_EPS) * pg_ref[...]
        xl = x_ref[0, r, :] + gate_ref[0] * on
        xl_o[0, r, :] = xl
        rs = lax.rsqrt(jnp.mean(xl * xl, axis=-1, keepdims=True) + NORM_EPS)
        h_scr[r, :] = (xl * rs * mult1 + sh_ref[0]).astype(BF16)

    def in_proj(r):
        h = h_scr[r, :]
        a = jnp.dot(h, w1_ref[:, 0:d], preferred_element_type=F32) + b1_ref[:, 0:d]
        gl = jnp.dot(h, w1_ref[:, d:2 * d], preferred_element_type=F32) + b1_ref[:, d:2 * d]
        u_o[0, r, :] = (a * _sigmoid(gl)).astype(BF16)
        z = jnp.dot(h, w1_ref[:, 2 * d:3 * d], preferred_element_type=F32) + b1_ref[:, 2 * d:3 * d]
        sz_o[0, r, :] = _silu(z).astype(BF16)

    subs = [slice(k * ts, (k + 1) * ts) for k in range(nsub)]
    outs = [out_proj(subs[0])]
    for k in range(nsub):
        if k + 1 < nsub:
            outs.append(out_proj(subs[k + 1]))
        residual_and_modulate(subs[k], outs[k])
        if k > 0:
            in_proj(subs[k - 1])
    in_proj(subs[nsub - 1])


def _mid(ya, yb, x, wa, wb, post_g, mod0, mod1, pre_g1, w1, b1, tm):
    bsz, t, d = x.shape
    aw = ya.shape[2]
    bw = yb.shape[2]
    rows = mod0.shape[0]
    m0 = mod0.reshape(rows, 1, 3 * d)
    m1 = mod1.reshape(rows, 1, 3 * d)
    tok = lambda c: pl.BlockSpec((1, tm, c), lambda b, i: (b, i, 0))
    const2 = lambda a: pl.BlockSpec(a.shape, lambda b, i: (0, 0))
    return pl.pallas_call(
        _mid_kernel,
        out_shape=[jax.ShapeDtypeStruct((bsz, t, d), F32),
                   jax.ShapeDtypeStruct((bsz, t, d), BF16),
                   jax.ShapeDtypeStruct((bsz, t, d), BF16)],
        grid=(bsz, t // tm),
        in_specs=[
            tok(aw), tok(bw), tok(d),
            const2(wa), const2(wb),
            pl.BlockSpec((1, d), lambda b, i: (0, 0)),
            pl.BlockSpec((1, 1, d), lambda b, i: (b, 0, 2)),
            pl.BlockSpec((1, 1, d), lambda b, i: (b, 0, 0)),
            pl.BlockSpec((1, 1, d), lambda b, i: (b, 0, 1)),
            pl.BlockSpec((1, d), lambda b, i: (0, 0)),
            const2(w1),
            pl.BlockSpec((1, 3 * d), lambda b, i: (0, 0)),
        ],
        out_specs=[tok(d), tok(d), tok(d)],
        scratch_shapes=[pltpu.VMEM((tm, d), BF16)],
        compiler_params=_cparams(("parallel", "parallel")),
        name="mid",
    )(ya, yb, x, wa, wb, post_g.reshape(1, d), m0, m1, m1, pre_g1.reshape(1, d), w1, b1.reshape(1, 3 * d))


def _tail_kernel(cur_ref, prev_ref, next_ref, sz_ref, xl_ref, dw_ref, dwb_ref, lng_ref, lnb_ref, wo_ref, bo_ref,
                 pg_ref, gate_ref, o_ref, buf, act, *, halo):
    i = pl.program_id(1)
    nt = pl.num_programs(1)
    tm = cur_ref.shape[1]
    d = cur_ref.shape[2]
    buf[0:halo, :] = jnp.where(i > 0, prev_ref[0].astype(F32), 0.0)
    buf[halo:halo + tm, :] = cur_ref[0].astype(F32)
    buf[halo + tm:2 * halo + tm, :] = jnp.where(i < nt - 1, next_ref[0].astype(F32), 0.0)
    pad = CONV_K // 2
    cw = 128
    sub = 8

    def conv_rows(r0, nr):
        for cb in range(d // cw):
            cs = slice(cb * cw, (cb + 1) * cw)
            acc = None
            for s in range(sub):
                part = None
                for a in range((halo + pad) // sub + 1):
                    j = sub * a + s - (halo - pad)
                    if 0 <= j < CONV_K:
                        term = buf[r0 + sub * a:r0 + sub * a + nr + sub, cs] * dw_ref[j:j + 1, cs]
                        part = term if part is None else part + term
                if part is not None:
                    acc = part[s:s + nr] if acc is None else acc + part[s:s + nr]
            act[r0:r0 + nr, cs] = acc + dwb_ref[:, cs]

    def activate(r):
        u = act[r, :]
        mu = jnp.mean(u, axis=-1, keepdims=True)
        uc = u - mu
        var = jnp.mean(uc * uc, axis=-1, keepdims=True)
        un = uc * lax.rsqrt(var + LN_EPS) * lng_ref[...] + lnb_ref[...]
        hact = (_silu(un) * sz_ref[0, r, :].astype(F32)).astype(BF16)
        return jnp.dot(hact, wo_ref[...], preferred_element_type=F32) + bo_ref[...]

    def finish(r, out):
        on = out * lax.rsqrt(jnp.mean(out * out, axis=-1, keepdims=True) + NORM_EPS) * pg_ref[...]
        o_ref[0, r, :] = xl_ref[0, r, :] + gate_ref[0] * on

    nsub = 2
    ts = tm // nsub
    pending = None
    for k in range(nsub):
        r = slice(k * ts, (k + 1) * ts)
        conv_rows(k * ts, ts)
        out = activate(r)
        if pending is not None:
            finish(*pending)
        pending = (r, out)
    finish(*pending)


def _tail(u, sz, xl, dw_w, dw_b, ln_g, ln_b, w_out, b_out, post_g, mod1, tm):
    bsz, t, d = xl.shape
    halo = 16
    nhb = tm // halo
    last_blk = t // halo - 1
    rows = mod1.shape[0]
    m1 = mod1.reshape(rows, 1, 3 * d)
    tok = lambda c: pl.BlockSpec((1, tm, c), lambda b, i: (b, i, 0))
    vec = lambda: pl.BlockSpec((1, d), lambda b, i: (0, 0))
    kern = functools.partial(_tail_kernel, halo=halo)
    return pl.pallas_call(
        kern,
        out_shape=jax.ShapeDtypeStruct((bsz, t, d), F32),
        grid=(bsz, t // tm),
        in_specs=[
            tok(d),
            pl.BlockSpec((1, halo, d), lambda b, i: (b, jnp.maximum(i * nhb - 1, 0), 0)),
            pl.BlockSpec((1, halo, d), lambda b, i: (b, jnp.minimum((i + 1) * nhb, last_blk), 0)),
            tok(d), tok(d),
            pl.BlockSpec(dw_w.shape, lambda b, i: (0, 0)),
            vec(), vec(), vec(),
            pl.BlockSpec(w_out.shape, lambda b, i: (0, 0)),
            vec(), vec(),
            pl.BlockSpec((1, 1, d), lambda b, i: (b, 0, 2)),
        ],
        out_specs=tok(d),
        scratch_shapes=[pltpu.VMEM((tm + 2 * halo, d), F32), pltpu.VMEM((tm, d), F32)],
        compiler_params=_cparams(("parallel", "parallel")),
        name="tail",
    )(u, u, u, sz, xl, dw_w, dw_b.reshape(1, d), ln_g.reshape(1, d), ln_b.reshape(1, d), w_out,
      b_out.reshape(1, d), post_g.reshape(1, d), m1)


def _conformer_kernel(ya_ref, yb_ref, x_ref, wa_ref, wb_ref, pg0_ref, gate0_ref, sh_ref, sc_ref, g1_ref, w1_ref,
                      b1_ref, dw_ref, dwb_ref, lng_ref, lnb_ref, wo_ref, bo_ref, pg1_ref, gate1_ref, zero_ref,
                      o_ref, h_scr, u_ring, sz_ring, xl_ring, buf, act, *, tiles_per_seq, halo):
    step_id = pl.program_id(0)
    d = x_ref.shape[2]
    tm = x_ref.shape[1]
    nsub = 2
    ts = tm // nsub
    subs = [slice(k * ts, (k + 1) * ts) for k in range(nsub)]
    u_cur, u_prev, u_pprev = step_id % 3, (step_id + 2) % 3, (step_id + 1) % 3
    r_cur, r_prev = step_id % 2, (step_id + 1) % 2
    ip = jnp.maximum(step_id - 1, 0) % tiles_per_seq

    @pl.when(step_id == 0)
    def _():
        u_ring[...] = jnp.zeros_like(u_ring)
        sz_ring[...] = jnp.zeros_like(sz_ring)
        xl_ring[...] = jnp.zeros_like(xl_ring)

    pad = CONV_K // 2
    cw = 128
    sub = 8
    buf[0:halo, :] = jnp.where(ip > 0, u_ring[u_pprev, tm - halo:tm, :].astype(F32), 0.0)
    buf[halo:halo + tm, :] = u_ring[u_prev].astype(F32)

    def conv_piece(r0, nr, cb):
        cs = slice(cb * cw, (cb + 1) * cw)
        acc = None
        for s in range(sub):
            part = None
            for a in range((halo + pad) // sub + 1):
                j = sub * a + s - (halo - pad)
                if 0 <= j < CONV_K:
                    term = buf[r0 + sub * a:r0 + sub * a + nr + sub, cs] * dw_ref[j:j + 1, cs]
                    part = term if part is None else part + term
            if part is not None:
                acc = part[s:s + nr] if acc is None else acc + part[s:s + nr]
        act[r0:r0 + nr, cs] = acc + dwb_ref[:, cs]

    def activate(r):
        u = act[r, :]
        mu = jnp.mean(u, axis=-1, keepdims=True)
        uc = u - mu
        var = jnp.mean(uc * uc, axis=-1, keepdims=True)
        un = uc * lax.rsqrt(var + LN_EPS) * lng_ref[...] + lnb_ref[...]
        hact = (_silu(un) * sz_ring[r_prev, r, :].astype(F32)).astype(BF16)
        return jnp.dot(hact, wo_ref[...], preferred_element_type=F32) + bo_ref[...]

    def finish(r, out):
        on = out * lax.rsqrt(jnp.mean(out * out, axis=-1, keepdims=True) + NORM_EPS) * pg1_ref[...]
        o_ref[0, r, :] = xl_ring[r_prev, r, :] + gate1_ref[0] * on

    assert nsub == 2 and ts + 2 * halo <= tm + halo
    ncb = d // cw
    first_half = [functools.partial(conv_piece, 0, ts, cb) for cb in range(ncb)]
    second_half = [functools.partial(conv_piece, ts, ts, cb) for cb in range(ncb)]

    def run(pieces, n):
        for _ in range(min(n, len(pieces))):
            pieces.pop(0)()

    mult1 = g1_ref[...] * (1.0 + sc_ref[0])

    def out_proj(r):
        return (jnp.dot(ya_ref[0, r, :], wa_ref[...], preferred_element_type=F32)
                + jnp.dot(yb_ref[0, r, :], wb_ref[...], preferred_element_type=F32))

    def residual_and_modulate(r, out):
        on = out * lax.rsqrt(jnp.mean(out * out, axis=-1, keepdims=True) + NORM_EPS) * pg0_ref[...]
        xl = x_ref[0, r, :] + gate0_ref[0] * on
        xl_ring[r_cur, r, :] = xl
        rs = lax.rsqrt(jnp.mean(xl * xl, axis=-1, keepdims=True) + NORM_EPS)
        h_scr[r, :] = (xl * rs * mult1 + sh_ref[0]).astype(BF16)

    def in_mm(r, part):
        return (jnp.dot(h_scr[r, :], w1_ref[:, part * d:(part + 1) * d], preferred_element_type=F32)
                + b1_ref[:, part * d:(part + 1) * d])

    def glu_store(r, a, gl):
        u_ring[u_cur, r, :] = (a * _sigmoid(gl)).astype(BF16)

    def gate_store(r, z):
        sz_ring[r_cur, r, :] = _silu(z).astype(BF16)

    def tie(r, conv_r0, cb):
        dep = act[conv_r0:conv_r0 + 16, cb * cw:(cb + 1) * cw]
        h_scr[r.start:r.start + 16, 0:cw] = h_scr[r.start:r.start + 16, 0:cw] + (dep * zero_ref[...]).astype(BF16)

    s0, s1 = subs
    o0 = out_proj(s0)
    o1 = out_proj(s1)
    run(first_half, 2)
    residual_and_modulate(s0, o0)
    tie(s0, 0, 1)
    a0 = in_mm(s0, 0)
    run(first_half, 2)
    tie(s0, 0, 3)
    gl0 = in_mm(s0, 1)
    run(first_half, 2)
    tie(s0, 0, 5)
    z0 = in_mm(s0, 2)
    glu_store(s0, a0, gl0)
    gate_store(s0, z0)
    residual_and_modulate(s1, o1)
    run(first_half, ncb)
    tie(s1, 0, ncb - 1)
    a1 = in_mm(s1, 0)
    buf[halo + tm:2 * halo + tm, :] = jnp.where(ip < tiles_per_seq - 1, u_ring[u_cur, 0:halo, :].astype(F32), 0.0)
    run(second_half, 3)
    tie(s1, ts, 2)
    gl1 = in_mm(s1, 1)
    run(second_half, 3)
    tie(s1, ts, 5)
    z1 = in_mm(s1, 2)
    glu_store(s1, a1, gl1)
    gate_store(s1, z1)
    run(second_half, ncb)
    out0 = activate(s0)
    out1 = activate(s1)
    finish(s0, out0)
    finish(s1, out1)


def _conformer(ya, yb, x, wa, wb, post_g0, mod0, mod1, pre_g1, w1, b1, dw_w, dw_b, ln_g, ln_b, w_out, b_out,
               post_g1, tm):
    bsz, t, d = x.shape
    aw = ya.shape[2]
    bw = yb.shape[2]
    halo = 16
    assert halo >= CONV_K // 2 and t % tm == 0
    tiles = t // tm
    n_steps = bsz * tiles
    rows = mod0.shape[0]
    m0 = mod0.reshape(rows, 1, 3 * d)
    m1 = mod1.reshape(rows, 1, 3 * d)

    def cur(s):
        p = jnp.minimum(s, n_steps - 1)
        return p // tiles, p % tiles

    def prev(s):
        p = jnp.maximum(s - 1, 0)
        return p // tiles, p % tiles

    once = pl.Buffered(1)
    tok = lambda c: pl.BlockSpec((1, tm, c), lambda s: (cur(s)[0], cur(s)[1], 0))
    vec = lambda: pl.BlockSpec((1, d), lambda s: (0, 0))
    const = lambda a: pl.BlockSpec(a.shape, lambda s: (0, 0), pipeline_mode=once)
    kern = functools.partial(_conformer_kernel, tiles_per_seq=tiles, halo=halo)
    return pl.pallas_call(
        kern,
        out_shape=jax.ShapeDtypeStruct((bsz, t, d), F32),
        grid=(n_steps + 1,),
        in_specs=[
            tok(aw), tok(bw), tok(d),
            const(wa), const(wb),
            vec(),
            pl.BlockSpec((1, 1, d), lambda s: (cur(s)[0], 0, 2)),
            pl.BlockSpec((1, 1, d), lambda s: (cur(s)[0], 0, 0)),
            pl.BlockSpec((1, 1, d), lambda s: (cur(s)[0], 0, 1)),
            vec(),
            const(w1),
            pl.BlockSpec((1, 3 * d), lambda s: (0, 0)),
            pl.BlockSpec(dw_w.shape, lambda s: (0, 0)),
            vec(), vec(), vec(),
            const(w_out),
            vec(), vec(),
            pl.BlockSpec((1, 1, d), lambda s: (prev(s)[0], 0, 2)),
            pl.BlockSpec((1, 128), lambda s: (0, 0)),
        ],
        out_specs=pl.BlockSpec((1, tm, d), lambda s: (prev(s)[0], prev(s)[1], 0)),
        scratch_shapes=[
            pltpu.VMEM((tm, d), BF16),
            pltpu.VMEM((3, tm, d), BF16),
            pltpu.VMEM((2, tm, d), BF16),
            pltpu.VMEM((2, tm, d), F32),
            pltpu.VMEM((tm + 2 * halo, d), F32),
            pltpu.VMEM((tm, d), F32),
        ],
        compiler_params=_cparams(("arbitrary",)),
        name="conformer",
    )(ya, yb, x, wa, wb, post_g0.reshape(1, d), m0, m1, m1, pre_g1.reshape(1, d), w1, b1.reshape(1, 3 * d),
      dw_w, dw_b.reshape(1, d), ln_g.reshape(1, d), ln_b.reshape(1, d), w_out, b_out.reshape(1, d),
      post_g1.reshape(1, d), m1, jnp.zeros((1, 128), F32))


def _rope_tables(t, nctx):
    f32 = np.float32
    rows = t // GRID_W
    row = np.repeat(np.arange(rows, dtype=f32), GRID_W)
    col = np.tile(np.arange(GRID_W, dtype=f32), rows)
    axis_dim = B_HD // 2
    inv_freq = (f32(ROPE_THETA) ** (-np.arange(0, axis_dim, 2, dtype=f32) / f32(axis_dim))).astype(f32)
    ang = np.concatenate([row[:, None] * inv_freq, col[:, None] * inv_freq], axis=1).astype(f32)
    cs = np.concatenate([np.cos(ang), np.cos(ang)], axis=1)
    sn = np.concatenate([-np.sin(ang), np.sin(ang)], axis=1)
    cs = np.concatenate([cs, np.ones((nctx, B_HD), f32)], axis=0).astype(f32)
    sn = np.concatenate([sn, np.zeros((nctx, B_HD), f32)], axis=0).astype(f32)
    return jnp.asarray(cs), jnp.asarray(sn)


def _head_perm():
    q = B_HD // 4
    return np.concatenate([np.arange(0, q), np.arange(2 * q, 3 * q), np.arange(q, 2 * q), np.arange(3 * q, 4 * q)])


def kernel(x, c, ctx, c_ctx, ada_w, ada_b, pre_norm_g, post_norm_g, ev_w_in, ev_short_conv_w, ev_a_log,
           ev_dt_bias, ev_gdn_norm_g, ev_q_norm_g, ev_k_norm_g, ev_w_out, od_w_in, od_b_in, od_dw_w, od_dw_b,
           od_ln_g, od_ln_b, od_w_out, od_b_out):
    bsz, t, d = x.shape
    nctx = ctx.shape[1]
    a_width = d // 2
    a_heads = a_width // A_DV
    a_qkv = a_heads * (2 * A_DK + A_DV)
    b_width = d - a_width
    b_heads = b_width // B_HD
    b_kv = b_heads // 2
    assert ada_w.shape[0] == 2 and ev_w_in.shape[0] == 1 and od_w_in.shape[0] == 1

    mod_rows = ((bsz + 1 + 7) // 8) * 8
    cond = jnp.zeros((mod_rows, d), F32).at[:bsz].set(c).at[bsz].set(c_ctx)
    mod = _ada(cond, ada_w, ada_b)

    w_in = ev_w_in[0]
    splits = np.cumsum([0, a_qkv, a_width, 2 * a_heads, 2 * a_heads, b_heads * B_HD, b_kv * B_HD, b_kv * B_HD,
                        b_width])
    w_qkv, w_za, w_a, w_b, w_qb, w_kb, w_vb, w_zb = [w_in[:, splits[j]:splits[j + 1]] for j in range(8)]
    perm = _head_perm()
    perm_q = np.concatenate([h * B_HD + perm for h in range(b_heads)])
    perm_k = np.concatenate([h * B_HD + perm for h in range(b_kv)])
    ab_pad = 128 - 4 * a_heads
    pieces = [("qkv", w_qkv), ("za", w_za), ("qb", w_qb[:, perm_q]), ("kb", w_kb[:, perm_k]), ("vb", w_vb),
              ("zb", w_zb), ("ab", jnp.concatenate([w_a, w_b, jnp.zeros((d, ab_pad), F32)], axis=1))]
    secs, off = {}, 0
    for name, wpart in pieces:
        secs[name] = (off, off + wpart.shape[1])
        off += wpart.shape[1]
    w2 = jnp.concatenate([p[1] for p in pieces], axis=1).astype(BF16)
    cs_tab, sn_tab = _rope_tables(t, nctx)
    qg = ev_q_norm_g[0][perm].reshape(1, B_HD)
    kg = ev_k_norm_g[0][perm].reshape(1, B_HD)
    alog = jnp.zeros((1, 128), F32).at[0, :2 * a_heads].set(ev_a_log[0].reshape(-1))
    dtb = jnp.zeros((1, 128), F32).at[0, :2 * a_heads].set(ev_dt_bias[0].reshape(-1))

    tm0 = nctx
    qkvn, sza, qb, kb, vb, szb, gb = _inproj0(x, ctx, mod[0], pre_norm_g[0], w2, secs, ev_short_conv_w[0], cs_tab,
                                              sn_tab, qg, kg, alog, dtb, tm0, a_heads)

    tall = t + nctx
    n_chunks = tall // GDN_CHUNK
    ncp = ((n_chunks + 7) // 8) * 8
    grow = gb.reshape(bsz, n_chunks, GDN_CHUNK, 4, a_heads).transpose(0, 4, 3, 1, 2)
    grow = jnp.pad(grow, ((0, 0), (0, 0), (0, 0), (0, ncp - n_chunks), (0, 0)))
    ya = _gdn(qkvn, grow, sza, ev_gdn_norm_g[0].reshape(1, A_DV), a_heads, t)

    yb = _attn(qb, kb, vb, szb, t, 512)

    w_out0 = ev_w_out[0].astype(BF16)
    return _conformer(ya, yb, x, w_out0[:a_width], w_out0[a_width:], post_norm_g[0], mod[0], mod[1], pre_norm_g[1],
                      od_w_in[0].astype(BF16), od_b_in[0], od_dw_w[0], od_dw_b[0], od_ln_g[0], od_ln_b[0],
                      od_w_out[0].astype(BF16), od_b_out[0], post_norm_g[1], 512)
```

```python
import functools
import math

import numpy as np
import jax
import jax.numpy as jnp
from jax import lax
from jax.experimental import pallas as pl
from jax.experimental.pallas import tpu as pltpu

F32 = jnp.float32
BF16 = jnp.bfloat16
HIGHEST = lax.Precision.HIGHEST

GRID_W = 64
A_DK = 128
A_DV = 128
SHORT_CONV = 5
GDN_CHUNK = 64
INV_BLOCK = 16
B_HD = 128
ROPE_THETA = 10000.0
CONV_K = 31
NORM_EPS = 1e-6
LN_EPS = 1e-5
NEG_BIG = -1e30

VMEM_LIMIT = 56 * 1024 * 1024


def _sigmoid(x):
    return 1.0 / (1.0 + jnp.exp(-x))


def _silu(x):
    return x * _sigmoid(x)


def _softplus(x):
    return jnp.maximum(x, 0.0) + jnp.log(1.0 + jnp.exp(-jnp.abs(x)))


def _cparams(sem):
    return pltpu.CompilerParams(dimension_semantics=sem, vmem_limit_bytes=VMEM_LIMIT)


def _ada_kernel(c_ref, w_ref, b_ref, o_ref):
    s = _silu(c_ref[...])
    o_ref[0] = jnp.dot(s, w_ref[0], preferred_element_type=F32, precision=HIGHEST) + b_ref[0]


def _ada(cond, ada_w, ada_b):
    depth, d, d3 = ada_w.shape
    rows = cond.shape[0]
    nt = d3 // d
    return pl.pallas_call(
        _ada_kernel,
        out_shape=jax.ShapeDtypeStruct((depth, rows, d3), F32),
        grid=(depth, nt),
        in_specs=[
            pl.BlockSpec((rows, d), lambda l, j: (0, 0)),
            pl.BlockSpec((1, d, d), lambda l, j: (l, 0, j)),
            pl.BlockSpec((1, 1, d), lambda l, j: (l, 0, j)),
        ],
        out_specs=pl.BlockSpec((1, rows, d), lambda l, j: (l, 0, j)),
        compiler_params=_cparams(("parallel", "parallel")),
        name="ada",
    )(cond, ada_w, ada_b.reshape(depth, 1, d3))


def _inproj0_kernel(x_ref, xp_ref, xn_ref, ctx_ref, sh_ref, sc_ref, g_ref, w_ref, cw_ref, cs_ref, sn_ref, qg_ref,
                    kg_ref, alog_ref, dtb_ref,
                    qkv_o, za_o, qb_o, kb_o, vb_o, zb_o, gb_o, h_scr, *, n_lat_tiles, n_tiles, secs, halo,
                    n_gdn_heads):
    step_id = pl.program_id(0)
    wslot = step_id % 2
    rslot = 1 - wslot
    tm = x_ref.shape[1]
    last = pl.num_programs(0) - 2
    i = jnp.minimum(step_id, last) % n_tiles

    @pl.when(step_id == 0)
    def _():
        h_scr[...] = jnp.zeros_like(h_scr)

    def modulate_tile():
        mult = g_ref[...] * (1.0 + sc_ref[0])
        shift = sh_ref[0]

        def modulated(xf):
            rs = lax.rsqrt(jnp.mean(xf * xf, axis=-1, keepdims=True) + NORM_EPS)
            return xf * rs * mult + shift

        prev_ok = jnp.logical_and(i > 0, i < n_lat_tiles)
        next_ok = i < n_lat_tiles - 1
        h_scr[wslot, 0:halo, :] = jnp.where(prev_ok, modulated(xp_ref[0]), 0.0).astype(BF16)
        h_scr[wslot, halo + tm:2 * halo + tm, :] = jnp.where(next_ok, modulated(xn_ref[0]), 0.0).astype(BF16)
        src = jnp.where(i < n_lat_tiles, x_ref[0], ctx_ref[0])
        h_scr[wslot, halo:halo + tm, :] = modulated(src).astype(BF16)

    def proj(name):
        c0, c1 = secs[name]
        return lambda: jnp.dot(h_scr[rslot, halo:halo + tm, :], w_ref[:, c0:c1], preferred_element_type=F32)

    pad = SHORT_CONV // 2
    pw = 2 * A_DK
    hpb = pw // A_DK

    def qkv_mm(pb):
        c0 = secs["qkv"][0] + pb * pw
        return lambda: jnp.dot(h_scr[rslot], w_ref[:, c0:c0 + pw], preferred_element_type=F32)

    def qkv_epi(pb):
        def epi(y):
            for sb in range(hpb):
                hb = pb * hpb + sb
                cl = slice(hb * A_DK, (hb + 1) * A_DK)
                acc = None
                for j in range(SHORT_CONV):
                    term = y[halo - pad + j:halo - pad + j + tm, sb * A_DK:(sb + 1) * A_DK] * cw_ref[j:j + 1, cl]
                    acc = term if acc is None else acc + term
                u = _silu(acc)
                if hb < 2 * n_gdn_heads:
                    u = u * lax.rsqrt(jnp.sum(u * u, axis=-1, keepdims=True) + NORM_EPS)
                    if hb < n_gdn_heads:
                        u = u * (A_DK ** -0.5)
                qkv_o[0, :, cl] = u.astype(BF16)
        return epi

    def gate_epi(o_ref):
        def epi(y):
            o_ref[0] = _silu(y).astype(BF16)
        return epi

    def plain_epi(y):
        vb_o[0] = y.astype(BF16)

    def norm_rope_epi(o_ref, g_ref_, out_scale):
        def epi(y):
            cs = cs_ref[...]
            sn = sn_ref[...]
            g = g_ref_[...]
            for hh in range(y.shape[1] // B_HD):
                sl = slice(hh * B_HD, (hh + 1) * B_HD)
                yh = y[:, sl]
                ms = jnp.mean(yh * yh, axis=-1, keepdims=True)
                yn = yh * lax.rsqrt(ms + NORM_EPS) * g
                o_ref[0, :, sl] = ((yn * cs + pltpu.roll(yn, B_HD // 2, 1) * sn) * out_scale).astype(BF16)
        return epi

    def decay_epi(ab):
        nab = gb_o.shape[2]
        gval = -jnp.exp(alog_ref[...]) * _softplus(ab + dtb_ref[...])
        bval = _sigmoid(ab)
        lane = lax.broadcasted_iota(jnp.int32, ab.shape, 1)
        gb_o[0] = jnp.where(lane < nab // 2, gval, bval)[:, :nab]

    n_qkv_blocks = (secs["qkv"][1] - secs["qkv"][0]) // pw
    heavy = [(qkv_mm(pb), qkv_epi(pb)) for pb in range(n_qkv_blocks)]
    light = [(proj("za"), gate_epi(za_o)), (proj("zb"), gate_epi(zb_o)), (proj("vb"), plain_epi),
             (proj("qb"), norm_rope_epi(qb_o, qg_ref, B_HD ** -0.5)), (proj("kb"), norm_rope_epi(kb_o, kg_ref, 1.0)),
             (proj("ab"), decay_epi)]
    stages = []
    for k in range(max(len(heavy), len(light))):
        stages += heavy[k:k + 1] + light[k:k + 1]
    y_next = stages[0][0]()
    modulate_tile()
    for k, (_, epi) in enumerate(stages):
        y = y_next
        if k + 1 < len(stages):
            y_next = stages[k + 1][0]()
        epi(y)


def _inproj0(x, ctx, mod0, pre_g, w2, secs, conv_w, cs_tab, sn_tab, qg, kg, alog, dtb, tm, n_gdn_heads):
    bsz, t, d = x.shape
    nctx = ctx.shape[1]
    tall = t + nctx
    assert t % tm == 0 and nctx == tm
    n_lat = t // tm
    ntile = n_lat + 1
    nab = 16
    halo = 16
    nhb = tm // halo
    last_hblk = t // halo - 1

    def widths(name):
        return secs[name][1] - secs[name][0]

    n_steps = bsz * ntile

    def cur(s):
        p = jnp.minimum(s, n_steps - 1)
        return p // ntile, p % ntile

    def prev(s):
        p = jnp.maximum(s - 1, 0)
        return p // ntile, p % ntile

    def row_spec(c):
        return pl.BlockSpec((1, tm, c), lambda s: (prev(s)[0], prev(s)[1], 0))

    def mod_row(s):
        b, i = cur(s)
        return jnp.where(i < n_lat, b, ctx_row)

    mod_rows = mod0.shape[0]
    mod3 = mod0.reshape(mod_rows, 1, 3 * d)
    ctx_row = bsz
    outs = [
        jax.ShapeDtypeStruct((bsz, tall, widths("qkv")), BF16),
        jax.ShapeDtypeStruct((bsz, tall, widths("za")), BF16),
        jax.ShapeDtypeStruct((bsz, tall, widths("qb")), BF16),
        jax.ShapeDtypeStruct((bsz, tall, widths("kb")), BF16),
        jax.ShapeDtypeStruct((bsz, tall, widths("vb")), BF16),
        jax.ShapeDtypeStruct((bsz, tall, widths("zb")), BF16),
        jax.ShapeDtypeStruct((bsz, tall, nab), F32),
    ]
    kern = functools.partial(_inproj0_kernel, n_lat_tiles=n_lat, n_tiles=ntile, secs=secs, halo=halo,
                             n_gdn_heads=n_gdn_heads)
    return pl.pallas_call(
        kern,
        out_shape=outs,
        grid=(n_steps + 1,),
        in_specs=[
            pl.BlockSpec((1, tm, d), lambda s: (cur(s)[0], jnp.minimum(cur(s)[1], n_lat - 1), 0)),
            pl.BlockSpec((1, halo, d), lambda s: (cur(s)[0], jnp.clip(cur(s)[1] * nhb - 1, 0, last_hblk), 0)),
            pl.BlockSpec((1, halo, d), lambda s: (cur(s)[0], jnp.clip((cur(s)[1] + 1) * nhb, 0, last_hblk), 0)),
            pl.BlockSpec((1, tm, d), lambda s: (cur(s)[0], 0, 0)),
            pl.BlockSpec((1, 1, d), lambda s: (mod_row(s), 0, 0)),
            pl.BlockSpec((1, 1, d), lambda s: (mod_row(s), 0, 1)),
            pl.BlockSpec((1, d), lambda s: (0, 0)),
            pl.BlockSpec(w2.shape, lambda s: (0, 0)),
            pl.BlockSpec(conv_w.shape, lambda s: (0, 0)),
            pl.BlockSpec((tm, B_HD), lambda s: (prev(s)[1], 0)),
            pl.BlockSpec((tm, B_HD), lambda s: (prev(s)[1], 0)),
            pl.BlockSpec((1, B_HD), lambda s: (0, 0)),
            pl.BlockSpec((1, B_HD), lambda s: (0, 0)),
            pl.BlockSpec((1, 128), lambda s: (0, 0)),
            pl.BlockSpec((1, 128), lambda s: (0, 0)),
        ],
        out_specs=[row_spec(widths("qkv")), row_spec(widths("za")), row_spec(widths("qb")), row_spec(widths("kb")),
                   row_spec(widths("vb")), row_spec(widths("zb")), row_spec(nab)],
        scratch_shapes=[pltpu.VMEM((2, tm + 2 * halo, d), BF16)],
        compiler_params=_cparams(("arbitrary",)),
        name="inproj0",
    )(x, x, x, ctx, mod3, mod3, pre_g.reshape(1, d), w2, conv_w, cs_tab, sn_tab, qg, kg, alog, dtb)


def _gdn_kernel(q_ref, k_ref, v_ref, gr_ref, sz_ref, ng_ref, o_ref,
                nq_scr, z_scr, au_scr, gt_scr, cr_scr, s_scr, oacc, *, n_lat_chunks, n_chunks, group):
    C = GDN_CHUNK
    step_id = pl.program_id(0)
    wset = step_id % 2
    rset = 1 - wset
    ri = lax.broadcasted_iota(jnp.int32, (C, C), 0)
    ci = lax.broadcasted_iota(jnp.int32, (C, C), 1)
    incl = (ri >= ci, ri <= ci)
    strict = (ri > ci, ri < ci)
    eye = (ri == ci).astype(F32)
    tri_row = ((ri <= ci).astype(F32), (ri >= ci).astype(F32))
    tri_col = (incl[0].astype(F32), incl[1].astype(F32))

    @pl.when(step_id == 0)
    def _():
        nq_scr[...] = jnp.zeros_like(nq_scr)
        z_scr[...] = jnp.zeros_like(z_scr)
        au_scr[...] = jnp.zeros_like(au_scr)
        gt_scr[...] = jnp.zeros_like(gt_scr)

    for d in range(2):
        cr_scr[d] = jnp.dot(gr_ref[0, 0, d], tri_row[d], preferred_element_type=F32, precision=HIGHEST)
    s_scr[...] = jnp.zeros_like(s_scr)
    oacc[...] = jnp.zeros_like(oacc)

    nt = (((1,), (1,)), ((), ()))
    nsq = int(math.log2(INV_BLOCK)) - 1
    diag_blk = (ri // INV_BLOCK) == (ci // INV_BLOCK)
    off_blks = []
    bs = INV_BLOCK
    while bs < C:
        off_blks.append(jnp.logical_and((ri // (2 * bs)) == (ci // (2 * bs)), (ri // bs) != (ci // bs)))
        bs *= 2

    def mm(a, b):
        return jnp.dot(a, b, preferred_element_type=F32)

    tl = (((0,), (0,)), ((), ()))

    def prep(chunk_ids):
        ch = []
        for n in chunk_ids:
            rows = pl.ds(pl.multiple_of(n * C, C), C)
            kb = k_ref[0, rows, :]
            qb = q_ref[0, rows, :]
            k32 = kb.astype(F32)
            q32 = qb.astype(F32)
            v32 = v_ref[0, rows, :].astype(F32)
            kk = lax.dot_general(kb, kb, nt, preferred_element_type=F32)
            qk = lax.dot_general(qb, kb, nt, preferred_element_type=F32)
            for d in range(2):
                g_row = gr_ref[0, 0, d, pl.ds(n, 1), :]
                beta_row = gr_ref[0, 0, 2 + d, pl.ds(n, 1), :]
                c_row = cr_scr[d, pl.ds(n, 1), :]
                c_col = jnp.sum(tri_col[d] * g_row, axis=-1, keepdims=True)
                beta_col = jnp.sum(eye * beta_row, axis=-1, keepdims=True)
                tot = jnp.sum(g_row, axis=-1, keepdims=True)
                decay = jnp.exp(jnp.where(incl[d], c_col - c_row, NEG_BIG))
                x = jnp.where(strict[d], kk * decay, 0.0) * (-beta_col)
                e_col = jnp.exp(c_col)
                rhs = jnp.concatenate([v32 * beta_col, k32 * (beta_col * e_col)], axis=1)
                qd = q32 * e_col
                kd = (k32 * jnp.exp(tot - c_col)).astype(BF16)
                am = jnp.where(incl[d], qk * decay, 0.0).astype(BF16)
                gt_scr[2 * wset + d, n] = jnp.broadcast_to(jnp.exp(tot), (8, 128))
                ch.append((d, n, rows, x, rhs, qd, kd, am))
        yield
        xs = [c[3] for c in ch]
        ys = [jnp.where(diag_blk, x, 0.0) for x in xs]
        xbs = [y.astype(BF16) for y in ys]
        ps = [mm(xb, xb) for xb in xbs]
        yield
        for m in range(nsq):
            pbs = [p.astype(BF16) for p in ps]
            ybs = [y.astype(BF16) for y in ys]
            if m < nsq - 1:
                rs = [mm(jnp.concatenate([pb, yb], axis=0), pb) for pb, yb in zip(pbs, ybs)]
                ys = [y + p + r[C:] for y, p, r in zip(ys, ps, rs)]
                ps = [r[:C] for r in rs]
            else:
                ys = [y + p + mm(yb, pb) for y, p, yb, pb in zip(ys, ps, ybs, pbs)]
            yield
        for off in off_blks:
            xos = [jnp.where(off, x, 0.0) for x in xs]
            ybs = [y.astype(BF16) for y in ys]
            ts = [xo + mm(yb, xo.astype(BF16)) for xo, yb in zip(xos, ybs)]
            yield
            ys = [y + t + mm(t.astype(BF16), yb) for y, t, yb in zip(ys, ts, ybs)]
            yield
        ybs = [y.astype(BF16) for y in ys]
        uws = [c[4] + mm(yb, c[4].astype(BF16)) for c, yb in zip(ch, ybs)]
        uwbs = [uw.astype(BF16) for uw in uws]
        yield
        kzs = [lax.dot_general(c[6], uwb, tl, preferred_element_type=F32) for c, uwb in zip(ch, uwbs)]
        azs = [mm(c[7], uwb) for c, uwb in zip(ch, uwbs)]
        for c, kz, az in zip(ch, kzs, azs):
            cd, n, rows = 2 * wset + c[0], c[1], c[2]
            z_scr[cd, n] = kz[:, :A_DV].astype(z_scr.dtype)
            nq_scr[cd, n, 0:A_DK, :] = kz[:, A_DV:].astype(BF16)
            nq_scr[cd, n, A_DK:A_DK + C, :] = (c[5] - az[:, A_DV:]).astype(BF16)
            au_scr[cd, rows, :] = az[:, :A_DV].astype(au_scr.dtype)
        yield

    def scan_step(ns, with_out):
        nrow = A_DK + C if with_out else A_DK
        ss = [s_scr[d] for d in range(2)]
        rs = [mm(nq_scr[2 * rset + d, ns[d], 0:nrow, :], ss[d].astype(BF16)) for d in range(2)]
        for d in range(2):
            cd, n = 2 * rset + d, ns[d]
            s_scr[d] = ss[d] * gt_scr[cd, n][0:1, :] - rs[d][:A_DK] + z_scr[cd, n].astype(F32)
            if with_out:
                rows = pl.ds(pl.multiple_of(n * C, C), C)
                o = oacc[rows, :] + rs[d][A_DK:] + au_scr[cd, rows, :].astype(F32)
                oacc[rows, :] = o
                on = o * lax.rsqrt(jnp.mean(o * o, axis=-1, keepdims=True) + NORM_EPS) * ng_ref[...]
                o_ref[0, rows, :] = (on * sz_ref[0, rows, :].astype(F32)).astype(BF16)

    def interleave(stages, scan_steps):
        pending = list(scan_steps)
        for _ in stages:
            if pending:
                pending.pop(0)()
        for rest in pending:
            rest()

    n_ctx_chunks = n_chunks - n_lat_chunks
    interleave(prep([n_lat_chunks + i for i in range(n_ctx_chunks)]),
               [functools.partial(scan_step, (n_lat_chunks + j, n_chunks - 1 - j), False)
                for j in range(n_ctx_chunks)])

    def body(gi, carry):
        base = gi * group
        interleave(prep([base + i for i in range(group)]),
                   [functools.partial(scan_step, (base + j, n_lat_chunks - 1 - base - j), True)
                    for j in range(group)])
        return carry

    lax.fori_loop(0, n_lat_chunks // group, body, 0)


def _gdn(qkvn, grow, sza, norm_g, n_heads, t):
    bsz, tall, _ = qkvn.shape
    C = GDN_CHUNK
    n_chunks = tall // C
    n_lat_chunks = t // C
    ncp = grow.shape[3]
    group = 8
    assert n_lat_chunks % group == 0
    kern = functools.partial(_gdn_kernel, n_lat_chunks=n_lat_chunks, n_chunks=n_chunks, group=group)
    n_pairs = bsz * n_heads

    def cur(s):
        p = jnp.minimum(s, n_pairs - 1)
        return p // n_heads, p % n_heads

    def prev(s):
        p = jnp.maximum(s - 1, 0)
        return p // n_heads, p % n_heads

    return pl.pallas_call(
        kern,
        out_shape=jax.ShapeDtypeStruct((bsz, t, n_heads * A_DV), BF16),
        grid=(n_pairs + 1,),
        in_specs=[
            pl.BlockSpec((1, tall, A_DK), lambda s: (cur(s)[0], 0, cur(s)[1])),
            pl.BlockSpec((1, tall, A_DK), lambda s: (cur(s)[0], 0, n_heads + cur(s)[1])),
            pl.BlockSpec((1, tall, A_DV), lambda s: (cur(s)[0], 0, 2 * n_heads + cur(s)[1])),
            pl.BlockSpec((1, 1, 4, ncp, C), lambda s: (cur(s)[0], cur(s)[1], 0, 0, 0)),
            pl.BlockSpec((1, t, A_DV), lambda s: (prev(s)[0], 0, prev(s)[1])),
            pl.BlockSpec((1, A_DV), lambda s: (0, 0)),
        ],
        out_specs=pl.BlockSpec((1, t, A_DV), lambda s: (prev(s)[0], 0, prev(s)[1])),
        scratch_shapes=[
            pltpu.VMEM((4, n_chunks, A_DK + C, A_DK), BF16),
            pltpu.VMEM((4, n_chunks, A_DK, A_DV), BF16),
            pltpu.VMEM((4, tall, A_DV), BF16),
            pltpu.VMEM((4, n_chunks, 8, 128), F32),
            pltpu.VMEM((2, ncp, C), F32),
            pltpu.VMEM((2, A_DK, A_DV), F32),
            pltpu.VMEM((t, A_DV), F32),
        ],
        compiler_params=_cparams(("arbitrary",)),
        name="gdn",
    )(qkvn, qkvn, qkvn, grow, sza, norm_g)


def _attn_kernel(q_ref, k_ref, v_ref, sz_ref, o_ref, *, kv_blocks):
    nt = (((1,), (1,)), ((), ()))
    n_heads = q_ref.shape[2] // B_HD
    items = [(hh, blk) for hh in range(n_heads) for blk in kv_blocks]

    def scores(item):
        hh, (k0, k1) = item
        q = q_ref[0, :, hh * B_HD:(hh + 1) * B_HD]
        return lax.dot_general(q, k_ref[0, k0:k1, :], nt, preferred_element_type=F32)

    s_next = scores(items[0])
    m = l = acc = None
    for idx, (hh, (k0, k1)) in enumerate(items):
        s = s_next
        if idx + 1 < len(items):
            s_next = scores(items[idx + 1])
        bm = jnp.max(s, axis=-1, keepdims=True)
        if k0 == kv_blocks[0][0]:
            m = bm
            p = jnp.exp(s - m)
            l = jnp.sum(p, axis=-1, keepdims=True)
            acc = jnp.dot(p.astype(BF16), v_ref[0, k0:k1, :], preferred_element_type=F32)
        else:
            m_new = jnp.maximum(m, bm)
            alpha = jnp.exp(m - m_new)
            p = jnp.exp(s - m_new)
            l = alpha * l + jnp.sum(p, axis=-1, keepdims=True)
            acc = alpha * acc + jnp.dot(p.astype(BF16), v_ref[0, k0:k1, :], preferred_element_type=F32)
            m = m_new
        if k1 == kv_blocks[-1][1]:
            sl = slice(hh * B_HD, (hh + 1) * B_HD)
            o_ref[0, :, sl] = (acc * (1.0 / l) * sz_ref[0, :, sl].astype(F32)).astype(BF16)


def _attn(qb, kb, vb, szb, t, tq):
    bsz, tall, qw = qb.shape
    n_kv = kb.shape[2] // B_HD
    gw = qw // n_kv
    kvb = 1024
    edges = list(range(0, tall, kvb)) + [tall]
    kv_blocks = tuple((edges[j], edges[j + 1]) for j in range(len(edges) - 1))
    return pl.pallas_call(
        functools.partial(_attn_kernel, kv_blocks=kv_blocks),
        out_shape=jax.ShapeDtypeStruct((bsz, t, qw), BF16),
        grid=(bsz, n_kv, t // tq),
        in_specs=[
            pl.BlockSpec((1, tq, gw), lambda b, g, i: (b, i, g)),
            pl.BlockSpec((1, tall, B_HD), lambda b, g, i: (b, 0, g)),
            pl.BlockSpec((1, tall, B_HD), lambda b, g, i: (b, 0, g)),
            pl.BlockSpec((1, tq, gw), lambda b, g, i: (b, i, g)),
        ],
        out_specs=pl.BlockSpec((1, tq, gw), lambda b, g, i: (b, i, g)),
        compiler_params=_cparams(("parallel", "parallel", "arbitrary")),
        name="attn",
    )(qb, kb, vb, szb)


def _conformer_kernel(ya_ref, yb_ref, x_ref, wa_ref, wb_ref, pg0_ref, gate0_ref, sh_ref, sc_ref, g1_ref, w1_ref,
                      b1_ref, dw_ref, dwb_ref, lng_ref, lnb_ref, wo_ref, bo_ref, pg1_ref, gate1_ref, zero_ref,
                      o_ref, h_scr, u_ring, sz_ring, xl_ring, buf, act, *, tiles_per_seq, halo):
    step_id = pl.program_id(0)
    d = x_ref.shape[2]
    tm = x_ref.shape[1]
    nsub = 2
    ts = tm // nsub
    subs = [slice(k * ts, (k + 1) * ts) for k in range(nsub)]
    u_cur, u_prev, u_pprev = step_id % 3, (step_id + 2) % 3, (step_id + 1) % 3
    r_cur, r_prev = step_id % 2, (step_id + 1) % 2
    ip = jnp.maximum(step_id - 1, 0) % tiles_per_seq

    @pl.when(step_id == 0)
    def _():
        u_ring[...] = jnp.zeros_like(u_ring)
        sz_ring[...] = jnp.zeros_like(sz_ring)
        xl_ring[...] = jnp.zeros_like(xl_ring)

    pad = CONV_K // 2
    cw = 128
    sub = 8
    buf[0:halo, :] = jnp.where(ip > 0, u_ring[u_pprev, tm - halo:tm, :].astype(F32), 0.0)
    buf[halo:halo + tm, :] = u_ring[u_prev].astype(F32)

    def conv_piece(r0, nr, cb):
        cs = slice(cb * cw, (cb + 1) * cw)
        acc = None
        for s in range(sub):
            part = None
            for a in range((halo + pad) // sub + 1):
                j = sub * a + s - (halo - pad)
                if 0 <= j < CONV_K:
                    term = buf[r0 + sub * a:r0 + sub * a + nr + sub, cs] * dw_ref[j:j + 1, cs]
                    part = term if part is None else part + term
            if part is not None:
                acc = part[s:s + nr] if acc is None else acc + part[s:s + nr]
        act[r0:r0 + nr, cs] = acc + dwb_ref[:, cs]

    def activate(r):
        u = act[r, :]
        mu = jnp.mean(u, axis=-1, keepdims=True)
        uc = u - mu
        var = jnp.mean(uc * uc, axis=-1, keepdims=True)
        un = uc * lax.rsqrt(var + LN_EPS) * lng_ref[...] + lnb_ref[...]
        hact = (_silu(un) * sz_ring[r_prev, r, :].astype(F32)).astype(BF16)
        return jnp.dot(hact, wo_ref[...], preferred_element_type=F32) + bo_ref[...]

    def finish(r, out):
        on = out * lax.rsqrt(jnp.mean(out * out, axis=-1, keepdims=True) + NORM_EPS) * pg1_ref[...]
        o_ref[0, r, :] = xl_ring[r_prev, r, :] + gate1_ref[0] * on

    assert nsub == 2 and ts + 2 * halo <= tm + halo
    ncb = d // cw
    first_half = [functools.partial(conv_piece, 0, ts, cb) for cb in range(ncb)]
    second_half = [functools.partial(conv_piece, ts, ts, cb) for cb in range(ncb)]

    def run(pieces, n):
        for _ in range(min(n, len(pieces))):
            pieces.pop(0)()

    mult1 = g1_ref[...] * (1.0 + sc_ref[0])

    def out_proj(r):
        return (jnp.dot(ya_ref[0, r, :], wa_ref[...], preferred_element_type=F32)
                + jnp.dot(yb_ref[0, r, :], wb_ref[...], preferred_element_type=F32))

    def residual_and_modulate(r, out):
        on = out * lax.rsqrt(jnp.mean(out * out, axis=-1, keepdims=True) + NORM_EPS) * pg0_ref[...]
        xl = x_ref[0, r, :] + gate0_ref[0] * on
        xl_ring[r_cur, r, :] = xl
        rs = lax.rsqrt(jnp.mean(xl * xl, axis=-1, keepdims=True) + NORM_EPS)
        h_scr[r, :] = (xl * rs * mult1 + sh_ref[0]).astype(BF16)

    def in_mm(r, part):
        return (jnp.dot(h_scr[r, :], w1_ref[:, part * d:(part + 1) * d], preferred_element_type=F32)
                + b1_ref[:, part * d:(part + 1) * d])

    def glu_store(r, a, gl):
        u_ring[u_cur, r, :] = (a * _sigmoid(gl)).astype(BF16)

    def gate_store(r, z):
        sz_ring[r_cur, r, :] = _silu(z).astype(BF16)

    def tie(r, conv_r0, cb):
        dep = act[conv_r0:conv_r0 + ts, cb * cw:(cb + 1) * cw]
        h_scr[r, 0:cw] = h_scr[r, 0:cw] + (dep * zero_ref[...]).astype(BF16)

    s0, s1 = subs
    o0 = out_proj(s0)
    o1 = out_proj(s1)
    run(first_half, 2)
    residual_and_modulate(s0, o0)
    tie(s0, 0, 1)
    a0 = in_mm(s0, 0)
    run(first_half, 2)
    tie(s0, 0, 3)
    gl0 = in_mm(s0, 1)
    run(first_half, 2)
    tie(s0, 0, 5)
    z0 = in_mm(s0, 2)
    glu_store(s0, a0, gl0)
    gate_store(s0, z0)
    residual_and_modulate(s1, o1)
    run(first_half, ncb)
    tie(s1, 0, ncb - 1)
    a1 = in_mm(s1, 0)
    buf[halo + tm:2 * halo + tm, :] = jnp.where(ip < tiles_per_seq - 1, u_ring[u_cur, 0:halo, :].astype(F32), 0.0)
    run(second_half, 3)
    tie(s1, ts, 2)
    gl1 = in_mm(s1, 1)
    run(second_half, 3)
    tie(s1, ts, 5)
    z1 = in_mm(s1, 2)
    glu_store(s1, a1, gl1)
    gate_store(s1, z1)
    run(second_half, ncb)
    out0 = activate(s0)
    out1 = activate(s1)
    finish(s0, out0)
    finish(s1, out1)


def _conformer(ya, yb, x, wa, wb, post_g0, mod0, mod1, pre_g1, w1, b1, dw_w, dw_b, ln_g, ln_b, w_out, b_out,
               post_g1, tm):
    bsz, t, d = x.shape
    aw = ya.shape[2]
    bw = yb.shape[2]
    halo = 16
    assert halo >= CONV_K // 2 and t % tm == 0
    tiles = t // tm
    n_steps = bsz * tiles
    rows = mod0.shape[0]
    m0 = mod0.reshape(rows, 1, 3 * d)
    m1 = mod1.reshape(rows, 1, 3 * d)

    def cur(s):
        p = jnp.minimum(s, n_steps - 1)
        return p // tiles, p % tiles

    def prev(s):
        p = jnp.maximum(s - 1, 0)
        return p // tiles, p % tiles

    once = pl.Buffered(1)
    tok = lambda c: pl.BlockSpec((1, tm, c), lambda s: (cur(s)[0], cur(s)[1], 0))
    vec = lambda: pl.BlockSpec((1, d), lambda s: (0, 0))
    const = lambda a: pl.BlockSpec(a.shape, lambda s: (0, 0), pipeline_mode=once)
    kern = functools.partial(_conformer_kernel, tiles_per_seq=tiles, halo=halo)
    return pl.pallas_call(
        kern,
        out_shape=jax.ShapeDtypeStruct((bsz, t, d), F32),
        grid=(n_steps + 1,),
        in_specs=[
            tok(aw), tok(bw), tok(d),
            const(wa), const(wb),
            vec(),
            pl.BlockSpec((1, 1, d), lambda s: (cur(s)[0], 0, 2)),
            pl.BlockSpec((1, 1, d), lambda s: (cur(s)[0], 0, 0)),
            pl.BlockSpec((1, 1, d), lambda s: (cur(s)[0], 0, 1)),
            vec(),
            const(w1),
            pl.BlockSpec((1, 3 * d), lambda s: (0, 0)),
            pl.BlockSpec(dw_w.shape, lambda s: (0, 0)),
            vec(), vec(), vec(),
            const(w_out),
            vec(), vec(),
            pl.BlockSpec((1, 1, d), lambda s: (prev(s)[0], 0, 2)),
            pl.BlockSpec((1, 128), lambda s: (0, 0)),
        ],
        out_specs=pl.BlockSpec((1, tm, d), lambda s: (prev(s)[0], prev(s)[1], 0)),
        scratch_shapes=[
            pltpu.VMEM((tm, d), BF16),
            pltpu.VMEM((3, tm, d), BF16),
            pltpu.VMEM((2, tm, d), BF16),
            pltpu.VMEM((2, tm, d), F32),
            pltpu.VMEM((tm + 2 * halo, d), F32),
            pltpu.VMEM((tm, d), F32),
        ],
        compiler_params=_cparams(("arbitrary",)),
        name="conformer",
    )(ya, yb, x, wa, wb, post_g0.reshape(1, d), m0, m1, m1, pre_g1.reshape(1, d), w1, b1.reshape(1, 3 * d),
      dw_w, dw_b.reshape(1, d), ln_g.reshape(1, d), ln_b.reshape(1, d), w_out, b_out.reshape(1, d),
      post_g1.reshape(1, d), m1, jnp.zeros((1, 128), F32))


def _rope_tables(t, nctx):
    f32 = np.float32
    rows = t // GRID_W
    row = np.repeat(np.arange(rows, dtype=f32), GRID_W)
    col = np.tile(np.arange(GRID_W, dtype=f32), rows)
    axis_dim = B_HD // 2
    inv_freq = (f32(ROPE_THETA) ** (-np.arange(0, axis_dim, 2, dtype=f32) / f32(axis_dim))).astype(f32)
    ang = np.concatenate([row[:, None] * inv_freq, col[:, None] * inv_freq], axis=1).astype(f32)
    cs = np.concatenate([np.cos(ang), np.cos(ang)], axis=1)
    sn = np.concatenate([-np.sin(ang), np.sin(ang)], axis=1)
    cs = np.concatenate([cs, np.ones((nctx, B_HD), f32)], axis=0).astype(f32)
    sn = np.concatenate([sn, np.zeros((nctx, B_HD), f32)], axis=0).astype(f32)
    return jnp.asarray(cs), jnp.asarray(sn)


def _head_perm():
    q = B_HD // 4
    return np.concatenate([np.arange(0, q), np.arange(2 * q, 3 * q), np.arange(q, 2 * q), np.arange(3 * q, 4 * q)])


def kernel(x, c, ctx, c_ctx, ada_w, ada_b, pre_norm_g, post_norm_g, ev_w_in, ev_short_conv_w, ev_a_log,
           ev_dt_bias, ev_gdn_norm_g, ev_q_norm_g, ev_k_norm_g, ev_w_out, od_w_in, od_b_in, od_dw_w, od_dw_b,
           od_ln_g, od_ln_b, od_w_out, od_b_out):
    bsz, t, d = x.shape
    nctx = ctx.shape[1]
    a_width = d // 2
    a_heads = a_width // A_DV
    a_qkv = a_heads * (2 * A_DK + A_DV)
    b_width = d - a_width
    b_heads = b_width // B_HD
    b_kv = b_heads // 2
    assert ada_w.shape[0] == 2 and ev_w_in.shape[0] == 1 and od_w_in.shape[0] == 1

    mod_rows = ((bsz + 1 + 7) // 8) * 8
    cond = jnp.zeros((mod_rows, d), F32).at[:bsz].set(c).at[bsz].set(c_ctx)
    mod = _ada(cond, ada_w, ada_b)

    w_in = ev_w_in[0]
    splits = np.cumsum([0, a_qkv, a_width, 2 * a_heads, 2 * a_heads, b_heads * B_HD, b_kv * B_HD, b_kv * B_HD,
                        b_width])
    w_qkv, w_za, w_a, w_b, w_qb, w_kb, w_vb, w_zb = [w_in[:, splits[j]:splits[j + 1]] for j in range(8)]
    perm = _head_perm()
    perm_q = np.concatenate([h * B_HD + perm for h in range(b_heads)])
    perm_k = np.concatenate([h * B_HD + perm for h in range(b_kv)])
    ab_pad = 128 - 4 * a_heads
    pieces = [("qkv", w_qkv), ("za", w_za), ("qb", w_qb[:, perm_q]), ("kb", w_kb[:, perm_k]), ("vb", w_vb),
              ("zb", w_zb), ("ab", jnp.concatenate([w_a, w_b, jnp.zeros((d, ab_pad), F32)], axis=1))]
    secs, off = {}, 0
    for name, wpart in pieces:
        secs[name] = (off, off + wpart.shape[1])
        off += wpart.shape[1]
    w2 = jnp.concatenate([p[1] for p in pieces], axis=1).astype(BF16)
    cs_tab, sn_tab = _rope_tables(t, nctx)
    qg = ev_q_norm_g[0][perm].reshape(1, B_HD)
    kg = ev_k_norm_g[0][perm].reshape(1, B_HD)
    alog = jnp.zeros((1, 128), F32).at[0, :2 * a_heads].set(ev_a_log[0].reshape(-1))
    dtb = jnp.zeros((1, 128), F32).at[0, :2 * a_heads].set(ev_dt_bias[0].reshape(-1))

    tm0 = nctx
    qkvn, sza, qb, kb, vb, szb, gb = _inproj0(x, ctx, mod[0], pre_norm_g[0], w2, secs, ev_short_conv_w[0], cs_tab,
                                              sn_tab, qg, kg, alog, dtb, tm0, a_heads)

    tall = t + nctx
    n_chunks = tall // GDN_CHUNK
    ncp = ((n_chunks + 7) // 8) * 8
    grow = gb.reshape(bsz, n_chunks, GDN_CHUNK, 4, a_heads).transpose(0, 4, 3, 1, 2)
    grow = jnp.pad(grow, ((0, 0), (0, 0), (0, 0), (0, ncp - n_chunks), (0, 0)))
    ya = _gdn(qkvn, grow, sza, ev_gdn_norm_g[0].reshape(1, A_DV), a_heads, t)

    yb = _attn(qb, kb, vb, szb, t, 512)

    w_out0 = ev_w_out[0].astype(BF16)
    return _conformer(ya, yb, x, w_out0[:a_width], w_out0[a_width:], post_norm_g[0], mod[0], mod[1], pre_norm_g[1],
                      od_w_in[0].astype(BF16), od_b_in[0], od_dw_w[0], od_dw_b[0], od_ln_g[0], od_ln_b[0],
                      od_w_out[0].astype(BF16), od_b_out[0], post_norm_g[1], 512)
```

```python
import functools
import math

import numpy as np
import jax
import jax.numpy as jnp
from jax import lax
from jax.experimental import pallas as pl
from jax.experimental.pallas import tpu as pltpu

F32 = jnp.float32
BF16 = jnp.bfloat16
HIGHEST = lax.Precision.HIGHEST

GRID_W = 64
A_DK = 128
A_DV = 128
SHORT_CONV = 5
GDN_CHUNK = 64
INV_BLOCK = 16
B_HD = 128
ROPE_THETA = 10000.0
CONV_K = 31
NORM_EPS = 1e-6
LN_EPS = 1e-5
NEG_BIG = -1e30

VMEM_LIMIT = 56 * 1024 * 1024


def _sigmoid(x):
    return 1.0 / (1.0 + jnp.exp(-x))


def _silu(x):
    return x * _sigmoid(x)


def _softplus(x):
    return jnp.maximum(x, 0.0) + jnp.log(1.0 + jnp.exp(-jnp.abs(x)))


def _cparams(sem):
    return pltpu.CompilerParams(dimension_semantics=sem, vmem_limit_bytes=VMEM_LIMIT)


def _ada_kernel(c_ref, w_ref, b_ref, o_ref):
    s = _silu(c_ref[...])
    o_ref[0] = jnp.dot(s, w_ref[0], preferred_element_type=F32, precision=HIGHEST) + b_ref[0]


def _ada(cond, ada_w, ada_b):
    depth, d, d3 = ada_w.shape
    rows = cond.shape[0]
    nt = d3 // d
    return pl.pallas_call(
        _ada_kernel,
        out_shape=jax.ShapeDtypeStruct((depth, rows, d3), F32),
        grid=(depth, nt),
        in_specs=[
            pl.BlockSpec((rows, d), lambda l, j: (0, 0)),
            pl.BlockSpec((1, d, d), lambda l, j: (l, 0, j)),
            pl.BlockSpec((1, 1, d), lambda l, j: (l, 0, j)),
        ],
        out_specs=pl.BlockSpec((1, rows, d), lambda l, j: (l, 0, j)),
        compiler_params=_cparams(("parallel", "parallel")),
        name="ada",
    )(cond, ada_w, ada_b.reshape(depth, 1, d3))


def _inproj0_kernel(x_ref, xp_ref, xn_ref, ctx_ref, sh_ref, sc_ref, g_ref, w_ref, cw_ref, cs_ref, sn_ref, qg_ref,
                    kg_ref, alog_ref, dtb_ref,
                    qkv_o, za_o, qb_o, kb_o, vb_o, zb_o, gb_o, h_scr, *, n_lat_tiles, n_tiles, secs, halo,
                    n_gdn_heads):
    step_id = pl.program_id(0)
    wslot = step_id % 2
    rslot = 1 - wslot
    tm = x_ref.shape[1]
    last = pl.num_programs(0) - 2
    i = jnp.minimum(step_id, last) % n_tiles

    @pl.when(step_id == 0)
    def _():
        h_scr[...] = jnp.zeros_like(h_scr)

    def modulate_tile():
        mult = g_ref[...] * (1.0 + sc_ref[0])
        shift = sh_ref[0]

        def modulated(xf):
            rs = lax.rsqrt(jnp.mean(xf * xf, axis=-1, keepdims=True) + NORM_EPS)
            return xf * rs * mult + shift

        prev_ok = jnp.logical_and(i > 0, i < n_lat_tiles)
        next_ok = i < n_lat_tiles - 1
        h_scr[wslot, 0:halo, :] = jnp.where(prev_ok, modulated(xp_ref[0]), 0.0).astype(BF16)
        h_scr[wslot, halo + tm:2 * halo + tm, :] = jnp.where(next_ok, modulated(xn_ref[0]), 0.0).astype(BF16)
        src = jnp.where(i < n_lat_tiles, x_ref[0], ctx_ref[0])
        h_scr[wslot, halo:halo + tm, :] = modulated(src).astype(BF16)

    def proj(name):
        c0, c1 = secs[name]
        return lambda: jnp.dot(h_scr[rslot, halo:halo + tm, :], w_ref[:, c0:c1], preferred_element_type=F32)

    pad = SHORT_CONV // 2
    pw = 2 * A_DK
    hpb = pw // A_DK

    def qkv_mm(pb):
        c0 = secs["qkv"][0] + pb * pw
        return lambda: jnp.dot(h_scr[rslot], w_ref[:, c0:c0 + pw], preferred_element_type=F32)

    def qkv_epi(pb):
        def epi(y):
            for sb in range(hpb):
                hb = pb * hpb + sb
                cl = slice(hb * A_DK, (hb + 1) * A_DK)
                acc = None
                for j in range(SHORT_CONV):
                    term = y[halo - pad + j:halo - pad + j + tm, sb * A_DK:(sb + 1) * A_DK] * cw_ref[j:j + 1, cl]
                    acc = term if acc is None else acc + term
                u = _silu(acc)
                if hb < 2 * n_gdn_heads:
                    u = u * lax.rsqrt(jnp.sum(u * u, axis=-1, keepdims=True) + NORM_EPS)
                    if hb < n_gdn_heads:
                        u = u * (A_DK ** -0.5)
                qkv_o[0, :, cl] = u.astype(BF16)
        return epi

    def gate_epi(o_ref):
        def epi(y):
            o_ref[0] = _silu(y).astype(BF16)
        return epi

    def plain_epi(y):
        vb_o[0] = y.astype(BF16)

    def norm_rope_epi(o_ref, g_ref_, out_scale):
        def epi(y):
            cs = cs_ref[...]
            sn = sn_ref[...]
            g = g_ref_[...]
            for hh in range(y.shape[1] // B_HD):
                sl = slice(hh * B_HD, (hh + 1) * B_HD)
                yh = y[:, sl]
                ms = jnp.mean(yh * yh, axis=-1, keepdims=True)
                yn = yh * lax.rsqrt(ms + NORM_EPS) * g
                o_ref[0, :, sl] = ((yn * cs + pltpu.roll(yn, B_HD // 2, 1) * sn) * out_scale).astype(BF16)
        return epi

    def decay_epi(ab):
        nab = gb_o.shape[2]
        gval = -jnp.exp(alog_ref[...]) * _softplus(ab + dtb_ref[...])
        bval = _sigmoid(ab)
        lane = lax.broadcasted_iota(jnp.int32, ab.shape, 1)
        gb_o[0] = jnp.where(lane < nab // 2, gval, bval)[:, :nab]

    n_qkv_blocks = (secs["qkv"][1] - secs["qkv"][0]) // pw
    heavy = [(qkv_mm(pb), qkv_epi(pb)) for pb in range(n_qkv_blocks)]
    light = [(proj("za"), gate_epi(za_o)), (proj("zb"), gate_epi(zb_o)), (proj("vb"), plain_epi),
             (proj("qb"), norm_rope_epi(qb_o, qg_ref, B_HD ** -0.5)), (proj("kb"), norm_rope_epi(kb_o, kg_ref, 1.0)),
             (proj("ab"), decay_epi)]
    stages = []
    for k in range(max(len(heavy), len(light))):
        stages += heavy[k:k + 1] + light[k:k + 1]
    y_next = stages[0][0]()
    modulate_tile()
    for k, (_, epi) in enumerate(stages):
        y = y_next
        if k + 1 < len(stages):
            y_next = stages[k + 1][0]()
        epi(y)


def _inproj0(x, ctx, mod0, pre_g, w2, secs, conv_w, cs_tab, sn_tab, qg, kg, alog, dtb, tm, n_gdn_heads):
    bsz, t, d = x.shape
    nctx = ctx.shape[1]
    tall = t + nctx
    assert t % tm == 0 and nctx == tm
    n_lat = t // tm
    ntile = n_lat + 1
    nab = 16
    halo = 16
    nhb = tm // halo
    last_hblk = t // halo - 1

    def widths(name):
        return secs[name][1] - secs[name][0]

    n_steps = bsz * ntile

    def cur(s):
        p = jnp.minimum(s, n_steps - 1)
        return p // ntile, p % ntile

    def prev(s):
        p = jnp.maximum(s - 1, 0)
        return p // ntile, p % ntile

    def row_spec(c):
        return pl.BlockSpec((1, tm, c), lambda s: (prev(s)[0], prev(s)[1], 0))

    def mod_row(s):
        b, i = cur(s)
        return jnp.where(i < n_lat, b, ctx_row)

    mod_rows = mod0.shape[0]
    mod3 = mod0.reshape(mod_rows, 1, 3 * d)
    ctx_row = bsz
    outs = [
        jax.ShapeDtypeStruct((bsz, tall, widths("qkv")), BF16),
        jax.ShapeDtypeStruct((bsz, tall, widths("za")), BF16),
        jax.ShapeDtypeStruct((bsz, tall, widths("qb")), BF16),
        jax.ShapeDtypeStruct((bsz, tall, widths("kb")), BF16),
        jax.ShapeDtypeStruct((bsz, tall, widths("vb")), BF16),
        jax.ShapeDtypeStruct((bsz, tall, widths("zb")), BF16),
        jax.ShapeDtypeStruct((bsz, tall, nab), F32),
    ]
    kern = functools.partial(_inproj0_kernel, n_lat_tiles=n_lat, n_tiles=ntile, secs=secs, halo=halo,
                             n_gdn_heads=n_gdn_heads)
    return pl.pallas_call(
        kern,
        out_shape=outs,
        grid=(n_steps + 1,),
        in_specs=[
            pl.BlockSpec((1, tm, d), lambda s: (cur(s)[0], jnp.minimum(cur(s)[1], n_lat - 1), 0)),
            pl.BlockSpec((1, halo, d), lambda s: (cur(s)[0], jnp.clip(cur(s)[1] * nhb - 1, 0, last_hblk), 0)),
            pl.BlockSpec((1, halo, d), lambda s: (cur(s)[0], jnp.clip((cur(s)[1] + 1) * nhb, 0, last_hblk), 0)),
            pl.BlockSpec((1, tm, d), lambda s: (cur(s)[0], 0, 0)),
            pl.BlockSpec((1, 1, d), lambda s: (mod_row(s), 0, 0)),
            pl.BlockSpec((1, 1, d), lambda s: (mod_row(s), 0, 1)),
            pl.BlockSpec((1, d), lambda s: (0, 0)),
            pl.BlockSpec(w2.shape, lambda s: (0, 0)),
            pl.BlockSpec(conv_w.shape, lambda s: (0, 0)),
            pl.BlockSpec((tm, B_HD), lambda s: (prev(s)[1], 0)),
            pl.BlockSpec((tm, B_HD), lambda s: (prev(s)[1], 0)),
            pl.BlockSpec((1, B_HD), lambda s: (0, 0)),
            pl.BlockSpec((1, B_HD), lambda s: (0, 0)),
            pl.BlockSpec((1, 128), lambda s: (0, 0)),
            pl.BlockSpec((1, 128), lambda s: (0, 0)),
        ],
        out_specs=[row_spec(widths("qkv")), row_spec(widths("za")), row_spec(widths("qb")), row_spec(widths("kb")),
                   row_spec(widths("vb")), row_spec(widths("zb")), row_spec(nab)],
        scratch_shapes=[pltpu.VMEM((2, tm + 2 * halo, d), BF16)],
        compiler_params=_cparams(("arbitrary",)),
        name="inproj0",
    )(x, x, x, ctx, mod3, mod3, pre_g.reshape(1, d), w2, conv_w, cs_tab, sn_tab, qg, kg, alog, dtb)


def _gdn_kernel(q_ref, k_ref, v_ref, gr_ref, sz_ref, ng_ref, o_ref,
                nq_scr, z_scr, au_scr, gt_scr, cr_scr, s_scr, oacc, *, n_lat_chunks, n_chunks, group):
    C = GDN_CHUNK
    step_id = pl.program_id(0)
    wset = step_id % 2
    rset = 1 - wset
    ri = lax.broadcasted_iota(jnp.int32, (C, C), 0)
    ci = lax.broadcasted_iota(jnp.int32, (C, C), 1)
    incl = (ri >= ci, ri <= ci)
    strict = (ri > ci, ri < ci)
    eye = (ri == ci).astype(F32)
    tri_row = ((ri <= ci).astype(F32), (ri >= ci).astype(F32))
    tri_col = (incl[0].astype(F32), incl[1].astype(F32))

    @pl.when(step_id == 0)
    def _():
        nq_scr[...] = jnp.zeros_like(nq_scr)
        z_scr[...] = jnp.zeros_like(z_scr)
        au_scr[...] = jnp.zeros_like(au_scr)
        gt_scr[...] = jnp.zeros_like(gt_scr)

    for d in range(2):
        cr_scr[d] = jnp.dot(gr_ref[0, 0, d], tri_row[d], preferred_element_type=F32, precision=HIGHEST)
    s_scr[...] = jnp.zeros_like(s_scr)
    oacc[...] = jnp.zeros_like(oacc)

    nt = (((1,), (1,)), ((), ()))
    nsq = int(math.log2(INV_BLOCK)) - 1
    diag_blk = (ri // INV_BLOCK) == (ci // INV_BLOCK)
    off_blks = []
    bs = INV_BLOCK
    while bs < C:
        off_blks.append(jnp.logical_and((ri // (2 * bs)) == (ci // (2 * bs)), (ri // bs) != (ci // bs)))
        bs *= 2

    def mm(a, b):
        return jnp.dot(a, b, preferred_element_type=F32)

    tl = (((0,), (0,)), ((), ()))

    def prep(chunk_ids):
        ch = []
        for n in chunk_ids:
            rows = pl.ds(pl.multiple_of(n * C, C), C)
            kb = k_ref[0, rows, :]
            qb = q_ref[0, rows, :]
            k32 = kb.astype(F32)
            q32 = qb.astype(F32)
            v32 = v_ref[0, rows, :].astype(F32)
            kk = lax.dot_general(kb, kb, nt, preferred_element_type=F32)
            qk = lax.dot_general(qb, kb, nt, preferred_element_type=F32)
            for d in range(2):
                g_row = gr_ref[0, 0, d, pl.ds(n, 1), :]
                beta_row = gr_ref[0, 0, 2 + d, pl.ds(n, 1), :]
                c_row = cr_scr[d, pl.ds(n, 1), :]
                c_col = jnp.sum(tri_col[d] * g_row, axis=-1, keepdims=True)
                beta_col = jnp.sum(eye * beta_row, axis=-1, keepdims=True)
                tot = jnp.sum(g_row, axis=-1, keepdims=True)
                decay = jnp.exp(jnp.where(incl[d], c_col - c_row, NEG_BIG))
                x = jnp.where(strict[d], kk * decay, 0.0) * (-beta_col)
                e_col = jnp.exp(c_col)
                rhs = jnp.concatenate([v32 * beta_col, k32 * (beta_col * e_col)], axis=1)
                qd = q32 * e_col
                kd = (k32 * jnp.exp(tot - c_col)).astype(BF16)
                am = jnp.where(incl[d], qk * decay, 0.0).astype(BF16)
                gt_scr[2 * wset + d, n] = jnp.broadcast_to(jnp.exp(tot), (8, 128))
                ch.append((d, n, rows, x, rhs, qd, kd, am))
        yield
        xs = [c[3] for c in ch]
        ys = [jnp.where(diag_blk, x, 0.0) for x in xs]
        xbs = [y.astype(BF16) for y in ys]
        ps = [mm(xb, xb) for xb in xbs]
        yield
        for m in range(nsq):
            pbs = [p.astype(BF16) for p in ps]
            ybs = [y.astype(BF16) for y in ys]
            if m < nsq - 1:
                rs = [mm(jnp.concatenate([pb, yb], axis=0), pb) for pb, yb in zip(pbs, ybs)]
                ys = [y + p + r[C:] for y, p, r in zip(ys, ps, rs)]
                ps = [r[:C] for r in rs]
            else:
                ys = [y + p + mm(yb, pb) for y, p, yb, pb in zip(ys, ps, ybs, pbs)]
            yield
        for off in off_blks:
            xos = [jnp.where(off, x, 0.0) for x in xs]
            ybs = [y.astype(BF16) for y in ys]
            ts = [xo + mm(yb, xo.astype(BF16)) for xo, yb in zip(xos, ybs)]
            yield
            ys = [y + t + mm(t.astype(BF16), yb) for y, t, yb in zip(ys, ts, ybs)]
            yield
        ybs = [y.astype(BF16) for y in ys]
        uws = [c[4] + mm(yb, c[4].astype(BF16)) for c, yb in zip(ch, ybs)]
        uwbs = [uw.astype(BF16) for uw in uws]
        yield
        kzs = [lax.dot_general(c[6], uwb, tl, preferred_element_type=F32) for c, uwb in zip(ch, uwbs)]
        azs = [mm(c[7], uwb) for c, uwb in zip(ch, uwbs)]
        for c, kz, az in zip(ch, kzs, azs):
            cd, n, rows = 2 * wset + c[0], c[1], c[2]
            z_scr[cd, n] = kz[:, :A_DV].astype(z_scr.dtype)
            nq_scr[cd, n, 0:A_DK, :] = kz[:, A_DV:].astype(BF16)
            nq_scr[cd, n, A_DK:A_DK + C, :] = (c[5] - az[:, A_DV:]).astype(BF16)
            au_scr[cd, rows, :] = az[:, :A_DV].astype(au_scr.dtype)
        yield

    def scan_step(ns, with_out):
        nrow = A_DK + C if with_out else A_DK
        ss = [s_scr[d] for d in range(2)]
        rs = [mm(nq_scr[2 * rset + d, ns[d], 0:nrow, :], ss[d].astype(BF16)) for d in range(2)]
        for d in range(2):
            cd, n = 2 * rset + d, ns[d]
            s_scr[d] = ss[d] * gt_scr[cd, n][0:1, :] - rs[d][:A_DK] + z_scr[cd, n].astype(F32)
            if with_out:
                rows = pl.ds(pl.multiple_of(n * C, C), C)
                o = oacc[rows, :] + rs[d][A_DK:] + au_scr[cd, rows, :].astype(F32)
                oacc[rows, :] = o
                on = o * lax.rsqrt(jnp.mean(o * o, axis=-1, keepdims=True) + NORM_EPS) * ng_ref[...]
                o_ref[0, rows, :] = (on * sz_ref[0, rows, :].astype(F32)).astype(BF16)

    def interleave(stages, scan_steps):
        pending = list(scan_steps)
        for _ in stages:
            if pending:
                pending.pop(0)()
        for rest in pending:
            rest()

    n_ctx_chunks = n_chunks - n_lat_chunks
    interleave(prep([n_lat_chunks + i for i in range(n_ctx_chunks)]),
               [functools.partial(scan_step, (n_lat_chunks + j, n_chunks - 1 - j), False)
                for j in range(n_ctx_chunks)])

    def body(gi, carry):
        base = gi * group
        interleave(prep([base + i for i in range(group)]),
                   [functools.partial(scan_step, (base + j, n_lat_chunks - 1 - base - j), True)
                    for j in range(group)])
        return carry

    lax.fori_loop(0, n_lat_chunks // group, body, 0)


def _gdn(qkvn, grow, sza, norm_g, n_heads, t):
    bsz, tall, _ = qkvn.shape
    C = GDN_CHUNK
    n_chunks = tall // C
    n_lat_chunks = t // C
    ncp = grow.shape[3]
    group = 8
    assert n_lat_chunks % group == 0
    kern = functools.partial(_gdn_kernel, n_lat_chunks=n_lat_chunks, n_chunks=n_chunks, group=group)
    n_pairs = bsz * n_heads

    def cur(s):
        p = jnp.minimum(s, n_pairs - 1)
        return p // n_heads, p % n_heads

    def prev(s):
        p = jnp.maximum(s - 1, 0)
        return p // n_heads, p % n_heads

    return pl.pallas_call(
        kern,
        out_shape=jax.ShapeDtypeStruct((bsz, t, n_heads * A_DV), BF16),
        grid=(n_pairs + 1,),
        in_specs=[
            pl.BlockSpec((1, tall, A_DK), lambda s: (cur(s)[0], 0, cur(s)[1])),
            pl.BlockSpec((1, tall, A_DK), lambda s: (cur(s)[0], 0, n_heads + cur(s)[1])),
            pl.BlockSpec((1, tall, A_DV), lambda s: (cur(s)[0], 0, 2 * n_heads + cur(s)[1])),
            pl.BlockSpec((1, 1, 4, ncp, C), lambda s: (cur(s)[0], cur(s)[1], 0, 0, 0)),
            pl.BlockSpec((1, t, A_DV), lambda s: (prev(s)[0], 0, prev(s)[1])),
            pl.BlockSpec((1, A_DV), lambda s: (0, 0)),
        ],
        out_specs=pl.BlockSpec((1, t, A_DV), lambda s: (prev(s)[0], 0, prev(s)[1])),
        scratch_shapes=[
            pltpu.VMEM((4, n_chunks, A_DK + C, A_DK), BF16),
            pltpu.VMEM((4, n_chunks, A_DK, A_DV), BF16),
            pltpu.VMEM((4, tall, A_DV), BF16),
            pltpu.VMEM((4, n_chunks, 8, 128), F32),
            pltpu.VMEM((2, ncp, C), F32),
            pltpu.VMEM((2, A_DK, A_DV), F32),
            pltpu.VMEM((t, A_DV), F32),
        ],
        compiler_params=_cparams(("arbitrary",)),
        name="gdn",
    )(qkvn, qkvn, qkvn, grow, sza, norm_g)


def _attn_kernel(q_ref, k_ref, v_ref, sz_ref, o_ref, *, kv_blocks):
    nt = (((1,), (1,)), ((), ()))
    n_heads = q_ref.shape[2] // B_HD
    items = [(hh, blk) for hh in range(n_heads) for blk in kv_blocks]

    def scores(item):
        hh, (k0, k1) = item
        q = q_ref[0, :, hh * B_HD:(hh + 1) * B_HD]
        return lax.dot_general(q, k_ref[0, k0:k1, :], nt, preferred_element_type=F32)

    s_next = scores(items[0])
    m = l = acc = None
    for idx, (hh, (k0, k1)) in enumerate(items):
        s = s_next
        if idx + 1 < len(items):
            s_next = scores(items[idx + 1])
        bm = jnp.max(s, axis=-1, keepdims=True)
        if k0 == kv_blocks[0][0]:
            m = bm
            p = jnp.exp(s - m)
            l = jnp.sum(p, axis=-1, keepdims=True)
            acc = jnp.dot(p.astype(BF16), v_ref[0, k0:k1, :], preferred_element_type=F32)
        else:
            m_new = jnp.maximum(m, bm)
            alpha = jnp.exp(m - m_new)
            p = jnp.exp(s - m_new)
            l = alpha * l + jnp.sum(p, axis=-1, keepdims=True)
            acc = alpha * acc + jnp.dot(p.astype(BF16), v_ref[0, k0:k1, :], preferred_element_type=F32)
            m = m_new
        if k1 == kv_blocks[-1][1]:
            sl = slice(hh * B_HD, (hh + 1) * B_HD)
            o_ref[0, :, sl] = (acc * (1.0 / l) * sz_ref[0, :, sl].astype(F32)).astype(BF16)


def _attn(qb, kb, vb, szb, t, tq):
    bsz, tall, qw = qb.shape
    n_kv = kb.shape[2] // B_HD
    gw = qw // n_kv
    kvb = 1024
    edges = list(range(0, tall, kvb)) + [tall]
    kv_blocks = tuple((edges[j], edges[j + 1]) for j in range(len(edges) - 1))
    return pl.pallas_call(
        functools.partial(_attn_kernel, kv_blocks=kv_blocks),
        out_shape=jax.ShapeDtypeStruct((bsz, t, qw), BF16),
        grid=(bsz, n_kv, t // tq),
        in_specs=[
            pl.BlockSpec((1, tq, gw), lambda b, g, i: (b, i, g)),
            pl.BlockSpec((1, tall, B_HD), lambda b, g, i: (b, 0, g)),
            pl.BlockSpec((1, tall, B_HD), lambda b, g, i: (b, 0, g)),
            pl.BlockSpec((1, tq, gw), lambda b, g, i: (b, i, g)),
        ],
        out_specs=pl.BlockSpec((1, tq, gw), lambda b, g, i: (b, i, g)),
        compiler_params=_cparams(("parallel", "parallel", "arbitrary")),
        name="attn",
    )(qb, kb, vb, szb)


def _conformer_kernel(ya_ref, yb_ref, x_ref, wa_ref, wb_ref, pg0_ref, gate0_ref, sh_ref, sc_ref, g1_ref, w1_ref,
                      b1_ref, dw_ref, dwb_ref, lng_ref, lnb_ref, wo_ref, bo_ref, pg1_ref, gate1_ref,
                      o_ref, h_scr, u_ring, sz_ring, xl_ring, buf, act, *, tiles_per_seq, halo):
    step_id = pl.program_id(0)
    d = x_ref.shape[2]
    tm = x_ref.shape[1]
    nsub = 2
    ts = tm // nsub
    subs = [slice(k * ts, (k + 1) * ts) for k in range(nsub)]
    u_cur, u_prev, u_pprev = step_id % 3, (step_id + 2) % 3, (step_id + 1) % 3
    r_cur, r_prev = step_id % 2, (step_id + 1) % 2
    ip = jnp.maximum(step_id - 1, 0) % tiles_per_seq

    @pl.when(step_id == 0)
    def _():
        u_ring[...] = jnp.zeros_like(u_ring)
        sz_ring[...] = jnp.zeros_like(sz_ring)
        xl_ring[...] = jnp.zeros_like(xl_ring)

    pad = CONV_K // 2
    cw = 128
    sub = 8
    buf[0:halo, :] = jnp.where(ip > 0, u_ring[u_pprev, tm - halo:tm, :].astype(F32), 0.0)
    buf[halo:halo + tm, :] = u_ring[u_prev].astype(F32)

    def conv_piece(r0, nr, cb):
        cs = slice(cb * cw, (cb + 1) * cw)
        acc = None
        for s in range(sub):
            part = None
            for a in range((halo + pad) // sub + 1):
                j = sub * a + s - (halo - pad)
                if 0 <= j < CONV_K:
                    term = buf[r0 + sub * a:r0 + sub * a + nr + sub, cs] * dw_ref[j:j + 1, cs]
                    part = term if part is None else part + term
            if part is not None:
                acc = part[s:s + nr] if acc is None else acc + part[s:s + nr]
        act[r0:r0 + nr, cs] = acc + dwb_ref[:, cs]

    def activate(r):
        u = act[r, :]
        mu = jnp.mean(u, axis=-1, keepdims=True)
        uc = u - mu
        var = jnp.mean(uc * uc, axis=-1, keepdims=True)
        un = uc * lax.rsqrt(var + LN_EPS) * lng_ref[...] + lnb_ref[...]
        hact = (_silu(un) * sz_ring[r_prev, r, :].astype(F32)).astype(BF16)
        return jnp.dot(hact, wo_ref[...], preferred_element_type=F32) + bo_ref[...]

    def finish(r, out):
        on = out * lax.rsqrt(jnp.mean(out * out, axis=-1, keepdims=True) + NORM_EPS) * pg1_ref[...]
        o_ref[0, r, :] = xl_ring[r_prev, r, :] + gate1_ref[0] * on

    assert nsub == 2 and ts + 2 * halo <= tm + halo
    ncb = d // cw
    first_half = [functools.partial(conv_piece, 0, ts, cb) for cb in range(ncb)]
    second_half = [functools.partial(conv_piece, ts, ts, cb) for cb in range(ncb)]

    def run(pieces, n):
        for _ in range(min(n, len(pieces))):
            pieces.pop(0)()

    mult1 = g1_ref[...] * (1.0 + sc_ref[0])

    def out_proj(r):
        return (jnp.dot(ya_ref[0, r, :], wa_ref[...], preferred_element_type=F32)
                + jnp.dot(yb_ref[0, r, :], wb_ref[...], preferred_element_type=F32))

    def residual_and_modulate(r, out):
        on = out * lax.rsqrt(jnp.mean(out * out, axis=-1, keepdims=True) + NORM_EPS) * pg0_ref[...]
        xl = x_ref[0, r, :] + gate0_ref[0] * on
        xl_ring[r_cur, r, :] = xl
        rs = lax.rsqrt(jnp.mean(xl * xl, axis=-1, keepdims=True) + NORM_EPS)
        h_scr[r, :] = (xl * rs * mult1 + sh_ref[0]).astype(BF16)

    def in_mm(r, part):
        return (jnp.dot(h_scr[r, :], w1_ref[:, part * d:(part + 1) * d], preferred_element_type=F32)
                + b1_ref[:, part * d:(part + 1) * d])

    def glu_store(r, a, gl):
        u_ring[u_cur, r, :] = (a * _sigmoid(gl)).astype(BF16)

    def gate_store(r, z):
        sz_ring[r_cur, r, :] = _silu(z).astype(BF16)

    s0, s1 = subs
    o0 = out_proj(s0)
    o1 = out_proj(s1)
    run(first_half, 2)
    residual_and_modulate(s0, o0)
    a0 = in_mm(s0, 0)
    run(first_half, 2)
    gl0 = in_mm(s0, 1)
    run(first_half, 2)
    z0 = in_mm(s0, 2)
    glu_store(s0, a0, gl0)
    gate_store(s0, z0)
    residual_and_modulate(s1, o1)
    run(first_half, ncb)
    a1 = in_mm(s1, 0)
    buf[halo + tm:2 * halo + tm, :] = jnp.where(ip < tiles_per_seq - 1, u_ring[u_cur, 0:halo, :].astype(F32), 0.0)
    run(second_half, 3)
    gl1 = in_mm(s1, 1)
    run(second_half, 3)
    z1 = in_mm(s1, 2)
    glu_store(s1, a1, gl1)
    gate_store(s1, z1)
    run(second_half, ncb)
    out0 = activate(s0)
    out1 = activate(s1)
    finish(s0, out0)
    finish(s1, out1)


def _conformer(ya, yb, x, wa, wb, post_g0, mod0, mod1, pre_g1, w1, b1, dw_w, dw_b, ln_g, ln_b, w_out, b_out,
               post_g1, tm):
    bsz, t, d = x.shape
    aw = ya.shape[2]
    bw = yb.shape[2]
    halo = 16
    assert halo >= CONV_K // 2 and t % tm == 0
    tiles = t // tm
    n_steps = bsz * tiles
    rows = mod0.shape[0]
    m0 = mod0.reshape(rows, 1, 3 * d)
    m1 = mod1.reshape(rows, 1, 3 * d)

    def cur(s):
        p = jnp.minimum(s, n_steps - 1)
        return p // tiles, p % tiles

    def prev(s):
        p = jnp.maximum(s - 1, 0)
        return p // tiles, p % tiles

    once = pl.Buffered(1)
    tok = lambda c: pl.BlockSpec((1, tm, c), lambda s: (cur(s)[0], cur(s)[1], 0))
    vec = lambda: pl.BlockSpec((1, d), lambda s: (0, 0))
    const = lambda a: pl.BlockSpec(a.shape, lambda s: (0, 0), pipeline_mode=once)
    kern = functools.partial(_conformer_kernel, tiles_per_seq=tiles, halo=halo)
    return pl.pallas_call(
        kern,
        out_shape=jax.ShapeDtypeStruct((bsz, t, d), F32),
        grid=(n_steps + 1,),
        in_specs=[
            tok(aw), tok(bw), tok(d),
            const(wa), const(wb),
            vec(),
            pl.BlockSpec((1, 1, d), lambda s: (cur(s)[0], 0, 2)),
            pl.BlockSpec((1, 1, d), lambda s: (cur(s)[0], 0, 0)),
            pl.BlockSpec((1, 1, d), lambda s: (cur(s)[0], 0, 1)),
            vec(),
            const(w1),
            pl.BlockSpec((1, 3 * d), lambda s: (0, 0)),
            pl.BlockSpec(dw_w.shape, lambda s: (0, 0)),
            vec(), vec(), vec(),
            const(w_out),
            vec(), vec(),
            pl.BlockSpec((1, 1, d), lambda s: (prev(s)[0], 0, 2)),
        ],
        out_specs=pl.BlockSpec((1, tm, d), lambda s: (prev(s)[0], prev(s)[1], 0)),
        scratch_shapes=[
            pltpu.VMEM((tm, d), BF16),
            pltpu.VMEM((3, tm, d), BF16),
            pltpu.VMEM((2, tm, d), BF16),
            pltpu.VMEM((2, tm, d), F32),
            pltpu.VMEM((tm + 2 * halo, d), F32),
            pltpu.VMEM((tm, d), F32),
        ],
        compiler_params=_cparams(("arbitrary",)),
        name="conformer",
    )(ya, yb, x, wa, wb, post_g0.reshape(1, d), m0, m1, m1, pre_g1.reshape(1, d), w1, b1.reshape(1, 3 * d),
      dw_w, dw_b.reshape(1, d), ln_g.reshape(1, d), ln_b.reshape(1, d), w_out, b_out.reshape(1, d),
      post_g1.reshape(1, d), m1)


def _rope_tables(t, nctx):
    f32 = np.float32
    rows = t // GRID_W
    row = np.repeat(np.arange(rows, dtype=f32), GRID_W)
    col = np.tile(np.arange(GRID_W, dtype=f32), rows)
    axis_dim = B_HD // 2
    inv_freq = (f32(ROPE_THETA) ** (-np.arange(0, axis_dim, 2, dtype=f32) / f32(axis_dim))).astype(f32)
    ang = np.concatenate([row[:, None] * inv_freq, col[:, None] * inv_freq], axis=1).astype(f32)
    cs = np.concatenate([np.cos(ang), np.cos(ang)], axis=1)
    sn = np.concatenate([-np.sin(ang), np.sin(ang)], axis=1)
    cs = np.concatenate([cs, np.ones((nctx, B_HD), f32)], axis=0).astype(f32)
    sn = np.concatenate([sn, np.zeros((nctx, B_HD), f32)], axis=0).astype(f32)
    return jnp.asarray(cs), jnp.asarray(sn)


def _head_perm():
    q = B_HD // 4
    return np.concatenate([np.arange(0, q), np.arange(2 * q, 3 * q), np.arange(q, 2 * q), np.arange(3 * q, 4 * q)])


def kernel(x, c, ctx, c_ctx, ada_w, ada_b, pre_norm_g, post_norm_g, ev_w_in, ev_short_conv_w, ev_a_log,
           ev_dt_bias, ev_gdn_norm_g, ev_q_norm_g, ev_k_norm_g, ev_w_out, od_w_in, od_b_in, od_dw_w, od_dw_b,
           od_ln_g, od_ln_b, od_w_out, od_b_out):
    bsz, t, d = x.shape
    nctx = ctx.shape[1]
    a_width = d // 2
    a_heads = a_width // A_DV
    a_qkv = a_heads * (2 * A_DK + A_DV)
    b_width = d - a_width
    b_heads = b_width // B_HD
    b_kv = b_heads // 2
    assert ada_w.shape[0] == 2 and ev_w_in.shape[0] == 1 and od_w_in.shape[0] == 1

    mod_rows = ((bsz + 1 + 7) // 8) * 8
    cond = jnp.zeros((mod_rows, d), F32).at[:bsz].set(c).at[bsz].set(c_ctx)
    mod = _ada(cond, ada_w, ada_b)

    w_in = ev_w_in[0]
    splits = np.cumsum([0, a_qkv, a_width, 2 * a_heads, 2 * a_heads, b_heads * B_HD, b_kv * B_HD, b_kv * B_HD,
                        b_width])
    w_qkv, w_za, w_a, w_b, w_qb, w_kb, w_vb, w_zb = [w_in[:, splits[j]:splits[j + 1]] for j in range(8)]
    perm = _head_perm()
    perm_q = np.concatenate([h * B_HD + perm for h in range(b_heads)])
    perm_k = np.concatenate([h * B_HD + perm for h in range(b_kv)])
    ab_pad = 128 - 4 * a_heads
    pieces = [("qkv", w_qkv), ("za", w_za), ("qb", w_qb[:, perm_q]), ("kb", w_kb[:, perm_k]), ("vb", w_vb),
              ("zb", w_zb), ("ab", jnp.concatenate([w_a, w_b, jnp.zeros((d, ab_pad), F32)], axis=1))]
    secs, off = {}, 0
    for name, wpart in pieces:
        secs[name] = (off, off + wpart.shape[1])
        off += wpart.shape[1]
    w2 = jnp.concatenate([p[1] for p in pieces], axis=1).astype(BF16)
    cs_tab, sn_tab = _rope_tables(t, nctx)
    qg = ev_q_norm_g[0][perm].reshape(1, B_HD)
    kg = ev_k_norm_g[0][perm].reshape(1, B_HD)
    alog = jnp.zeros((1, 128), F32).at[0, :2 * a_heads].set(ev_a_log[0].reshape(-1))
    dtb = jnp.zeros((1, 128), F32).at[0, :2 * a_heads].set(ev_dt_bias[0].reshape(-1))

    tm0 = nctx
    qkvn, sza, qb, kb, vb, szb, gb = _inproj0(x, ctx, mod[0], pre_norm_g[0], w2, secs, ev_short_conv_w[0], cs_tab,
                                              sn_tab, qg, kg, alog, dtb, tm0, a_heads)

    tall = t + nctx
    n_chunks = tall // GDN_CHUNK
    ncp = ((n_chunks + 7) // 8) * 8
    grow = gb.reshape(bsz, n_chunks, GDN_CHUNK, 4, a_heads).transpose(0, 4, 3, 1, 2)
    grow = jnp.pad(grow, ((0, 0), (0, 0), (0, 0), (0, ncp - n_chunks), (0, 0)))
    ya = _gdn(qkvn, grow, sza, ev_gdn_norm_g[0].reshape(1, A_DV), a_heads, t)

    yb = _attn(qb, kb, vb, szb, t, 512)

    w_out0 = ev_w_out[0].astype(BF16)
    return _conformer(ya, yb, x, w_out0[:a_width], w_out0[a_width:], post_norm_g[0], mod[0], mod[1], pre_norm_g[1],
                      od_w_in[0].astype(BF16), od_b_in[0], od_dw_w[0], od_dw_b[0], od_ln_g[0], od_ln_b[0],
                      od_w_out[0].astype(BF16), od_b_out[0], post_norm_g[1], 512)
```

```python
import functools
import math

import numpy as np
import jax
import jax.numpy as jnp
from jax import lax
from jax.experimental import pallas as pl
from jax.experimental.pallas import tpu as pltpu

F32 = jnp.float32
BF16 = jnp.bfloat16
HIGHEST = lax.Precision.HIGHEST

GRID_W = 64
A_DK = 128
A_DV = 128
SHORT_CONV = 5
GDN_CHUNK = 64
INV_BLOCK = 16
B_HD = 128
ROPE_THETA = 10000.0
CONV_K = 31
NORM_EPS = 1e-6
LN_EPS = 1e-5
NEG_BIG = -1e30

VMEM_LIMIT = 56 * 1024 * 1024


def _sigmoid(x):
    return 1.0 / (1.0 + jnp.exp(-x))


def _silu(x):
    return x * _sigmoid(x)


def _softplus(x):
    return jnp.maximum(x, 0.0) + jnp.log(1.0 + jnp.exp(-jnp.abs(x)))


def _cparams(sem):
    return pltpu.CompilerParams(dimension_semantics=sem, vmem_limit_bytes=VMEM_LIMIT)


def _ada_kernel(c_ref, w_ref, b_ref, o_ref):
    s = _silu(c_ref[...])
    o_ref[0] = jnp.dot(s, w_ref[0], preferred_element_type=F32, precision=HIGHEST) + b_ref[0]


def _ada(cond, ada_w, ada_b):
    depth, d, d3 = ada_w.shape
    rows = cond.shape[0]
    nt = d3 // d
    return pl.pallas_call(
        _ada_kernel,
        out_shape=jax.ShapeDtypeStruct((depth, rows, d3), F32),
        grid=(depth, nt),
        in_specs=[
            pl.BlockSpec((rows, d), lambda l, j: (0, 0)),
            pl.BlockSpec((1, d, d), lambda l, j: (l, 0, j)),
            pl.BlockSpec((1, 1, d), lambda l, j: (l, 0, j)),
        ],
        out_specs=pl.BlockSpec((1, rows, d), lambda l, j: (l, 0, j)),
        compiler_params=_cparams(("parallel", "parallel")),
        name="ada",
    )(cond, ada_w, ada_b.reshape(depth, 1, d3))


def _inproj0_kernel(x_ref, xp_ref, xn_ref, ctx_ref, sh_ref, sc_ref, g_ref, w_ref, cw_ref, cs_ref, sn_ref, qg_ref,
                    kg_ref, alog_ref, dtb_ref,
                    qkv_o, za_o, qb_o, kb_o, vb_o, zb_o, gb_o, h_scr, *, n_lat_tiles, n_tiles, secs, halo,
                    n_gdn_heads):
    step_id = pl.program_id(0)
    wslot = step_id % 2
    rslot = 1 - wslot
    tm = x_ref.shape[1]
    last = pl.num_programs(0) - 2
    i = jnp.minimum(step_id, last) % n_tiles

    @pl.when(step_id == 0)
    def _():
        h_scr[...] = jnp.zeros_like(h_scr)

    def modulate_tile():
        mult = g_ref[...] * (1.0 + sc_ref[0])
        shift = sh_ref[0]

        def modulated(xf):
            rs = lax.rsqrt(jnp.mean(xf * xf, axis=-1, keepdims=True) + NORM_EPS)
            return xf * rs * mult + shift

        prev_ok = jnp.logical_and(i > 0, i < n_lat_tiles)
        next_ok = i < n_lat_tiles - 1
        h_scr[wslot, 0:halo, :] = jnp.where(prev_ok, modulated(xp_ref[0]), 0.0).astype(BF16)
        h_scr[wslot, halo + tm:2 * halo + tm, :] = jnp.where(next_ok, modulated(xn_ref[0]), 0.0).astype(BF16)
        src = jnp.where(i < n_lat_tiles, x_ref[0], ctx_ref[0])
        h_scr[wslot, halo:halo + tm, :] = modulated(src).astype(BF16)

    def proj(name):
        c0, c1 = secs[name]
        return lambda: jnp.dot(h_scr[rslot, halo:halo + tm, :], w_ref[:, c0:c1], preferred_element_type=F32)

    pad = SHORT_CONV // 2
    pw = 2 * A_DK
    hpb = pw // A_DK

    def qkv_mm(pb):
        c0 = secs["qkv"][0] + pb * pw
        return lambda: jnp.dot(h_scr[rslot], w_ref[:, c0:c0 + pw], preferred_element_type=F32)

    def qkv_epi(pb):
        def epi(y):
            for sb in range(hpb):
                hb = pb * hpb + sb
                cl = slice(hb * A_DK, (hb + 1) * A_DK)
                acc = None
                for j in range(SHORT_CONV):
                    term = y[halo - pad + j:halo - pad + j + tm, sb * A_DK:(sb + 1) * A_DK] * cw_ref[j:j + 1, cl]
                    acc = term if acc is None else acc + term
                u = _silu(acc)
                if hb < 2 * n_gdn_heads:
                    u = u * lax.rsqrt(jnp.sum(u * u, axis=-1, keepdims=True) + NORM_EPS)
                    if hb < n_gdn_heads:
                        u = u * (A_DK ** -0.5)
                qkv_o[0, :, cl] = u.astype(BF16)
        return epi

    def gate_epi(o_ref):
        def epi(y):
            o_ref[0] = _silu(y).astype(BF16)
        return epi

    def plain_epi(y):
        vb_o[0] = y.astype(BF16)

    def norm_rope_epi(o_ref, g_ref_, out_scale):
        def epi(y):
            cs = cs_ref[...]
            sn = sn_ref[...]
            g = g_ref_[...]
            for hh in range(y.shape[1] // B_HD):
                sl = slice(hh * B_HD, (hh + 1) * B_HD)
                yh = y[:, sl]
                ms = jnp.mean(yh * yh, axis=-1, keepdims=True)
                yn = yh * lax.rsqrt(ms + NORM_EPS) * g
                o_ref[0, :, sl] = ((yn * cs + pltpu.roll(yn, B_HD // 2, 1) * sn) * out_scale).astype(BF16)
        return epi

    def decay_epi(ab):
        nab = gb_o.shape[1]
        gval = -jnp.exp(alog_ref[...]) * _softplus(ab + dtb_ref[...])
        bval = _sigmoid(ab)
        lane = lax.broadcasted_iota(jnp.int32, ab.shape, 1)
        gb_o[0] = jnp.where(lane < nab // 2, gval, bval).T[:nab, :]

    n_qkv_blocks = (secs["qkv"][1] - secs["qkv"][0]) // pw
    heavy = [(qkv_mm(pb), qkv_epi(pb)) for pb in range(n_qkv_blocks)]
    light = [(proj("za"), gate_epi(za_o)), (proj("zb"), gate_epi(zb_o)), (proj("vb"), plain_epi),
             (proj("qb"), norm_rope_epi(qb_o, qg_ref, B_HD ** -0.5)), (proj("kb"), norm_rope_epi(kb_o, kg_ref, 1.0)),
             (proj("ab"), decay_epi)]
    stages = []
    for k in range(max(len(heavy), len(light))):
        stages += heavy[k:k + 1] + light[k:k + 1]
    y_next = stages[0][0]()
    modulate_tile()
    for k, (_, epi) in enumerate(stages):
        y = y_next
        if k + 1 < len(stages):
            y_next = stages[k + 1][0]()
        epi(y)


def _inproj0(x, ctx, mod0, pre_g, w2, secs, conv_w, cs_tab, sn_tab, qg, kg, alog, dtb, tm, n_gdn_heads):
    bsz, t, d = x.shape
    nctx = ctx.shape[1]
    tall = t + nctx
    assert t % tm == 0 and nctx == tm
    n_lat = t // tm
    ntile = n_lat + 1
    nab = 16
    halo = 16
    nhb = tm // halo
    last_hblk = t // halo - 1

    def widths(name):
        return secs[name][1] - secs[name][0]

    n_steps = bsz * ntile

    def cur(s):
        p = jnp.minimum(s, n_steps - 1)
        return p // ntile, p % ntile

    def prev(s):
        p = jnp.maximum(s - 1, 0)
        return p // ntile, p % ntile

    def row_spec(c):
        return pl.BlockSpec((1, tm, c), lambda s: (prev(s)[0], prev(s)[1], 0))

    def mod_row(s):
        b, i = cur(s)
        return jnp.where(i < n_lat, b, ctx_row)

    mod_rows = mod0.shape[0]
    mod3 = mod0.reshape(mod_rows, 1, 3 * d)
    ctx_row = bsz
    outs = [
        jax.ShapeDtypeStruct((bsz, tall, widths("qkv")), BF16),
        jax.ShapeDtypeStruct((bsz, tall, widths("za")), BF16),
        jax.ShapeDtypeStruct((bsz, tall, widths("qb")), BF16),
        jax.ShapeDtypeStruct((bsz, tall, widths("kb")), BF16),
        jax.ShapeDtypeStruct((bsz, tall, widths("vb")), BF16),
        jax.ShapeDtypeStruct((bsz, tall, widths("zb")), BF16),
        jax.ShapeDtypeStruct((bsz, nab, tall), F32),
    ]
    kern = functools.partial(_inproj0_kernel, n_lat_tiles=n_lat, n_tiles=ntile, secs=secs, halo=halo,
                             n_gdn_heads=n_gdn_heads)
    return pl.pallas_call(
        kern,
        out_shape=outs,
        grid=(n_steps + 1,),
        in_specs=[
            pl.BlockSpec((1, tm, d), lambda s: (cur(s)[0], jnp.minimum(cur(s)[1], n_lat - 1), 0)),
            pl.BlockSpec((1, halo, d), lambda s: (cur(s)[0], jnp.clip(cur(s)[1] * nhb - 1, 0, last_hblk), 0)),
            pl.BlockSpec((1, halo, d), lambda s: (cur(s)[0], jnp.clip((cur(s)[1] + 1) * nhb, 0, last_hblk), 0)),
            pl.BlockSpec((1, tm, d), lambda s: (cur(s)[0], 0, 0)),
            pl.BlockSpec((1, 1, d), lambda s: (mod_row(s), 0, 0)),
            pl.BlockSpec((1, 1, d), lambda s: (mod_row(s), 0, 1)),
            pl.BlockSpec((1, d), lambda s: (0, 0)),
            pl.BlockSpec(w2.shape, lambda s: (0, 0)),
            pl.BlockSpec(conv_w.shape, lambda s: (0, 0)),
            pl.BlockSpec((tm, B_HD), lambda s: (prev(s)[1], 0)),
            pl.BlockSpec((tm, B_HD), lambda s: (prev(s)[1], 0)),
            pl.BlockSpec((1, B_HD), lambda s: (0, 0)),
            pl.BlockSpec((1, B_HD), lambda s: (0, 0)),
            pl.BlockSpec((1, 128), lambda s: (0, 0)),
            pl.BlockSpec((1, 128), lambda s: (0, 0)),
        ],
        out_specs=[row_spec(widths("qkv")), row_spec(widths("za")), row_spec(widths("qb")), row_spec(widths("kb")),
                   row_spec(widths("vb")), row_spec(widths("zb")),
                   pl.BlockSpec((1, nab, tm), lambda s: (prev(s)[0], 0, prev(s)[1]))],
        scratch_shapes=[pltpu.VMEM((2, tm + 2 * halo, d), BF16)],
        compiler_params=_cparams(("arbitrary",)),
        name="inproj0",
    )(x, x, x, ctx, mod3, mod3, pre_g.reshape(1, d), w2, conv_w, cs_tab, sn_tab, qg, kg, alog, dtb)


def _gdn_kernel(q_ref, k_ref, v_ref, gr_ref, sz_ref, ng_ref, o_ref,
                nq_scr, z_scr, au_scr, gt_scr, cr_scr, s_scr, oacc, *, n_lat_chunks, n_chunks, group):
    C = GDN_CHUNK
    step_id = pl.program_id(0)
    wset = step_id % 2
    rset = 1 - wset
    ri = lax.broadcasted_iota(jnp.int32, (C, C), 0)
    ci = lax.broadcasted_iota(jnp.int32, (C, C), 1)
    incl = (ri >= ci, ri <= ci)
    strict = (ri > ci, ri < ci)
    eye = (ri == ci).astype(F32)
    tri_row = ((ri <= ci).astype(F32), (ri >= ci).astype(F32))
    tri_col = (incl[0].astype(F32), incl[1].astype(F32))

    @pl.when(step_id == 0)
    def _():
        nq_scr[...] = jnp.zeros_like(nq_scr)
        z_scr[...] = jnp.zeros_like(z_scr)
        au_scr[...] = jnp.zeros_like(au_scr)
        gt_scr[...] = jnp.zeros_like(gt_scr)

    for d in range(2):
        cr_scr[d] = jnp.dot(gr_ref[0, 0, d], tri_row[d], preferred_element_type=F32, precision=HIGHEST)
    s_scr[...] = jnp.zeros_like(s_scr)
    oacc[...] = jnp.zeros_like(oacc)

    nt = (((1,), (1,)), ((), ()))
    nsq = int(math.log2(INV_BLOCK)) - 1
    diag_blk = (ri // INV_BLOCK) == (ci // INV_BLOCK)
    off_blks = []
    bs = INV_BLOCK
    while bs < C:
        off_blks.append(jnp.logical_and((ri // (2 * bs)) == (ci // (2 * bs)), (ri // bs) != (ci // bs)))
        bs *= 2

    def mm(a, b):
        return jnp.dot(a, b, preferred_element_type=F32)

    tl = (((0,), (0,)), ((), ()))

    def prep(chunk_ids):
        ch = []
        for n in chunk_ids:
            rows = pl.ds(pl.multiple_of(n * C, C), C)
            kb = k_ref[0, rows, :]
            qb = q_ref[0, rows, :]
            k32 = kb.astype(F32)
            q32 = qb.astype(F32)
            v32 = v_ref[0, rows, :].astype(F32)
            kk = lax.dot_general(kb, kb, nt, preferred_element_type=F32)
            qk = lax.dot_general(qb, kb, nt, preferred_element_type=F32)
            for d in range(2):
                g_row = gr_ref[0, 0, d, pl.ds(n, 1), :]
                beta_row = gr_ref[0, 0, 2 + d, pl.ds(n, 1), :]
                c_row = cr_scr[d, pl.ds(n, 1), :]
                c_col = jnp.sum(tri_col[d] * g_row, axis=-1, keepdims=True)
                beta_col = jnp.sum(eye * beta_row, axis=-1, keepdims=True)
                tot = jnp.sum(g_row, axis=-1, keepdims=True)
                decay = jnp.exp(jnp.where(incl[d], c_col - c_row, NEG_BIG))
                x = jnp.where(strict[d], kk * decay, 0.0) * (-beta_col)
                e_col = jnp.exp(c_col)
                rhs = jnp.concatenate([v32 * beta_col, k32 * (beta_col * e_col)], axis=1)
                qd = q32 * e_col
                kd = (k32 * jnp.exp(tot - c_col)).astype(BF16)
                am = jnp.where(incl[d], qk * decay, 0.0).astype(BF16)
                gt_scr[2 * wset + d, n] = jnp.broadcast_to(jnp.exp(tot), (8, 128))
                ch.append((d, n, rows, x, rhs, qd, kd, am))
        yield
        xs = [c[3] for c in ch]
        ys = [jnp.where(diag_blk, x, 0.0) for x in xs]
        xbs = [y.astype(BF16) for y in ys]
        ps = [mm(xb, xb) for xb in xbs]
        yield
        for m in range(nsq):
            pbs = [p.astype(BF16) for p in ps]
            ybs = [y.astype(BF16) for y in ys]
            if m < nsq - 1:
                rs = [mm(jnp.concatenate([pb, yb], axis=0), pb) for pb, yb in zip(pbs, ybs)]
                ys = [y + p + r[C:] for y, p, r in zip(ys, ps, rs)]
                ps = [r[:C] for r in rs]
            else:
                ys = [y + p + mm(yb, pb) for y, p, yb, pb in zip(ys, ps, ybs, pbs)]
            yield
        for off in off_blks:
            xos = [jnp.where(off, x, 0.0) for x in xs]
            ybs = [y.astype(BF16) for y in ys]
            ts = [xo + mm(yb, xo.astype(BF16)) for xo, yb in zip(xos, ybs)]
            yield
            ys = [y + t + mm(t.astype(BF16), yb) for y, t, yb in zip(ys, ts, ybs)]
            yield
        ybs = [y.astype(BF16) for y in ys]
        uws = [c[4] + mm(yb, c[4].astype(BF16)) for c, yb in zip(ch, ybs)]
        uwbs = [uw.astype(BF16) for uw in uws]
        yield
        kzs = [lax.dot_general(c[6], uwb, tl, preferred_element_type=F32) for c, uwb in zip(ch, uwbs)]
        azs = [mm(c[7], uwb) for c, uwb in zip(ch, uwbs)]
        for c, kz, az in zip(ch, kzs, azs):
            cd, n, rows = 2 * wset + c[0], c[1], c[2]
            z_scr[cd, n] = kz[:, :A_DV].astype(z_scr.dtype)
            nq_scr[cd, n, 0:A_DK, :] = kz[:, A_DV:].astype(BF16)
            nq_scr[cd, n, A_DK:A_DK + C, :] = (c[5] - az[:, A_DV:]).astype(BF16)
            au_scr[cd, rows, :] = az[:, :A_DV].astype(au_scr.dtype)
        yield

    def scan_step(ns, with_out):
        nrow = A_DK + C if with_out else A_DK
        ss = [s_scr[d] for d in range(2)]
        rs = [mm(nq_scr[2 * rset + d, ns[d], 0:nrow, :], ss[d].astype(BF16)) for d in range(2)]
        for d in range(2):
            cd, n = 2 * rset + d, ns[d]
            s_scr[d] = ss[d] * gt_scr[cd, n][0:1, :] - rs[d][:A_DK] + z_scr[cd, n].astype(F32)
            if with_out:
                rows = pl.ds(pl.multiple_of(n * C, C), C)
                o = oacc[rows, :] + rs[d][A_DK:] + au_scr[cd, rows, :].astype(F32)
                oacc[rows, :] = o
                on = o * lax.rsqrt(jnp.mean(o * o, axis=-1, keepdims=True) + NORM_EPS) * ng_ref[...]
                o_ref[0, rows, :] = (on * sz_ref[0, rows, :].astype(F32)).astype(BF16)

    def interleave(stages, scan_steps):
        pending = list(scan_steps)
        for _ in stages:
            if pending:
                pending.pop(0)()
        for rest in pending:
            rest()

    n_ctx_chunks = n_chunks - n_lat_chunks
    interleave(prep([n_lat_chunks + i for i in range(n_ctx_chunks)]),
               [functools.partial(scan_step, (n_lat_chunks + j, n_chunks - 1 - j), False)
                for j in range(n_ctx_chunks)])

    def body(gi, carry):
        base = gi * group
        interleave(prep([base + i for i in range(group)]),
                   [functools.partial(scan_step, (base + j, n_lat_chunks - 1 - base - j), True)
                    for j in range(group)])
        return carry

    lax.fori_loop(0, n_lat_chunks // group, body, 0)


def _gdn(qkvn, grow, sza, norm_g, n_heads, t):
    bsz, tall, _ = qkvn.shape
    C = GDN_CHUNK
    n_chunks = tall // C
    n_lat_chunks = t // C
    ncp = grow.shape[3]
    group = 8
    assert n_lat_chunks % group == 0
    kern = functools.partial(_gdn_kernel, n_lat_chunks=n_lat_chunks, n_chunks=n_chunks, group=group)
    n_pairs = bsz * n_heads

    def cur(s):
        p = jnp.minimum(s, n_pairs - 1)
        return p // n_heads, p % n_heads

    def prev(s):
        p = jnp.maximum(s - 1, 0)
        return p // n_heads, p % n_heads

    return pl.pallas_call(
        kern,
        out_shape=jax.ShapeDtypeStruct((bsz, t, n_heads * A_DV), BF16),
        grid=(n_pairs + 1,),
        in_specs=[
            pl.BlockSpec((1, tall, A_DK), lambda s: (cur(s)[0], 0, cur(s)[1])),
            pl.BlockSpec((1, tall, A_DK), lambda s: (cur(s)[0], 0, n_heads + cur(s)[1])),
            pl.BlockSpec((1, tall, A_DV), lambda s: (cur(s)[0], 0, 2 * n_heads + cur(s)[1])),
            pl.BlockSpec((1, 1, 4, ncp, C), lambda s: (cur(s)[0], cur(s)[1], 0, 0, 0)),
            pl.BlockSpec((1, t, A_DV), lambda s: (prev(s)[0], 0, prev(s)[1])),
            pl.BlockSpec((1, A_DV), lambda s: (0, 0)),
        ],
        out_specs=pl.BlockSpec((1, t, A_DV), lambda s: (prev(s)[0], 0, prev(s)[1])),
        scratch_shapes=[
            pltpu.VMEM((4, n_chunks, A_DK + C, A_DK), BF16),
            pltpu.VMEM((4, n_chunks, A_DK, A_DV), BF16),
            pltpu.VMEM((4, tall, A_DV), BF16),
            pltpu.VMEM((4, n_chunks, 8, 128), F32),
            pltpu.VMEM((2, ncp, C), F32),
            pltpu.VMEM((2, A_DK, A_DV), F32),
            pltpu.VMEM((t, A_DV), F32),
        ],
        compiler_params=_cparams(("arbitrary",)),
        name="gdn",
    )(qkvn, qkvn, qkvn, grow, sza, norm_g)


def _attn_kernel(q_ref, k_ref, v_ref, sz_ref, o_ref, *, kv_blocks):
    nt = (((1,), (1,)), ((), ()))
    n_heads = q_ref.shape[2] // B_HD
    items = [(hh, blk) for hh in range(n_heads) for blk in kv_blocks]

    def scores(item):
        hh, (k0, k1) = item
        q = q_ref[0, :, hh * B_HD:(hh + 1) * B_HD]
        return lax.dot_general(q, k_ref[0, k0:k1, :], nt, preferred_element_type=F32)

    s_next = scores(items[0])
    m = l = acc = None
    for idx, (hh, (k0, k1)) in enumerate(items):
        s = s_next
        if idx + 1 < len(items):
            s_next = scores(items[idx + 1])
        bm = jnp.max(s, axis=-1, keepdims=True)
        if k0 == kv_blocks[0][0]:
            m = bm
            p = jnp.exp(s - m)
            l = jnp.sum(p, axis=-1, keepdims=True)
            acc = jnp.dot(p.astype(BF16), v_ref[0, k0:k1, :], preferred_element_type=F32)
        else:
            m_new = jnp.maximum(m, bm)
            alpha = jnp.exp(m - m_new)
            p = jnp.exp(s - m_new)
            l = alpha * l + jnp.sum(p, axis=-1, keepdims=True)
            acc = alpha * acc + jnp.dot(p.astype(BF16), v_ref[0, k0:k1, :], preferred_element_type=F32)
            m = m_new
        if k1 == kv_blocks[-1][1]:
            sl = slice(hh * B_HD, (hh + 1) * B_HD)
            o_ref[0, :, sl] = (acc * (1.0 / l) * sz_ref[0, :, sl].astype(F32)).astype(BF16)


def _attn(qb, kb, vb, szb, t, tq):
    bsz, tall, qw = qb.shape
    n_kv = kb.shape[2] // B_HD
    gw = qw // n_kv
    kvb = 1024
    edges = list(range(0, tall, kvb)) + [tall]
    kv_blocks = tuple((edges[j], edges[j + 1]) for j in range(len(edges) - 1))
    return pl.pallas_call(
        functools.partial(_attn_kernel, kv_blocks=kv_blocks),
        out_shape=jax.ShapeDtypeStruct((bsz, t, qw), BF16),
        grid=(bsz, n_kv, t // tq),
        in_specs=[
            pl.BlockSpec((1, tq, gw), lambda b, g, i: (b, i, g)),
            pl.BlockSpec((1, tall, B_HD), lambda b, g, i: (b, 0, g)),
            pl.BlockSpec((1, tall, B_HD), lambda b, g, i: (b, 0, g)),
            pl.BlockSpec((1, tq, gw), lambda b, g, i: (b, i, g)),
        ],
        out_specs=pl.BlockSpec((1, tq, gw), lambda b, g, i: (b, i, g)),
        compiler_params=_cparams(("parallel", "parallel", "arbitrary")),
        name="attn",
    )(qb, kb, vb, szb)


def _conformer_kernel(ya_ref, yb_ref, x_ref, wa_ref, wb_ref, pg0_ref, gate0_ref, sh_ref, sc_ref, g1_ref, w1_ref,
                      b1_ref, dw_ref, dwb_ref, lng_ref, lnb_ref, wo_ref, bo_ref, pg1_ref, gate1_ref,
                      o_ref, h_scr, u_ring, sz_ring, xl_ring, buf, act, *, tiles_per_seq, halo):
    step_id = pl.program_id(0)
    d = x_ref.shape[2]
    tm = x_ref.shape[1]
    nsub = 2
    ts = tm // nsub
    subs = [slice(k * ts, (k + 1) * ts) for k in range(nsub)]
    u_cur, u_prev, u_pprev = step_id % 3, (step_id + 2) % 3, (step_id + 1) % 3
    r_cur, r_prev = step_id % 2, (step_id + 1) % 2
    ip = jnp.maximum(step_id - 1, 0) % tiles_per_seq

    @pl.when(step_id == 0)
    def _():
        u_ring[...] = jnp.zeros_like(u_ring)
        sz_ring[...] = jnp.zeros_like(sz_ring)
        xl_ring[...] = jnp.zeros_like(xl_ring)

    pad = CONV_K // 2
    cw = 128
    sub = 8
    buf[0:halo, :] = jnp.where(ip > 0, u_ring[u_pprev, tm - halo:tm, :].astype(F32), 0.0)
    buf[halo:halo + tm, :] = u_ring[u_prev].astype(F32)

    def conv_piece(r0, nr, cb):
        cs = slice(cb * cw, (cb + 1) * cw)
        acc = None
        for s in range(sub):
            part = None
            for a in range((halo + pad) // sub + 1):
                j = sub * a + s - (halo - pad)
                if 0 <= j < CONV_K:
                    term = buf[r0 + sub * a:r0 + sub * a + nr + sub, cs] * dw_ref[j:j + 1, cs]
                    part = term if part is None else part + term
            if part is not None:
                acc = part[s:s + nr] if acc is None else acc + part[s:s + nr]
        act[r0:r0 + nr, cs] = acc + dwb_ref[:, cs]

    def activate(r):
        u = act[r, :]
        mu = jnp.mean(u, axis=-1, keepdims=True)
        uc = u - mu
        var = jnp.mean(uc * uc, axis=-1, keepdims=True)
        un = uc * lax.rsqrt(var + LN_EPS) * lng_ref[...] + lnb_ref[...]
        hact = (_silu(un) * sz_ring[r_prev, r, :].astype(F32)).astype(BF16)
        return jnp.dot(hact, wo_ref[...], preferred_element_type=F32) + bo_ref[...]

    def finish(r, out):
        on = out * lax.rsqrt(jnp.mean(out * out, axis=-1, keepdims=True) + NORM_EPS) * pg1_ref[...]
        o_ref[0, r, :] = xl_ring[r_prev, r, :] + gate1_ref[0] * on

    assert nsub == 2 and ts + 2 * halo <= tm + halo
    ncb = d // cw
    first_half = [functools.partial(conv_piece, 0, ts, cb) for cb in range(ncb)]
    second_half = [functools.partial(conv_piece, ts, ts, cb) for cb in range(ncb)]

    def run(pieces, n):
        for _ in range(min(n, len(pieces))):
            pieces.pop(0)()

    mult1 = g1_ref[...] * (1.0 + sc_ref[0])

    def out_proj(r):
        return (jnp.dot(ya_ref[0, r, :], wa_ref[...], preferred_element_type=F32)
                + jnp.dot(yb_ref[0, r, :], wb_ref[...], preferred_element_type=F32))

    def residual_and_modulate(r, out):
        on = out * lax.rsqrt(jnp.mean(out * out, axis=-1, keepdims=True) + NORM_EPS) * pg0_ref[...]
        xl = x_ref[0, r, :] + gate0_ref[0] * on
        xl_ring[r_cur, r, :] = xl
        rs = lax.rsqrt(jnp.mean(xl * xl, axis=-1, keepdims=True) + NORM_EPS)
        h_scr[r, :] = (xl * rs * mult1 + sh_ref[0]).astype(BF16)

    def in_mm(r, part):
        return (jnp.dot(h_scr[r, :], w1_ref[:, part * d:(part + 1) * d], preferred_element_type=F32)
                + b1_ref[:, part * d:(part + 1) * d])

    def glu_store(r, a, gl):
        u_ring[u_cur, r, :] = (a * _sigmoid(gl)).astype(BF16)

    def gate_store(r, z):
        sz_ring[r_cur, r, :] = _silu(z).astype(BF16)

    s0, s1 = subs
    o0 = out_proj(s0)
    o1 = out_proj(s1)
    run(first_half, 2)
    residual_and_modulate(s0, o0)
    a0 = in_mm(s0, 0)
    run(first_half, 2)
    gl0 = in_mm(s0, 1)
    run(first_half, 2)
    z0 = in_mm(s0, 2)
    glu_store(s0, a0, gl0)
    gate_store(s0, z0)
    residual_and_modulate(s1, o1)
    run(first_half, ncb)
    a1 = in_mm(s1, 0)
    buf[halo + tm:2 * halo + tm, :] = jnp.where(ip < tiles_per_seq - 1, u_ring[u_cur, 0:halo, :].astype(F32), 0.0)
    run(second_half, 3)
    gl1 = in_mm(s1, 1)
    run(second_half, 3)
    z1 = in_mm(s1, 2)
    glu_store(s1, a1, gl1)
    gate_store(s1, z1)
    run(second_half, ncb)
    out0 = activate(s0)
    out1 = activate(s1)
    finish(s0, out0)
    finish(s1, out1)


def _conformer(ya, yb, x, wa, wb, post_g0, mod0, mod1, pre_g1, w1, b1, dw_w, dw_b, ln_g, ln_b, w_out, b_out,
               post_g1, tm):
    bsz, t, d = x.shape
    aw = ya.shape[2]
    bw = yb.shape[2]
    halo = 16
    assert halo >= CONV_K // 2 and t % tm == 0
    tiles = t // tm
    n_steps = bsz * tiles
    rows = mod0.shape[0]
    m0 = mod0.reshape(rows, 1, 3 * d)
    m1 = mod1.reshape(rows, 1, 3 * d)

    def cur(s):
        p = jnp.minimum(s, n_steps - 1)
        return p // tiles, p % tiles

    def prev(s):
        p = jnp.maximum(s - 1, 0)
        return p // tiles, p % tiles

    once = pl.Buffered(1)
    tok = lambda c: pl.BlockSpec((1, tm, c), lambda s: (cur(s)[0], cur(s)[1], 0))
    vec = lambda: pl.BlockSpec((1, d), lambda s: (0, 0))
    const = lambda a: pl.BlockSpec(a.shape, lambda s: (0, 0), pipeline_mode=once)
    kern = functools.partial(_conformer_kernel, tiles_per_seq=tiles, halo=halo)
    return pl.pallas_call(
        kern,
        out_shape=jax.ShapeDtypeStruct((bsz, t, d), F32),
        grid=(n_steps + 1,),
        in_specs=[
            tok(aw), tok(bw), tok(d),
            const(wa), const(wb),
            vec(),
            pl.BlockSpec((1, 1, d), lambda s: (cur(s)[0], 0, 2)),
            pl.BlockSpec((1, 1, d), lambda s: (cur(s)[0], 0, 0)),
            pl.BlockSpec((1, 1, d), lambda s: (cur(s)[0], 0, 1)),
            vec(),
            const(w1),
            pl.BlockSpec((1, 3 * d), lambda s: (0, 0)),
            pl.BlockSpec(dw_w.shape, lambda s: (0, 0)),
            vec(), vec(), vec(),
            const(w_out),
            vec(), vec(),
            pl.BlockSpec((1, 1, d), lambda s: (prev(s)[0], 0, 2)),
        ],
        out_specs=pl.BlockSpec((1, tm, d), lambda s: (prev(s)[0], prev(s)[1], 0)),
        scratch_shapes=[
            pltpu.VMEM((tm, d), BF16),
            pltpu.VMEM((3, tm, d), BF16),
            pltpu.VMEM((2, tm, d), BF16),
            pltpu.VMEM((2, tm, d), F32),
            pltpu.VMEM((tm + 2 * halo, d), F32),
            pltpu.VMEM((tm, d), F32),
        ],
        compiler_params=_cparams(("arbitrary",)),
        name="conformer",
    )(ya, yb, x, wa, wb, post_g0.reshape(1, d), m0, m1, m1, pre_g1.reshape(1, d), w1, b1.reshape(1, 3 * d),
      dw_w, dw_b.reshape(1, d), ln_g.reshape(1, d), ln_b.reshape(1, d), w_out, b_out.reshape(1, d),
      post_g1.reshape(1, d), m1)


def _rope_tables(t, nctx):
    f32 = np.float32
    rows = t // GRID_W
    row = np.repeat(np.arange(rows, dtype=f32), GRID_W)
    col = np.tile(np.arange(GRID_W, dtype=f32), rows)
    axis_dim = B_HD // 2
    inv_freq = (f32(ROPE_THETA) ** (-np.arange(0, axis_dim, 2, dtype=f32) / f32(axis_dim))).astype(f32)
    ang = np.concatenate([row[:, None] * inv_freq, col[:, None] * inv_freq], axis=1).astype(f32)
    cs = np.concatenate([np.cos(ang), np.cos(ang)], axis=1)
    sn = np.concatenate([-np.sin(ang), np.sin(ang)], axis=1)
    cs = np.concatenate([cs, np.ones((nctx, B_HD), f32)], axis=0).astype(f32)
    sn = np.concatenate([sn, np.zeros((nctx, B_HD), f32)], axis=0).astype(f32)
    return jnp.asarray(cs), jnp.asarray(sn)


def _head_perm():
    q = B_HD // 4
    return np.concatenate([np.arange(0, q), np.arange(2 * q, 3 * q), np.arange(q, 2 * q), np.arange(3 * q, 4 * q)])


def kernel(x, c, ctx, c_ctx, ada_w, ada_b, pre_norm_g, post_norm_g, ev_w_in, ev_short_conv_w, ev_a_log,
           ev_dt_bias, ev_gdn_norm_g, ev_q_norm_g, ev_k_norm_g, ev_w_out, od_w_in, od_b_in, od_dw_w, od_dw_b,
           od_ln_g, od_ln_b, od_w_out, od_b_out):
    bsz, t, d = x.shape
    nctx = ctx.shape[1]
    a_width = d // 2
    a_heads = a_width // A_DV
    a_qkv = a_heads * (2 * A_DK + A_DV)
    b_width = d - a_width
    b_heads = b_width // B_HD
    b_kv = b_heads // 2
    assert ada_w.shape[0] == 2 and ev_w_in.shape[0] == 1 and od_w_in.shape[0] == 1

    mod_rows = ((bsz + 1 + 7) // 8) * 8
    cond = jnp.zeros((mod_rows, d), F32).at[:bsz].set(c).at[bsz].set(c_ctx)
    mod = _ada(cond, ada_w, ada_b)

    w_in = ev_w_in[0]
    splits = np.cumsum([0, a_qkv, a_width, 2 * a_heads, 2 * a_heads, b_heads * B_HD, b_kv * B_HD, b_kv * B_HD,
                        b_width])
    w_qkv, w_za, w_a, w_b, w_qb, w_kb, w_vb, w_zb = [w_in[:, splits[j]:splits[j + 1]] for j in range(8)]
    perm = _head_perm()
    perm_q = np.concatenate([h * B_HD + perm for h in range(b_heads)])
    perm_k = np.concatenate([h * B_HD + perm for h in range(b_kv)])
    ab_pad = 128 - 4 * a_heads
    pieces = [("qkv", w_qkv), ("za", w_za), ("qb", w_qb[:, perm_q]), ("kb", w_kb[:, perm_k]), ("vb", w_vb),
              ("zb", w_zb), ("ab", jnp.concatenate([w_a, w_b, jnp.zeros((d, ab_pad), F32)], axis=1))]
    secs, off = {}, 0
    for name, wpart in pieces:
        secs[name] = (off, off + wpart.shape[1])
        off += wpart.shape[1]
    w2 = jnp.concatenate([p[1].astype(BF16) for p in pieces], axis=1)
    cs_tab, sn_tab = _rope_tables(t, nctx)
    qg = ev_q_norm_g[0][perm].reshape(1, B_HD)
    kg = ev_k_norm_g[0][perm].reshape(1, B_HD)
    alog = jnp.zeros((1, 128), F32).at[0, :2 * a_heads].set(ev_a_log[0].reshape(-1))
    dtb = jnp.zeros((1, 128), F32).at[0, :2 * a_heads].set(ev_dt_bias[0].reshape(-1))

    tm0 = nctx
    qkvn, sza, qb, kb, vb, szb, gb = _inproj0(x, ctx, mod[0], pre_norm_g[0], w2, secs, ev_short_conv_w[0], cs_tab,
                                              sn_tab, qg, kg, alog, dtb, tm0, a_heads)

    tall = t + nctx
    n_chunks = tall // GDN_CHUNK
    ncp = ((n_chunks + 7) // 8) * 8
    grow = gb.reshape(bsz, 4, a_heads, n_chunks, GDN_CHUNK).transpose(0, 2, 1, 3, 4)
    grow = jnp.pad(grow, ((0, 0), (0, 0), (0, 0), (0, ncp - n_chunks), (0, 0)))
    ya = _gdn(qkvn, grow, sza, ev_gdn_norm_g[0].reshape(1, A_DV), a_heads, t)

    yb = _attn(qb, kb, vb, szb, t, 512)

    w_out0 = ev_w_out[0].astype(BF16)
    return _conformer(ya, yb, x, w_out0[:a_width], w_out0[a_width:], post_norm_g[0], mod[0], mod[1], pre_norm_g[1],
                      od_w_in[0].astype(BF16), od_b_in[0], od_dw_w[0], od_dw_b[0], od_ln_g[0], od_ln_b[0],
                      od_w_out[0].astype(BF16), od_b_out[0], post_norm_g[1], 512)
```

```python
import functools
import math

import numpy as np
import jax
import jax.numpy as jnp
from jax import lax
from jax.experimental import pallas as pl
from jax.experimental.pallas import tpu as pltpu

F32 = jnp.float32
BF16 = jnp.bfloat16
HIGHEST = lax.Precision.HIGHEST

GRID_W = 64
A_DK = 128
A_DV = 128
SHORT_CONV = 5
GDN_CHUNK = 64
INV_BLOCK = 16
B_HD = 128
ROPE_THETA = 10000.0
CONV_K = 31
NORM_EPS = 1e-6
LN_EPS = 1e-5
NEG_BIG = -1e30

VMEM_LIMIT = 56 * 1024 * 1024


def _sigmoid(x):
    return 1.0 / (1.0 + jnp.exp(-x))


def _silu(x):
    return x * _sigmoid(x)


def _softplus(x):
    return jnp.maximum(x, 0.0) + jnp.log(1.0 + jnp.exp(-jnp.abs(x)))


def _cparams(sem):
    return pltpu.CompilerParams(dimension_semantics=sem, vmem_limit_bytes=VMEM_LIMIT)


def _ada_kernel(c_ref, w_ref, b_ref, o_ref):
    s = _silu(c_ref[...])
    o_ref[0] = jnp.dot(s, w_ref[0], preferred_element_type=F32, precision=HIGHEST) + b_ref[0]


def _ada(cond, ada_w, ada_b):
    depth, d, d3 = ada_w.shape
    rows = cond.shape[0]
    nt = d3 // d
    return pl.pallas_call(
        _ada_kernel,
        out_shape=jax.ShapeDtypeStruct((depth, rows, d3), F32),
        grid=(depth, nt),
        in_specs=[
            pl.BlockSpec((rows, d), lambda l, j: (0, 0)),
            pl.BlockSpec((1, d, d), lambda l, j: (l, 0, j)),
            pl.BlockSpec((1, 1, d), lambda l, j: (l, 0, j)),
        ],
        out_specs=pl.BlockSpec((1, rows, d), lambda l, j: (l, 0, j)),
        compiler_params=_cparams(("parallel", "parallel")),
        name="ada",
    )(cond, ada_w, ada_b.reshape(depth, 1, d3))


def _inproj0_kernel(x_ref, xp_ref, xn_ref, ctx_ref, sh_ref, sc_ref, g_ref, w_ref, cw_ref, cs_ref, sn_ref, qg_ref,
                    kg_ref, alog_ref, dtb_ref,
                    qkv_o, za_o, qb_o, kb_o, vb_o, zb_o, gb_o, h_scr, *, n_lat_tiles, n_tiles, secs, halo,
                    n_gdn_heads):
    step_id = pl.program_id(0)
    wslot = step_id % 2
    rslot = 1 - wslot
    tm = x_ref.shape[1]
    last = pl.num_programs(0) - 2
    i = jnp.minimum(step_id, last) % n_tiles

    @pl.when(step_id == 0)
    def _():
        h_scr[...] = jnp.zeros_like(h_scr)

    def modulate_tile():
        mult = g_ref[...] * (1.0 + sc_ref[0])
        shift = sh_ref[0]

        def modulated(xf):
            rs = lax.rsqrt(jnp.mean(xf * xf, axis=-1, keepdims=True) + NORM_EPS)
            return xf * rs * mult + shift

        prev_ok = jnp.logical_and(i > 0, i < n_lat_tiles)
        next_ok = i < n_lat_tiles - 1
        h_scr[wslot, 0:halo, :] = jnp.where(prev_ok, modulated(xp_ref[0]), 0.0).astype(BF16)
        h_scr[wslot, halo + tm:2 * halo + tm, :] = jnp.where(next_ok, modulated(xn_ref[0]), 0.0).astype(BF16)
        src = jnp.where(i < n_lat_tiles, x_ref[0], ctx_ref[0])
        h_scr[wslot, halo:halo + tm, :] = modulated(src).astype(BF16)

    def proj(name):
        c0, c1 = secs[name]
        return lambda: jnp.dot(h_scr[rslot, halo:halo + tm, :], w_ref[:, c0:c1], preferred_element_type=F32)

    pad = SHORT_CONV // 2
    pw = 2 * A_DK
    hpb = pw // A_DK

    def qkv_mm(pb):
        c0 = secs["qkv"][0] + pb * pw
        return lambda: jnp.dot(h_scr[rslot], w_ref[:, c0:c0 + pw], preferred_element_type=F32)

    def qkv_epi(pb):
        def epi(y):
            for sb in range(hpb):
                hb = pb * hpb + sb
                cl = slice(hb * A_DK, (hb + 1) * A_DK)
                acc = None
                for j in range(SHORT_CONV):
                    term = y[halo - pad + j:halo - pad + j + tm, sb * A_DK:(sb + 1) * A_DK] * cw_ref[j:j + 1, cl]
                    acc = term if acc is None else acc + term
                u = _silu(acc)
                if hb < 2 * n_gdn_heads:
                    u = u * lax.rsqrt(jnp.sum(u * u, axis=-1, keepdims=True) + NORM_EPS)
                    if hb < n_gdn_heads:
                        u = u * (A_DK ** -0.5)
                qkv_o[0, :, cl] = u.astype(BF16)
        return epi

    def gate_epi(o_ref):
        def epi(y):
            o_ref[0] = _silu(y).astype(BF16)
        return epi

    def plain_epi(y):
        vb_o[0] = y.astype(BF16)

    def norm_rope_epi(o_ref, g_ref_, out_scale, transposed):
        def epi(y):
            cs = cs_ref[...]
            sn = sn_ref[...]
            g = g_ref_[...]
            for hh in range(y.shape[1] // B_HD):
                sl = slice(hh * B_HD, (hh + 1) * B_HD)
                yh = y[:, sl]
                ms = jnp.mean(yh * yh, axis=-1, keepdims=True)
                yn = yh * lax.rsqrt(ms + NORM_EPS) * g
                r = (yn * cs + pltpu.roll(yn, B_HD // 2, 1) * sn) * out_scale
                if transposed:
                    o_ref[0, hh] = r.T.astype(BF16)
                else:
                    o_ref[0, :, sl] = r.astype(BF16)
        return epi

    def decay_epi(ab):
        nab = gb_o.shape[1]
        gval = -jnp.exp(alog_ref[...]) * _softplus(ab + dtb_ref[...])
        bval = _sigmoid(ab)
        lane = lax.broadcasted_iota(jnp.int32, ab.shape, 1)
        gb_o[0] = jnp.where(lane < nab // 2, gval, bval).T[:nab, :]

    n_qkv_blocks = (secs["qkv"][1] - secs["qkv"][0]) // pw
    heavy = [(qkv_mm(pb), qkv_epi(pb)) for pb in range(n_qkv_blocks)]
    light = [(proj("za"), gate_epi(za_o)), (proj("zb"), gate_epi(zb_o)), (proj("vb"), plain_epi),
             (proj("qb"), norm_rope_epi(qb_o, qg_ref, B_HD ** -0.5, False)),
             (proj("kb"), norm_rope_epi(kb_o, kg_ref, 1.0, True)),
             (proj("ab"), decay_epi)]
    stages = []
    for k in range(max(len(heavy), len(light))):
        stages += heavy[k:k + 1] + light[k:k + 1]
    y_next = stages[0][0]()
    modulate_tile()
    for k, (_, epi) in enumerate(stages):
        y = y_next
        if k + 1 < len(stages):
            y_next = stages[k + 1][0]()
        epi(y)


def _inproj0(x, ctx, mod0, pre_g, w2, secs, conv_w, cs_tab, sn_tab, qg, kg, alog, dtb, tm, n_gdn_heads):
    bsz, t, d = x.shape
    nctx = ctx.shape[1]
    tall = t + nctx
    assert t % tm == 0 and nctx == tm
    n_lat = t // tm
    ntile = n_lat + 1
    nab = 16
    halo = 16
    nhb = tm // halo
    last_hblk = t // halo - 1

    def widths(name):
        return secs[name][1] - secs[name][0]

    n_steps = bsz * ntile

    def cur(s):
        p = jnp.minimum(s, n_steps - 1)
        return p // ntile, p % ntile

    def prev(s):
        p = jnp.maximum(s - 1, 0)
        return p // ntile, p % ntile

    def row_spec(c):
        return pl.BlockSpec((1, tm, c), lambda s: (prev(s)[0], prev(s)[1], 0))

    def mod_row(s):
        b, i = cur(s)
        return jnp.where(i < n_lat, b, ctx_row)

    mod_rows = mod0.shape[0]
    mod3 = mod0.reshape(mod_rows, 1, 3 * d)
    ctx_row = bsz
    outs = [
        jax.ShapeDtypeStruct((bsz, tall, widths("qkv")), BF16),
        jax.ShapeDtypeStruct((bsz, tall, widths("za")), BF16),
        jax.ShapeDtypeStruct((bsz, tall, widths("qb")), BF16),
        jax.ShapeDtypeStruct((bsz, widths("kb") // B_HD, B_HD, tall), BF16),
        jax.ShapeDtypeStruct((bsz, tall, widths("vb")), BF16),
        jax.ShapeDtypeStruct((bsz, tall, widths("zb")), BF16),
        jax.ShapeDtypeStruct((bsz, nab, tall), F32),
    ]
    kern = functools.partial(_inproj0_kernel, n_lat_tiles=n_lat, n_tiles=ntile, secs=secs, halo=halo,
                             n_gdn_heads=n_gdn_heads)
    return pl.pallas_call(
        kern,
        out_shape=outs,
        grid=(n_steps + 1,),
        in_specs=[
            pl.BlockSpec((1, tm, d), lambda s: (cur(s)[0], jnp.minimum(cur(s)[1], n_lat - 1), 0)),
            pl.BlockSpec((1, halo, d), lambda s: (cur(s)[0], jnp.clip(cur(s)[1] * nhb - 1, 0, last_hblk), 0)),
            pl.BlockSpec((1, halo, d), lambda s: (cur(s)[0], jnp.clip((cur(s)[1] + 1) * nhb, 0, last_hblk), 0)),
            pl.BlockSpec((1, tm, d), lambda s: (cur(s)[0], 0, 0)),
            pl.BlockSpec((1, 1, d), lambda s: (mod_row(s), 0, 0)),
            pl.BlockSpec((1, 1, d), lambda s: (mod_row(s), 0, 1)),
            pl.BlockSpec((1, d), lambda s: (0, 0)),
            pl.BlockSpec(w2.shape, lambda s: (0, 0)),
            pl.BlockSpec(conv_w.shape, lambda s: (0, 0)),
            pl.BlockSpec((tm, B_HD), lambda s: (prev(s)[1], 0)),
            pl.BlockSpec((tm, B_HD), lambda s: (prev(s)[1], 0)),
            pl.BlockSpec((1, B_HD), lambda s: (0, 0)),
            pl.BlockSpec((1, B_HD), lambda s: (0, 0)),
            pl.BlockSpec((1, 128), lambda s: (0, 0)),
            pl.BlockSpec((1, 128), lambda s: (0, 0)),
        ],
        out_specs=[row_spec(widths("qkv")), row_spec(widths("za")), row_spec(widths("qb")),
                   pl.BlockSpec((1, widths("kb") // B_HD, B_HD, tm), lambda s: (prev(s)[0], 0, 0, prev(s)[1])),
                   row_spec(widths("vb")), row_spec(widths("zb")),
                   pl.BlockSpec((1, nab, tm), lambda s: (prev(s)[0], 0, prev(s)[1]))],
        scratch_shapes=[pltpu.VMEM((2, tm + 2 * halo, d), BF16)],
        compiler_params=_cparams(("arbitrary",)),
        name="inproj0",
    )(x, x, x, ctx, mod3, mod3, pre_g.reshape(1, d), w2, conv_w, cs_tab, sn_tab, qg, kg, alog, dtb)


def _gdn_kernel(q_ref, k_ref, v_ref, gr_ref, sz_ref, ng_ref, o_ref,
                nq_scr, z_scr, au_scr, gt_scr, cr_scr, s_scr, oacc, *, n_lat_chunks, n_chunks, group):
    C = GDN_CHUNK
    step_id = pl.program_id(0)
    wset = step_id % 2
    rset = 1 - wset
    ri = lax.broadcasted_iota(jnp.int32, (C, C), 0)
    ci = lax.broadcasted_iota(jnp.int32, (C, C), 1)
    incl = (ri >= ci, ri <= ci)
    strict = (ri > ci, ri < ci)
    eye = (ri == ci).astype(F32)
    tri_row = ((ri <= ci).astype(F32), (ri >= ci).astype(F32))
    tri_col = (incl[0].astype(F32), incl[1].astype(F32))

    @pl.when(step_id == 0)
    def _():
        nq_scr[...] = jnp.zeros_like(nq_scr)
        z_scr[...] = jnp.zeros_like(z_scr)
        au_scr[...] = jnp.zeros_like(au_scr)
        gt_scr[...] = jnp.zeros_like(gt_scr)

    for d in range(2):
        cr_scr[d] = jnp.dot(gr_ref[0, 0, d], tri_row[d], preferred_element_type=F32, precision=HIGHEST)
    s_scr[...] = jnp.zeros_like(s_scr)
    oacc[...] = jnp.zeros_like(oacc)

    nt = (((1,), (1,)), ((), ()))
    nsq = int(math.log2(INV_BLOCK)) - 1
    diag_blk = (ri // INV_BLOCK) == (ci // INV_BLOCK)
    off_blks = []
    bs = INV_BLOCK
    while bs < C:
        off_blks.append(jnp.logical_and((ri // (2 * bs)) == (ci // (2 * bs)), (ri // bs) != (ci // bs)))
        bs *= 2

    def mm(a, b):
        return jnp.dot(a, b, preferred_element_type=F32)

    tl = (((0,), (0,)), ((), ()))

    def prep(chunk_ids):
        ch = []
        for n in chunk_ids:
            rows = pl.ds(pl.multiple_of(n * C, C), C)
            kb = k_ref[0, rows, :]
            qb = q_ref[0, rows, :]
            k32 = kb.astype(F32)
            q32 = qb.astype(F32)
            v32 = v_ref[0, rows, :].astype(F32)
            kk = lax.dot_general(kb, kb, nt, preferred_element_type=F32)
            qk = lax.dot_general(qb, kb, nt, preferred_element_type=F32)
            for d in range(2):
                g_row = gr_ref[0, 0, d, pl.ds(n, 1), :]
                beta_row = gr_ref[0, 0, 2 + d, pl.ds(n, 1), :]
                c_row = cr_scr[d, pl.ds(n, 1), :]
                c_col = jnp.sum(tri_col[d] * g_row, axis=-1, keepdims=True)
                beta_col = jnp.sum(eye * beta_row, axis=-1, keepdims=True)
                tot = jnp.sum(g_row, axis=-1, keepdims=True)
                decay = jnp.exp(jnp.where(incl[d], c_col - c_row, NEG_BIG))
                x = jnp.where(strict[d], kk * decay, 0.0) * (-beta_col)
                e_col = jnp.exp(c_col)
                rhs = jnp.concatenate([v32 * beta_col, k32 * (beta_col * e_col)], axis=1)
                qd = q32 * e_col
                kd = (k32 * jnp.exp(tot - c_col)).astype(BF16)
                am = jnp.where(incl[d], qk * decay, 0.0).astype(BF16)
                gt_scr[2 * wset + d, n] = jnp.broadcast_to(jnp.exp(tot), (8, 128))
                ch.append((d, n, rows, x, rhs, qd, kd, am))
        yield
        xs = [c[3] for c in ch]
        ys = [jnp.where(diag_blk, x, 0.0) for x in xs]
        xbs = [y.astype(BF16) for y in ys]
        ps = [mm(xb, xb) for xb in xbs]
        yield
        for m in range(nsq):
            pbs = [p.astype(BF16) for p in ps]
            ybs = [y.astype(BF16) for y in ys]
            if m < nsq - 1:
                rs = [mm(jnp.concatenate([pb, yb], axis=0), pb) for pb, yb in zip(pbs, ybs)]
                ys = [y + p + r[C:] for y, p, r in zip(ys, ps, rs)]
                ps = [r[:C] for r in rs]
            else:
                ys = [y + p + mm(yb, pb) for y, p, yb, pb in zip(ys, ps, ybs, pbs)]
            yield
        for off in off_blks:
            xos = [jnp.where(off, x, 0.0) for x in xs]
            ybs = [y.astype(BF16) for y in ys]
            ts = [xo + mm(yb, xo.astype(BF16)) for xo, yb in zip(xos, ybs)]
            yield
            ys = [y + t + mm(t.astype(BF16), yb) for y, t, yb in zip(ys, ts, ybs)]
            yield
        ybs = [y.astype(BF16) for y in ys]
        uws = [c[4] + mm(yb, c[4].astype(BF16)) for c, yb in zip(ch, ybs)]
        uwbs = [uw.astype(BF16) for uw in uws]
        yield
        kzs = [lax.dot_general(c[6], uwb, tl, preferred_element_type=F32) for c, uwb in zip(ch, uwbs)]
        azs = [mm(c[7], uwb) for c, uwb in zip(ch, uwbs)]
        for c, kz, az in zip(ch, kzs, azs):
            cd, n, rows = 2 * wset + c[0], c[1], c[2]
            z_scr[cd, n] = kz[:, :A_DV].astype(z_scr.dtype)
            nq_scr[cd, n, 0:A_DK, :] = kz[:, A_DV:].astype(BF16)
            nq_scr[cd, n, A_DK:A_DK + C, :] = (c[5] - az[:, A_DV:]).astype(BF16)
            au_scr[cd, rows, :] = az[:, :A_DV].astype(au_scr.dtype)
        yield

    def scan_step(ns, with_out):
        nrow = A_DK + C if with_out else A_DK
        ss = [s_scr[d] for d in range(2)]
        rs = [mm(nq_scr[2 * rset + d, ns[d], 0:nrow, :], ss[d].astype(BF16)) for d in range(2)]
        for d in range(2):
            cd, n = 2 * rset + d, ns[d]
            s_scr[d] = ss[d] * gt_scr[cd, n][0:1, :] - rs[d][:A_DK] + z_scr[cd, n].astype(F32)
            if with_out:
                rows = pl.ds(pl.multiple_of(n * C, C), C)
                o = oacc[rows, :] + rs[d][A_DK:] + au_scr[cd, rows, :].astype(F32)
                oacc[rows, :] = o
                on = o * lax.rsqrt(jnp.mean(o * o, axis=-1, keepdims=True) + NORM_EPS) * ng_ref[...]
                o_ref[0, rows, :] = (on * sz_ref[0, rows, :].astype(F32)).astype(BF16)

    def interleave(stages, scan_steps):
        pending = list(scan_steps)
        for _ in stages:
            if pending:
                pending.pop(0)()
        for rest in pending:
            rest()

    n_ctx_chunks = n_chunks - n_lat_chunks
    interleave(prep([n_lat_chunks + i for i in range(n_ctx_chunks)]),
               [functools.partial(scan_step, (n_lat_chunks + j, n_chunks - 1 - j), False)
                for j in range(n_ctx_chunks)])

    def body(gi, carry):
        base = gi * group
        interleave(prep([base + i for i in range(group)]),
                   [functools.partial(scan_step, (base + j, n_lat_chunks - 1 - base - j), True)
                    for j in range(group)])
        return carry

    lax.fori_loop(0, n_lat_chunks // group, body, 0)


def _gdn(qkvn, grow, sza, norm_g, n_heads, t):
    bsz, tall, _ = qkvn.shape
    C = GDN_CHUNK
    n_chunks = tall // C
    n_lat_chunks = t // C
    ncp = grow.shape[3]
    group = 8
    assert n_lat_chunks % group == 0
    kern = functools.partial(_gdn_kernel, n_lat_chunks=n_lat_chunks, n_chunks=n_chunks, group=group)
    n_pairs = bsz * n_heads

    def cur(s):
        p = jnp.minimum(s, n_pairs - 1)
        return p // n_heads, p % n_heads

    def prev(s):
        p = jnp.maximum(s - 1, 0)
        return p // n_heads, p % n_heads

    return pl.pallas_call(
        kern,
        out_shape=jax.ShapeDtypeStruct((bsz, t, n_heads * A_DV), BF16),
        grid=(n_pairs + 1,),
        in_specs=[
            pl.BlockSpec((1, tall, A_DK), lambda s: (cur(s)[0], 0, cur(s)[1])),
            pl.BlockSpec((1, tall, A_DK), lambda s: (cur(s)[0], 0, n_heads + cur(s)[1])),
            pl.BlockSpec((1, tall, A_DV), lambda s: (cur(s)[0], 0, 2 * n_heads + cur(s)[1])),
            pl.BlockSpec((1, 1, 4, ncp, C), lambda s: (cur(s)[0], cur(s)[1], 0, 0, 0)),
            pl.BlockSpec((1, t, A_DV), lambda s: (prev(s)[0], 0, prev(s)[1])),
            pl.BlockSpec((1, A_DV), lambda s: (0, 0)),
        ],
        out_specs=pl.BlockSpec((1, t, A_DV), lambda s: (prev(s)[0], 0, prev(s)[1])),
        scratch_shapes=[
            pltpu.VMEM((4, n_chunks, A_DK + C, A_DK), BF16),
            pltpu.VMEM((4, n_chunks, A_DK, A_DV), BF16),
            pltpu.VMEM((4, tall, A_DV), BF16),
            pltpu.VMEM((4, n_chunks, 8, 128), F32),
            pltpu.VMEM((2, ncp, C), F32),
            pltpu.VMEM((2, A_DK, A_DV), F32),
            pltpu.VMEM((t, A_DV), F32),
        ],
        compiler_params=_cparams(("arbitrary",)),
        name="gdn",
    )(qkvn, qkvn, qkvn, grow, sza, norm_g)


def _attn_kernel(q_ref, k_ref, v_ref, sz_ref, o_ref, *, kv_blocks):
    n_heads = q_ref.shape[2] // B_HD
    items = [(hh, blk) for hh in range(n_heads) for blk in kv_blocks]

    def scores(item):
        hh, (k0, k1) = item
        q = q_ref[0, :, hh * B_HD:(hh + 1) * B_HD]
        return jnp.dot(q, k_ref[0, 0, :, k0:k1], preferred_element_type=F32)

    s_next = scores(items[0])
    m = l = acc = None
    for idx, (hh, (k0, k1)) in enumerate(items):
        s = s_next
        if idx + 1 < len(items):
            s_next = scores(items[idx + 1])
        bm = jnp.max(s, axis=-1, keepdims=True)
        if k0 == kv_blocks[0][0]:
            m = bm
            p = jnp.exp(s - m)
            l = jnp.sum(p, axis=-1, keepdims=True)
            acc = jnp.dot(p.astype(BF16), v_ref[0, k0:k1, :], preferred_element_type=F32)
        else:
            m_new = jnp.maximum(m, bm)
            alpha = jnp.exp(m - m_new)
            p = jnp.exp(s - m_new)
            l = alpha * l + jnp.sum(p, axis=-1, keepdims=True)
            acc = alpha * acc + jnp.dot(p.astype(BF16), v_ref[0, k0:k1, :], preferred_element_type=F32)
            m = m_new
        if k1 == kv_blocks[-1][1]:
            sl = slice(hh * B_HD, (hh + 1) * B_HD)
            o_ref[0, :, sl] = (acc * (1.0 / l) * sz_ref[0, :, sl].astype(F32)).astype(BF16)


def _attn(qb, kbt, vb, szb, t, tq):
    bsz, tall, qw = qb.shape
    n_kv = kbt.shape[1]
    gw = qw // n_kv
    kvb = 1024
    edges = list(range(0, tall, kvb)) + [tall]
    kv_blocks = tuple((edges[j], edges[j + 1]) for j in range(len(edges) - 1))
    return pl.pallas_call(
        functools.partial(_attn_kernel, kv_blocks=kv_blocks),
        out_shape=jax.ShapeDtypeStruct((bsz, t, qw), BF16),
        grid=(bsz, n_kv, t // tq),
        in_specs=[
            pl.BlockSpec((1, tq, gw), lambda b, g, i: (b, i, g)),
            pl.BlockSpec((1, 1, B_HD, tall), lambda b, g, i: (b, g, 0, 0)),
            pl.BlockSpec((1, tall, B_HD), lambda b, g, i: (b, 0, g)),
            pl.BlockSpec((1, tq, gw), lambda b, g, i: (b, i, g)),
        ],
        out_specs=pl.BlockSpec((1, tq, gw), lambda b, g, i: (b, i, g)),
        compiler_params=_cparams(("parallel", "parallel", "arbitrary")),
        name="attn",
    )(qb, kbt, vb, szb)


def _conformer_kernel(ya_ref, yb_ref, x_ref, wa_ref, wb_ref, pg0_ref, gate0_ref, sh_ref, sc_ref, g1_ref, w1_ref,
                      b1_ref, dw_ref, dwb_ref, lng_ref, lnb_ref, wo_ref, bo_ref, pg1_ref, gate1_ref,
                      o_ref, h_scr, u_ring, sz_ring, xl_ring, buf, act, *, tiles_per_seq, halo):
    step_id = pl.program_id(0)
    d = x_ref.shape[2]
    tm = x_ref.shape[1]
    nsub = 2
    ts = tm // nsub
    subs = [slice(k * ts, (k + 1) * ts) for k in range(nsub)]
    u_cur, u_prev, u_pprev = step_id % 3, (step_id + 2) % 3, (step_id + 1) % 3
    r_cur, r_prev = step_id % 2, (step_id + 1) % 2
    ip = jnp.maximum(step_id - 1, 0) % tiles_per_seq

    @pl.when(step_id == 0)
    def _():
        u_ring[...] = jnp.zeros_like(u_ring)
        sz_ring[...] = jnp.zeros_like(sz_ring)
        xl_ring[...] = jnp.zeros_like(xl_ring)

    pad = CONV_K // 2
    cw = 128
    sub = 8
    buf[0:halo, :] = jnp.where(ip > 0, u_ring[u_pprev, tm - halo:tm, :].astype(F32), 0.0)
    buf[halo:halo + tm, :] = u_ring[u_prev].astype(F32)

    def conv_piece(r0, nr, cb):
        cs = slice(cb * cw, (cb + 1) * cw)
        acc = None
        for s in range(sub):
            part = None
            for a in range((halo + pad) // sub + 1):
                j = sub * a + s - (halo - pad)
                if 0 <= j < CONV_K:
                    term = buf[r0 + sub * a:r0 + sub * a + nr + sub, cs] * dw_ref[j:j + 1, cs]
                    part = term if part is None else part + term
            if part is not None:
                acc = part[s:s + nr] if acc is None else acc + part[s:s + nr]
        act[r0:r0 + nr, cs] = acc + dwb_ref[:, cs]

    def activate(r):
        u = act[r, :]
        mu = jnp.mean(u, axis=-1, keepdims=True)
        uc = u - mu
        var = jnp.mean(uc * uc, axis=-1, keepdims=True)
        un = uc * lax.rsqrt(var + LN_EPS) * lng_ref[...] + lnb_ref[...]
        hact = (_silu(un) * sz_ring[r_prev, r, :].astype(F32)).astype(BF16)
        return jnp.dot(hact, wo_ref[...], preferred_element_type=F32) + bo_ref[...]

    def finish(r, out):
        on = out * lax.rsqrt(jnp.mean(out * out, axis=-1, keepdims=True) + NORM_EPS) * pg1_ref[...]
        o_ref[0, r, :] = xl_ring[r_prev, r, :] + gate1_ref[0] * on

    assert nsub == 2 and ts + 2 * halo <= tm + halo
    ncb = d // cw
    first_half = [functools.partial(conv_piece, 0, ts, cb) for cb in range(ncb)]
    second_half = [functools.partial(conv_piece, ts, ts, cb) for cb in range(ncb)]

    def run(pieces, n):
        for _ in range(min(n, len(pieces))):
            pieces.pop(0)()

    mult1 = g1_ref[...] * (1.0 + sc_ref[0])

    def out_proj(r):
        return (jnp.dot(ya_ref[0, r, :], wa_ref[...], preferred_element_type=F32)
                + jnp.dot(yb_ref[0, r, :], wb_ref[...], preferred_element_type=F32))

    def residual_and_modulate(r, out):
        on = out * lax.rsqrt(jnp.mean(out * out, axis=-1, keepdims=True) + NORM_EPS) * pg0_ref[...]
        xl = x_ref[0, r, :] + gate0_ref[0] * on
        xl_ring[r_cur, r, :] = xl
        rs = lax.rsqrt(jnp.mean(xl * xl, axis=-1, keepdims=True) + NORM_EPS)
        h_scr[r, :] = (xl * rs * mult1 + sh_ref[0]).astype(BF16)

    def in_mm(r, part):
        return (jnp.dot(h_scr[r, :], w1_ref[:, part * d:(part + 1) * d], preferred_element_type=F32)
                + b1_ref[:, part * d:(part + 1) * d])

    def glu_store(r, a, gl):
        u_ring[u_cur, r, :] = (a * _sigmoid(gl)).astype(BF16)

    def gate_store(r, z):
        sz_ring[r_cur, r, :] = _silu(z).astype(BF16)

    s0, s1 = subs
    o0 = out_proj(s0)
    o1 = out_proj(s1)
    run(first_half, 2)
    residual_and_modulate(s0, o0)
    a0 = in_mm(s0, 0)
    run(first_half, 2)
    gl0 = in_mm(s0, 1)
    run(first_half, 2)
    z0 = in_mm(s0, 2)
    glu_store(s0, a0, gl0)
    gate_store(s0, z0)
    residual_and_modulate(s1, o1)
    run(first_half, ncb)
    a1 = in_mm(s1, 0)
    buf[halo + tm:2 * halo + tm, :] = jnp.where(ip < tiles_per_seq - 1, u_ring[u_cur, 0:halo, :].astype(F32), 0.0)
    run(second_half, 3)
    gl1 = in_mm(s1, 1)
    run(second_half, 3)
    z1 = in_mm(s1, 2)
    glu_store(s1, a1, gl1)
    gate_store(s1, z1)
    run(second_half, ncb)
    out0 = activate(s0)
    out1 = activate(s1)
    finish(s0, out0)
    finish(s1, out1)


def _conformer(ya, yb, x, wa, wb, post_g0, mod0, mod1, pre_g1, w1, b1, dw_w, dw_b, ln_g, ln_b, w_out, b_out,
               post_g1, tm):
    bsz, t, d = x.shape
    aw = ya.shape[2]
    bw = yb.shape[2]
    halo = 16
    assert halo >= CONV_K // 2 and t % tm == 0
    tiles = t // tm
    n_steps = bsz * tiles
    rows = mod0.shape[0]
    m0 = mod0.reshape(rows, 1, 3 * d)
    m1 = mod1.reshape(rows, 1, 3 * d)

    def cur(s):
        p = jnp.minimum(s, n_steps - 1)
        return p // tiles, p % tiles

    def prev(s):
        p = jnp.maximum(s - 1, 0)
        return p // tiles, p % tiles

    once = pl.Buffered(1)
    tok = lambda c: pl.BlockSpec((1, tm, c), lambda s: (cur(s)[0], cur(s)[1], 0))
    vec = lambda: pl.BlockSpec((1, d), lambda s: (0, 0))
    const = lambda a: pl.BlockSpec(a.shape, lambda s: (0, 0), pipeline_mode=once)
    kern = functools.partial(_conformer_kernel, tiles_per_seq=tiles, halo=halo)
    return pl.pallas_call(
        kern,
        out_shape=jax.ShapeDtypeStruct((bsz, t, d), F32),
        grid=(n_steps + 1,),
        in_specs=[
            tok(aw), tok(bw), tok(d),
            const(wa), const(wb),
            vec(),
            pl.BlockSpec((1, 1, d), lambda s: (cur(s)[0], 0, 2)),
            pl.BlockSpec((1, 1, d), lambda s: (cur(s)[0], 0, 0)),
            pl.BlockSpec((1, 1, d), lambda s: (cur(s)[0], 0, 1)),
            vec(),
            const(w1),
            pl.BlockSpec((1, 3 * d), lambda s: (0, 0)),
            pl.BlockSpec(dw_w.shape, lambda s: (0, 0)),
            vec(), vec(), vec(),
            const(w_out),
            vec(), vec(),
            pl.BlockSpec((1, 1, d), lambda s: (prev(s)[0], 0, 2)),
        ],
        out_specs=pl.BlockSpec((1, tm, d), lambda s: (prev(s)[0], prev(s)[1], 0)),
        scratch_shapes=[
            pltpu.VMEM((tm, d), BF16),
            pltpu.VMEM((3, tm, d), BF16),
            pltpu.VMEM((2, tm, d), BF16),
            pltpu.VMEM((2, tm, d), F32),
            pltpu.VMEM((tm + 2 * halo, d), F32),
            pltpu.VMEM((tm, d), F32),
        ],
        compiler_params=_cparams(("arbitrary",)),
        name="conformer",
    )(ya, yb, x, wa, wb, post_g0.reshape(1, d), m0, m1, m1, pre_g1.reshape(1, d), w1, b1.reshape(1, 3 * d),
      dw_w, dw_b.reshape(1, d), ln_g.reshape(1, d), ln_b.reshape(1, d), w_out, b_out.reshape(1, d),
      post_g1.reshape(1, d), m1)


def _rope_tables(t, nctx):
    f32 = np.float32
    rows = t // GRID_W
    row = np.repeat(np.arange(rows, dtype=f32), GRID_W)
    col = np.tile(np.arange(GRID_W, dtype=f32), rows)
    axis_dim = B_HD // 2
    inv_freq = (f32(ROPE_THETA) ** (-np.arange(0, axis_dim, 2, dtype=f32) / f32(axis_dim))).astype(f32)
    ang = np.concatenate([row[:, None] * inv_freq, col[:, None] * inv_freq], axis=1).astype(f32)
    cs = np.concatenate([np.cos(ang), np.cos(ang)], axis=1)
    sn = np.concatenate([-np.sin(ang), np.sin(ang)], axis=1)
    cs = np.concatenate([cs, np.ones((nctx, B_HD), f32)], axis=0).astype(f32)
    sn = np.concatenate([sn, np.zeros((nctx, B_HD), f32)], axis=0).astype(f32)
    return jnp.asarray(cs), jnp.asarray(sn)


def _head_perm():
    q = B_HD // 4
    return np.concatenate([np.arange(0, q), np.arange(2 * q, 3 * q), np.arange(q, 2 * q), np.arange(3 * q, 4 * q)])


def kernel(x, c, ctx, c_ctx, ada_w, ada_b, pre_norm_g, post_norm_g, ev_w_in, ev_short_conv_w, ev_a_log,
           ev_dt_bias, ev_gdn_norm_g, ev_q_norm_g, ev_k_norm_g, ev_w_out, od_w_in, od_b_in, od_dw_w, od_dw_b,
           od_ln_g, od_ln_b, od_w_out, od_b_out):
    bsz, t, d = x.shape
    nctx = ctx.shape[1]
    a_width = d // 2
    a_heads = a_width // A_DV
    a_qkv = a_heads * (2 * A_DK + A_DV)
    b_width = d - a_width
    b_heads = b_width // B_HD
    b_kv = b_heads // 2
    assert ada_w.shape[0] == 2 and ev_w_in.shape[0] == 1 and od_w_in.shape[0] == 1

    mod_rows = ((bsz + 1 + 7) // 8) * 8
    cond = jnp.zeros((mod_rows, d), F32).at[:bsz].set(c).at[bsz].set(c_ctx)
    mod = _ada(cond, ada_w, ada_b)

    w_in = ev_w_in[0]
    splits = np.cumsum([0, a_qkv, a_width, 2 * a_heads, 2 * a_heads, b_heads * B_HD, b_kv * B_HD, b_kv * B_HD,
                        b_width])
    w_qkv, w_za, w_a, w_b, w_qb, w_kb, w_vb, w_zb = [w_in[:, splits[j]:splits[j + 1]] for j in range(8)]
    perm = _head_perm()
    perm_q = np.concatenate([h * B_HD + perm for h in range(b_heads)])
    perm_k = np.concatenate([h * B_HD + perm for h in range(b_kv)])
    ab_pad = 128 - 4 * a_heads
    pieces = [("qkv", w_qkv), ("za", w_za), ("qb", w_qb[:, perm_q]), ("kb", w_kb[:, perm_k]), ("vb", w_vb),
              ("zb", w_zb), ("ab", jnp.concatenate([w_a, w_b, jnp.zeros((d, ab_pad), F32)], axis=1))]
    secs, off = {}, 0
    for name, wpart in pieces:
        secs[name] = (off, off + wpart.shape[1])
        off += wpart.shape[1]
    w2 = jnp.concatenate([p[1].astype(BF16) for p in pieces], axis=1)
    cs_tab, sn_tab = _rope_tables(t, nctx)
    qg = ev_q_norm_g[0][perm].reshape(1, B_HD)
    kg = ev_k_norm_g[0][perm].reshape(1, B_HD)
    alog = jnp.zeros((1, 128), F32).at[0, :2 * a_heads].set(ev_a_log[0].reshape(-1))
    dtb = jnp.zeros((1, 128), F32).at[0, :2 * a_heads].set(ev_dt_bias[0].reshape(-1))

    tm0 = nctx
    qkvn, sza, qb, kb, vb, szb, gb = _inproj0(x, ctx, mod[0], pre_norm_g[0], w2, secs, ev_short_conv_w[0], cs_tab,
                                              sn_tab, qg, kg, alog, dtb, tm0, a_heads)

    tall = t + nctx
    n_chunks = tall // GDN_CHUNK
    ncp = ((n_chunks + 7) // 8) * 8
    grow = gb.reshape(bsz, 4, a_heads, n_chunks, GDN_CHUNK).transpose(0, 2, 1, 3, 4)
    grow = jnp.pad(grow, ((0, 0), (0, 0), (0, 0), (0, ncp - n_chunks), (0, 0)))
    ya = _gdn(qkvn, grow, sza, ev_gdn_norm_g[0].reshape(1, A_DV), a_heads, t)

    yb = _attn(qb, kb, vb, szb, t, 512)

    w_out0 = ev_w_out[0].astype(BF16)
    return _conformer(ya, yb, x, w_out0[:a_width], w_out0[a_width:], post_norm_g[0], mod[0], mod[1], pre_norm_g[1],
                      od_w_in[0].astype(BF16), od_b_in[0], od_dw_w[0], od_dw_b[0], od_ln_g[0], od_ln_b[0],
                      od_w_out[0].astype(BF16), od_b_out[0], post_norm_g[1], 512)
```

```python
import functools
import math

import numpy as np
import jax
import jax.numpy as jnp
from jax import lax
from jax.experimental import pallas as pl
from jax.experimental.pallas import tpu as pltpu

F32 = jnp.float32
BF16 = jnp.bfloat16

GRID_W = 64
A_DK = 128
A_DV = 128
SHORT_CONV = 5
GDN_CHUNK = 64
INV_BLOCK = 16
B_HD = 128
ROPE_THETA = 10000.0
CONV_K = 31
NORM_EPS = 1e-6
LN_EPS = 1e-5
NEG_BIG = -1e30

VMEM_LIMIT = 56 * 1024 * 1024


def _sigmoid(x):
    return 1.0 / (1.0 + jnp.exp(-x))


def _silu(x):
    return x * _sigmoid(x)


def _softplus(x):
    return jnp.maximum(x, 0.0) + jnp.log(1.0 + jnp.exp(-jnp.abs(x)))


def _cparams(sem):
    return pltpu.CompilerParams(dimension_semantics=sem, vmem_limit_bytes=VMEM_LIMIT)


def _split_bf16(a, parts):
    out, r = [], a
    for _ in range(parts):
        p = r.astype(BF16)
        out.append(p)
        r = r - p.astype(F32)
    return out


def _ada_kernel(c_ref, w_ref, b_ref, o_ref):
    sh, sl = _split_bf16(_silu(c_ref[...]), 2)
    wh, wl = _split_bf16(w_ref[0], 2)

    def mm(a, b):
        return jnp.dot(a, b, preferred_element_type=F32)

    o_ref[0] = mm(sh, wh) + (mm(sl, wh) + mm(sh, wl)) + b_ref[0]


def _ada(cond, ada_w, ada_b):
    depth, d, d3 = ada_w.shape
    rows = cond.shape[0]
    nt = d3 // d
    return pl.pallas_call(
        _ada_kernel,
        out_shape=jax.ShapeDtypeStruct((depth, rows, d3), F32),
        grid=(depth, nt),
        in_specs=[
            pl.BlockSpec((rows, d), lambda l, j: (0, 0)),
            pl.BlockSpec((1, d, d), lambda l, j: (l, 0, j)),
            pl.BlockSpec((1, 1, d), lambda l, j: (l, 0, j)),
        ],
        out_specs=pl.BlockSpec((1, rows, d), lambda l, j: (l, 0, j)),
        compiler_params=_cparams(("parallel", "parallel")),
        name="ada",
    )(cond, ada_w, ada_b.reshape(depth, 1, d3))


def _inproj0_kernel(x_ref, xp_ref, xn_ref, ctx_ref, sh_ref, sc_ref, g_ref, w_ref, cw_ref, cs_ref, sn_ref, qg_ref,
                    kg_ref, alog_ref, dtb_ref,
                    qkv_o, za_o, qb_o, kb_o, vb_o, zb_o, gb_o, h_scr, *, n_lat_tiles, n_tiles, secs, halo,
                    n_gdn_heads):
    step_id = pl.program_id(0)
    wslot = step_id % 2
    rslot = 1 - wslot
    tm = x_ref.shape[1]
    last = pl.num_programs(0) - 2
    i = jnp.minimum(step_id, last) % n_tiles

    @pl.when(step_id == 0)
    def _():
        h_scr[...] = jnp.zeros_like(h_scr)

    def modulate_tile():
        mult = g_ref[...] * (1.0 + sc_ref[0])
        shift = sh_ref[0]

        def modulated(xf):
            rs = lax.rsqrt(jnp.mean(xf * xf, axis=-1, keepdims=True) + NORM_EPS)
            return xf * rs * mult + shift

        prev_ok = jnp.logical_and(i > 0, i < n_lat_tiles)
        next_ok = i < n_lat_tiles - 1
        h_scr[wslot, 0:halo, :] = jnp.where(prev_ok, modulated(xp_ref[0]), 0.0).astype(BF16)
        h_scr[wslot, halo + tm:2 * halo + tm, :] = jnp.where(next_ok, modulated(xn_ref[0]), 0.0).astype(BF16)
        src = jnp.where(i < n_lat_tiles, x_ref[0], ctx_ref[0])
        h_scr[wslot, halo:halo + tm, :] = modulated(src).astype(BF16)

    def proj(name):
        c0, c1 = secs[name]
        return lambda: jnp.dot(h_scr[rslot, halo:halo + tm, :], w_ref[:, c0:c1], preferred_element_type=F32)

    pad = SHORT_CONV // 2
    pw = 2 * A_DK
    hpb = pw // A_DK

    def qkv_mm(pb):
        c0 = secs["qkv"][0] + pb * pw
        return lambda: jnp.dot(h_scr[rslot], w_ref[:, c0:c0 + pw], preferred_element_type=F32)

    def qkv_epi(pb):
        def epi(y):
            for sb in range(hpb):
                hb = pb * hpb + sb
                cl = slice(hb * A_DK, (hb + 1) * A_DK)
                acc = None
                for j in range(SHORT_CONV):
                    term = y[halo - pad + j:halo - pad + j + tm, sb * A_DK:(sb + 1) * A_DK] * cw_ref[j:j + 1, cl]
                    acc = term if acc is None else acc + term
                u = _silu(acc)
                if hb < 2 * n_gdn_heads:
                    u = u * lax.rsqrt(jnp.sum(u * u, axis=-1, keepdims=True) + NORM_EPS)
                    if hb < n_gdn_heads:
                        u = u * (A_DK ** -0.5)
                qkv_o[0, :, cl] = u.astype(BF16)
        return epi

    def gate_epi(o_ref):
        def epi(y):
            o_ref[0] = _silu(y).astype(BF16)
        return epi

    def plain_epi(y):
        vb_o[0] = y.astype(BF16)

    def norm_rope_epi(o_ref, g_ref_, out_scale):
        def epi(y):
            cs = cs_ref[...]
            sn = sn_ref[...]
            g = g_ref_[...]
            for hh in range(y.shape[1] // B_HD):
                sl = slice(hh * B_HD, (hh + 1) * B_HD)
                yh = y[:, sl]
                ms = jnp.mean(yh * yh, axis=-1, keepdims=True)
                yn = yh * lax.rsqrt(ms + NORM_EPS) * g
                o_ref[0, :, sl] = ((yn * cs + pltpu.roll(yn, B_HD // 2, 1) * sn) * out_scale).astype(BF16)
        return epi

    def decay_epi(ab):
        nab = gb_o.shape[1]
        gval = -jnp.exp(alog_ref[...]) * _softplus(ab + dtb_ref[...])
        bval = _sigmoid(ab)
        lane = lax.broadcasted_iota(jnp.int32, ab.shape, 1)
        gb_o[0] = jnp.where(lane < nab // 2, gval, bval).T[:nab, :]

    n_qkv_blocks = (secs["qkv"][1] - secs["qkv"][0]) // pw
    heavy = [(qkv_mm(pb), qkv_epi(pb)) for pb in range(n_qkv_blocks)]
    light = [(proj("za"), gate_epi(za_o)), (proj("zb"), gate_epi(zb_o)), (proj("vb"), plain_epi),
             (proj("qb"), norm_rope_epi(qb_o, qg_ref, B_HD ** -0.5)), (proj("kb"), norm_rope_epi(kb_o, kg_ref, 1.0)),
             (proj("ab"), decay_epi)]
    stages = []
    for k in range(max(len(heavy), len(light))):
        stages += heavy[k:k + 1] + light[k:k + 1]
    y_next = stages[0][0]()
    modulate_tile()
    for k, (_, epi) in enumerate(stages):
        y = y_next
        if k + 1 < len(stages):
            y_next = stages[k + 1][0]()
        epi(y)


def _inproj0(x, ctx, mod0, pre_g, w2, secs, conv_w, cs_tab, sn_tab, qg, kg, alog, dtb, tm, n_gdn_heads):
    bsz, t, d = x.shape
    nctx = ctx.shape[1]
    tall = t + nctx
    assert t % tm == 0 and nctx == tm
    n_lat = t // tm
    ntile = n_lat + 1
    nab = 16
    halo = 16
    nhb = tm // halo
    last_hblk = t // halo - 1

    def widths(name):
        return secs[name][1] - secs[name][0]

    n_steps = bsz * ntile

    def cur(s):
        p = jnp.minimum(s, n_steps - 1)
        return p // ntile, p % ntile

    def prev(s):
        p = jnp.maximum(s - 1, 0)
        return p // ntile, p % ntile

    def row_spec(c):
        return pl.BlockSpec((1, tm, c), lambda s: (prev(s)[0], prev(s)[1], 0))

    def mod_row(s):
        b, i = cur(s)
        return jnp.where(i < n_lat, b, ctx_row)

    mod_rows = mod0.shape[0]
    mod3 = mod0.reshape(mod_rows, 1, 3 * d)
    ctx_row = bsz
    outs = [
        jax.ShapeDtypeStruct((bsz, tall, widths("qkv")), BF16),
        jax.ShapeDtypeStruct((bsz, tall, widths("za")), BF16),
        jax.ShapeDtypeStruct((bsz, tall, widths("qb")), BF16),
        jax.ShapeDtypeStruct((bsz, tall, widths("kb")), BF16),
        jax.ShapeDtypeStruct((bsz, tall, widths("vb")), BF16),
        jax.ShapeDtypeStruct((bsz, tall, widths("zb")), BF16),
        jax.ShapeDtypeStruct((bsz, nab, tall), F32),
    ]
    kern = functools.partial(_inproj0_kernel, n_lat_tiles=n_lat, n_tiles=ntile, secs=secs, halo=halo,
                             n_gdn_heads=n_gdn_heads)
    return pl.pallas_call(
        kern,
        out_shape=outs,
        grid=(n_steps + 1,),
        in_specs=[
            pl.BlockSpec((1, tm, d), lambda s: (cur(s)[0], jnp.minimum(cur(s)[1], n_lat - 1), 0)),
            pl.BlockSpec((1, halo, d), lambda s: (cur(s)[0], jnp.clip(cur(s)[1] * nhb - 1, 0, last_hblk), 0)),
            pl.BlockSpec((1, halo, d), lambda s: (cur(s)[0], jnp.clip((cur(s)[1] + 1) * nhb, 0, last_hblk), 0)),
            pl.BlockSpec((1, tm, d), lambda s: (cur(s)[0], 0, 0)),
            pl.BlockSpec((1, 1, d), lambda s: (mod_row(s), 0, 0)),
            pl.BlockSpec((1, 1, d), lambda s: (mod_row(s), 0, 1)),
            pl.BlockSpec((1, d), lambda s: (0, 0)),
            pl.BlockSpec(w2.shape, lambda s: (0, 0)),
            pl.BlockSpec(conv_w.shape, lambda s: (0, 0)),
            pl.BlockSpec((tm, B_HD), lambda s: (prev(s)[1], 0)),
            pl.BlockSpec((tm, B_HD), lambda s: (prev(s)[1], 0)),
            pl.BlockSpec((1, B_HD), lambda s: (0, 0)),
            pl.BlockSpec((1, B_HD), lambda s: (0, 0)),
            pl.BlockSpec((1, 128), lambda s: (0, 0)),
            pl.BlockSpec((1, 128), lambda s: (0, 0)),
        ],
        out_specs=[row_spec(widths("qkv")), row_spec(widths("za")), row_spec(widths("qb")), row_spec(widths("kb")),
                   row_spec(widths("vb")), row_spec(widths("zb")),
                   pl.BlockSpec((1, nab, tm), lambda s: (prev(s)[0], 0, prev(s)[1]))],
        scratch_shapes=[pltpu.VMEM((2, tm + 2 * halo, d), BF16)],
        compiler_params=_cparams(("arbitrary",)),
        name="inproj0",
    )(x, x, x, ctx, mod3, mod3, pre_g.reshape(1, d), w2, conv_w, cs_tab, sn_tab, qg, kg, alog, dtb)


def _gdn_kernel(q_ref, k_ref, v_ref, gr_ref, sz_ref, ng_ref, o_ref,
                nq_scr, z_scr, au_scr, gt_scr, cr_scr, s_scr, oacc, *, n_lat_chunks, n_chunks, group):
    C = GDN_CHUNK
    step_id = pl.program_id(0)
    wset = step_id % 2
    rset = 1 - wset
    ri = lax.broadcasted_iota(jnp.int32, (C, C), 0)
    ci = lax.broadcasted_iota(jnp.int32, (C, C), 1)
    incl = (ri >= ci, ri <= ci)
    strict = (ri > ci, ri < ci)
    eye = (ri == ci).astype(F32)
    tri_row = ((ri <= ci).astype(F32), (ri >= ci).astype(F32))
    tri_col = (incl[0].astype(F32), incl[1].astype(F32))

    @pl.when(step_id == 0)
    def _():
        nq_scr[...] = jnp.zeros_like(nq_scr)
        z_scr[...] = jnp.zeros_like(z_scr)
        au_scr[...] = jnp.zeros_like(au_scr)
        gt_scr[...] = jnp.zeros_like(gt_scr)

    for d in range(2):
        tri_b = tri_row[d].astype(BF16)
        cr_scr[d] = sum(jnp.dot(p, tri_b, preferred_element_type=F32) for p in _split_bf16(gr_ref[0, 0, d], 3))
    s_scr[...] = jnp.zeros_like(s_scr)
    oacc[...] = jnp.zeros_like(oacc)

    nt = (((1,), (1,)), ((), ()))
    nsq = int(math.log2(INV_BLOCK)) - 1
    diag_blk = (ri // INV_BLOCK) == (ci // INV_BLOCK)
    off_blks = []
    bs = INV_BLOCK
    while bs < C:
        off_blks.append(jnp.logical_and((ri // (2 * bs)) == (ci // (2 * bs)), (ri // bs) != (ci // bs)))
        bs *= 2

    def mm(a, b):
        return jnp.dot(a, b, preferred_element_type=F32)

    tl = (((0,), (0,)), ((), ()))

    def prep(chunk_ids):
        ch = []
        for n in chunk_ids:
            rows = pl.ds(pl.multiple_of(n * C, C), C)
            kb = k_ref[0, rows, :]
            qb = q_ref[0, rows, :]
            k32 = kb.astype(F32)
            q32 = qb.astype(F32)
            v32 = v_ref[0, rows, :].astype(F32)
            kk = lax.dot_general(kb, kb, nt, preferred_element_type=F32)
            qk = lax.dot_general(qb, kb, nt, preferred_element_type=F32)
            for d in range(2):
                g_row = gr_ref[0, 0, d, pl.ds(n, 1), :]
                beta_row = gr_ref[0, 0, 2 + d, pl.ds(n, 1), :]
                c_row = cr_scr[d, pl.ds(n, 1), :]
                c_col = jnp.sum(tri_col[d] * g_row, axis=-1, keepdims=True)
                beta_col = jnp.sum(eye * beta_row, axis=-1, keepdims=True)
                tot = jnp.sum(g_row, axis=-1, keepdims=True)
                decay = jnp.exp(jnp.where(incl[d], c_col - c_row, NEG_BIG))
                x = jnp.where(strict[d], kk * decay, 0.0) * (-beta_col)
                e_col = jnp.exp(c_col)
                rhs = jnp.concatenate([v32 * beta_col, k32 * (beta_col * e_col)], axis=1)
                qd = q32 * e_col
                kd = (k32 * jnp.exp(tot - c_col)).astype(BF16)
                am = jnp.where(incl[d], qk * decay, 0.0).astype(BF16)
                gt_scr[2 * wset + d, n] = jnp.broadcast_to(jnp.exp(tot), (8, 128))
                ch.append((d, n, rows, x, rhs, qd, kd, am))
        yield
        xs = [c[3] for c in ch]
        ys = [jnp.where(diag_blk, x, 0.0) for x in xs]
        xbs = [y.astype(BF16) for y in ys]
        ps = [mm(xb, xb) for xb in xbs]
        yield
        for m in range(nsq):
            pbs = [p.astype(BF16) for p in ps]
            ybs = [y.astype(BF16) for y in ys]
            if m < nsq - 1:
                rs = [mm(jnp.concatenate([pb, yb], axis=0), pb) for pb, yb in zip(pbs, ybs)]
                ys = [y + p + r[C:] for y, p, r in zip(ys, ps, rs)]
                ps = [r[:C] for r in rs]
            else:
                ys = [y + p + mm(yb, pb) for y, p, yb, pb in zip(ys, ps, ybs, pbs)]
            yield
        for off in off_blks:
            xos = [jnp.where(off, x, 0.0) for x in xs]
            ybs = [y.astype(BF16) for y in ys]
            ts = [xo + mm(yb, xo.astype(BF16)) for xo, yb in zip(xos, ybs)]
            yield
            ys = [y + t + mm(t.astype(BF16), yb) for y, t, yb in zip(ys, ts, ybs)]
            yield
        ybs = [y.astype(BF16) for y in ys]
        uws = [c[4] + mm(yb, c[4].astype(BF16)) for c, yb in zip(ch, ybs)]
        uwbs = [uw.astype(BF16) for uw in uws]
        yield
        kzs = [lax.dot_general(c[6], uwb, tl, preferred_element_type=F32) for c, uwb in zip(ch, uwbs)]
        azs = [mm(c[7], uwb) for c, uwb in zip(ch, uwbs)]
        for c, kz, az in zip(ch, kzs, azs):
            cd, n, rows = 2 * wset + c[0], c[1], c[2]
            z_scr[cd, n] = kz[:, :A_DV].astype(z_scr.dtype)
            nq_scr[cd, n, 0:A_DK, :] = kz[:, A_DV:].astype(BF16)
            nq_scr[cd, n, A_DK:A_DK + C, :] = (c[5] - az[:, A_DV:]).astype(BF16)
            au_scr[cd, rows, :] = az[:, :A_DV].astype(au_scr.dtype)
        yield

    def scan_step(ns, with_out):
        nrow = A_DK + C if with_out else A_DK
        ss = [s_scr[d] for d in range(2)]
        rs = [mm(nq_scr[2 * rset + d, ns[d], 0:nrow, :], ss[d].astype(BF16)) for d in range(2)]
        for d in range(2):
            cd, n = 2 * rset + d, ns[d]
            s_scr[d] = ss[d] * gt_scr[cd, n][0:1, :] - rs[d][:A_DK] + z_scr[cd, n].astype(F32)
            if with_out:
                rows = pl.ds(pl.multiple_of(n * C, C), C)
                o = oacc[rows, :] + rs[d][A_DK:] + au_scr[cd, rows, :].astype(F32)
                oacc[rows, :] = o
                on = o * lax.rsqrt(jnp.mean(o * o, axis=-1, keepdims=True) + NORM_EPS) * ng_ref[...]
                o_ref[0, rows, :] = (on * sz_ref[0, rows, :].astype(F32)).astype(BF16)

    def interleave(stages, scan_steps):
        pending = list(scan_steps)
        for _ in stages:
            if pending:
                pending.pop(0)()
        for rest in pending:
            rest()

    n_ctx_chunks = n_chunks - n_lat_chunks
    interleave(prep([n_lat_chunks + i for i in range(n_ctx_chunks)]),
               [functools.partial(scan_step, (n_lat_chunks + j, n_chunks - 1 - j), False)
                for j in range(n_ctx_chunks)])

    def body(gi, carry):
        base = gi * group
        interleave(prep([base + i for i in range(group)]),
                   [functools.partial(scan_step, (base + j, n_lat_chunks - 1 - base - j), True)
                    for j in range(group)])
        return carry

    lax.fori_loop(0, n_lat_chunks // group, body, 0)


def _gdn(qkvn, grow, sza, norm_g, n_heads, t):
    bsz, tall, _ = qkvn.shape
    C = GDN_CHUNK
    n_chunks = tall // C
    n_lat_chunks = t // C
    ncp = grow.shape[3]
    group = 8
    assert n_lat_chunks % group == 0
    kern = functools.partial(_gdn_kernel, n_lat_chunks=n_lat_chunks, n_chunks=n_chunks, group=group)
    n_pairs = bsz * n_heads

    def cur(s):
        p = jnp.minimum(s, n_pairs - 1)
        return p // n_heads, p % n_heads

    def prev(s):
        p = jnp.maximum(s - 1, 0)
        return p // n_heads, p % n_heads

    return pl.pallas_call(
        kern,
        out_shape=jax.ShapeDtypeStruct((bsz, t, n_heads * A_DV), BF16),
        grid=(n_pairs + 1,),
        in_specs=[
            pl.BlockSpec((1, tall, A_DK), lambda s: (cur(s)[0], 0, cur(s)[1])),
            pl.BlockSpec((1, tall, A_DK), lambda s: (cur(s)[0], 0, n_heads + cur(s)[1])),
            pl.BlockSpec((1, tall, A_DV), lambda s: (cur(s)[0], 0, 2 * n_heads + cur(s)[1])),
            pl.BlockSpec((1, 1, 4, ncp, C), lambda s: (cur(s)[0], cur(s)[1], 0, 0, 0)),
            pl.BlockSpec((1, t, A_DV), lambda s: (prev(s)[0], 0, prev(s)[1])),
            pl.BlockSpec((1, A_DV), lambda s: (0, 0)),
        ],
        out_specs=pl.BlockSpec((1, t, A_DV), lambda s: (prev(s)[0], 0, prev(s)[1])),
        scratch_shapes=[
            pltpu.VMEM((4, n_chunks, A_DK + C, A_DK), BF16),
            pltpu.VMEM((4, n_chunks, A_DK, A_DV), BF16),
            pltpu.VMEM((4, tall, A_DV), BF16),
            pltpu.VMEM((4, n_chunks, 8, 128), F32),
            pltpu.VMEM((2, ncp, C), F32),
            pltpu.VMEM((2, A_DK, A_DV), F32),
            pltpu.VMEM((t, A_DV), F32),
        ],
        compiler_params=_cparams(("arbitrary",)),
        name="gdn",
    )(qkvn, qkvn, qkvn, grow, sza, norm_g)


def _attn_kernel(q_ref, k_ref, v_ref, sz_ref, o_ref, *, kv_blocks):
    nt = (((1,), (1,)), ((), ()))
    n_heads = q_ref.shape[2] // B_HD
    items = [(hh, blk) for hh in range(n_heads) for blk in kv_blocks]

    def scores(item):
        hh, (k0, k1) = item
        q = q_ref[0, :, hh * B_HD:(hh + 1) * B_HD]
        return lax.dot_general(q, k_ref[0, k0:k1, :], nt, preferred_element_type=F32)

    s_next = scores(items[0])
    m = l = acc = None
    for idx, (hh, (k0, k1)) in enumerate(items):
        s = s_next
        if idx + 1 < len(items):
            s_next = scores(items[idx + 1])
        bm = jnp.max(s, axis=-1, keepdims=True)
        if k0 == kv_blocks[0][0]:
            m = bm
            p = jnp.exp(s - m)
            l = jnp.sum(p, axis=-1, keepdims=True)
            acc = jnp.dot(p.astype(BF16), v_ref[0, k0:k1, :], preferred_element_type=F32)
        else:
            m_new = jnp.maximum(m, bm)
            alpha = jnp.exp(m - m_new)
            p = jnp.exp(s - m_new)
            l = alpha * l + jnp.sum(p, axis=-1, keepdims=True)
            acc = alpha * acc + jnp.dot(p.astype(BF16), v_ref[0, k0:k1, :], preferred_element_type=F32)
            m = m_new
        if k1 == kv_blocks[-1][1]:
            sl = slice(hh * B_HD, (hh + 1) * B_HD)
            o_ref[0, :, sl] = (acc * (1.0 / l) * sz_ref[0, :, sl].astype(F32)).astype(BF16)


def _attn(qb, kb, vb, szb, t, tq):
    bsz, tall, qw = qb.shape
    n_kv = kb.shape[2] // B_HD
    gw = qw // n_kv
    kvb = 1024
    edges = list(range(0, tall, kvb)) + [tall]
    kv_blocks = tuple((edges[j], edges[j + 1]) for j in range(len(edges) - 1))
    return pl.pallas_call(
        functools.partial(_attn_kernel, kv_blocks=kv_blocks),
        out_shape=jax.ShapeDtypeStruct((bsz, t, qw), BF16),
        grid=(bsz, n_kv, t // tq),
        in_specs=[
            pl.BlockSpec((1, tq, gw), lambda b, g, i: (b, i, g)),
            pl.BlockSpec((1, tall, B_HD), lambda b, g, i: (b, 0, g)),
            pl.BlockSpec((1, tall, B_HD), lambda b, g, i: (b, 0, g)),
            pl.BlockSpec((1, tq, gw), lambda b, g, i: (b, i, g)),
        ],
        out_specs=pl.BlockSpec((1, tq, gw), lambda b, g, i: (b, i, g)),
        compiler_params=_cparams(("parallel", "parallel", "arbitrary")),
        name="attn",
    )(qb, kb, vb, szb)


def _conformer_kernel(ya_ref, yb_ref, x_ref, wa_ref, wb_ref, pg0_ref, gate0_ref, sh_ref, sc_ref, g1_ref, w1_ref,
                      b1_ref, dw_ref, dwb_ref, lng_ref, lnb_ref, wo_ref, bo_ref, pg1_ref, gate1_ref,
                      o_ref, h_scr, u_ring, sz_ring, xl_ring, buf, act, *, tiles_per_seq, halo):
    step_id = pl.program_id(0)
    d = x_ref.shape[2]
    tm = x_ref.shape[1]
    nsub = 2
    ts = tm // nsub
    subs = [slice(k * ts, (k + 1) * ts) for k in range(nsub)]
    u_cur, u_prev, u_pprev = step_id % 3, (step_id + 2) % 3, (step_id + 1) % 3
    r_cur, r_prev = step_id % 2, (step_id + 1) % 2
    ip = jnp.maximum(step_id - 1, 0) % tiles_per_seq

    @pl.when(step_id == 0)
    def _():
        u_ring[...] = jnp.zeros_like(u_ring)
        sz_ring[...] = jnp.zeros_like(sz_ring)
        xl_ring[...] = jnp.zeros_like(xl_ring)

    pad = CONV_K // 2
    cw = 128
    sub = 8
    buf[0:halo, :] = jnp.where(ip > 0, u_ring[u_pprev, tm - halo:tm, :].astype(F32), 0.0)
    buf[halo:halo + tm, :] = u_ring[u_prev].astype(F32)

    def conv_piece(r0, nr, cb):
        cs = slice(cb * cw, (cb + 1) * cw)
        acc = None
        for s in range(sub):
            part = None
            for a in range((halo + pad) // sub + 1):
                j = sub * a + s - (halo - pad)
                if 0 <= j < CONV_K:
                    term = buf[r0 + sub * a:r0 + sub * a + nr + sub, cs] * dw_ref[j:j + 1, cs]
                    part = term if part is None else part + term
            if part is not None:
                acc = part[s:s + nr] if acc is None else acc + part[s:s + nr]
        act[r0:r0 + nr, cs] = acc + dwb_ref[:, cs]

    def activate(r):
        u = act[r, :]
        mu = jnp.mean(u, axis=-1, keepdims=True)
        uc = u - mu
        var = jnp.mean(uc * uc, axis=-1, keepdims=True)
        un = uc * lax.rsqrt(var + LN_EPS) * lng_ref[...] + lnb_ref[...]
        hact = (_silu(un) * sz_ring[r_prev, r, :].astype(F32)).astype(BF16)
        return jnp.dot(hact, wo_ref[...], preferred_element_type=F32) + bo_ref[...]

    def finish(r, out):
        on = out * lax.rsqrt(jnp.mean(out * out, axis=-1, keepdims=True) + NORM_EPS) * pg1_ref[...]
        o_ref[0, r, :] = xl_ring[r_prev, r, :] + gate1_ref[0] * on

    assert nsub == 2 and ts + 2 * halo <= tm + halo
    ncb = d // cw
    first_half = [functools.partial(conv_piece, 0, ts, cb) for cb in range(ncb)]
    second_half = [functools.partial(conv_piece, ts, ts, cb) for cb in range(ncb)]

    def run(pieces, n):
        for _ in range(min(n, len(pieces))):
            pieces.pop(0)()

    mult1 = g1_ref[...] * (1.0 + sc_ref[0])

    def out_proj(r):
        return (jnp.dot(ya_ref[0, r, :], wa_ref[...], preferred_element_type=F32)
                + jnp.dot(yb_ref[0, r, :], wb_ref[...], preferred_element_type=F32))

    def residual_and_modulate(r, out):
        on = out * lax.rsqrt(jnp.mean(out * out, axis=-1, keepdims=True) + NORM_EPS) * pg0_ref[...]
        xl = x_ref[0, r, :] + gate0_ref[0] * on
        xl_ring[r_cur, r, :] = xl
        rs = lax.rsqrt(jnp.mean(xl * xl, axis=-1, keepdims=True) + NORM_EPS)
        h_scr[r, :] = (xl * rs * mult1 + sh_ref[0]).astype(BF16)

    def in_mm(r, part):
        return (jnp.dot(h_scr[r, :], w1_ref[:, part * d:(part + 1) * d], preferred_element_type=F32)
                + b1_ref[:, part * d:(part + 1) * d])

    def glu_store(r, a, gl):
        u_ring[u_cur, r, :] = (a * _sigmoid(gl)).astype(BF16)

    def gate_store(r, z):
        sz_ring[r_cur, r, :] = _silu(z).astype(BF16)

    s0, s1 = subs
    o0 = out_proj(s0)
    o1 = out_proj(s1)
    run(first_half, 2)
    residual_and_modulate(s0, o0)
    a0 = in_mm(s0, 0)
    run(first_half, 2)
    gl0 = in_mm(s0, 1)
    run(first_half, 2)
    z0 = in_mm(s0, 2)
    glu_store(s0, a0, gl0)
    gate_store(s0, z0)
    residual_and_modulate(s1, o1)
    run(first_half, ncb)
    a1 = in_mm(s1, 0)
    buf[halo + tm:2 * halo + tm, :] = jnp.where(ip < tiles_per_seq - 1, u_ring[u_cur, 0:halo, :].astype(F32), 0.0)
    run(second_half, 3)
    gl1 = in_mm(s1, 1)
    run(second_half, 3)
    z1 = in_mm(s1, 2)
    glu_store(s1, a1, gl1)
    gate_store(s1, z1)
    run(second_half, ncb)
    out0 = activate(s0)
    out1 = activate(s1)
    finish(s0, out0)
    finish(s1, out1)


def _conformer(ya, yb, x, wa, wb, post_g0, mod0, mod1, pre_g1, w1, b1, dw_w, dw_b, ln_g, ln_b, w_out, b_out,
               post_g1, tm):
    bsz, t, d = x.shape
    aw = ya.shape[2]
    bw = yb.shape[2]
    halo = 16
    assert halo >= CONV_K // 2 and t % tm == 0
    tiles = t // tm
    n_steps = bsz * tiles
    rows = mod0.shape[0]
    m0 = mod0.reshape(rows, 1, 3 * d)
    m1 = mod1.reshape(rows, 1, 3 * d)

    def cur(s):
        p = jnp.minimum(s, n_steps - 1)
        return p // tiles, p % tiles

    def prev(s):
        p = jnp.maximum(s - 1, 0)
        return p // tiles, p % tiles

    once = pl.Buffered(1)
    tok = lambda c: pl.BlockSpec((1, tm, c), lambda s: (cur(s)[0], cur(s)[1], 0))
    vec = lambda: pl.BlockSpec((1, d), lambda s: (0, 0))
    const = lambda a: pl.BlockSpec(a.shape, lambda s: (0, 0), pipeline_mode=once)
    kern = functools.partial(_conformer_kernel, tiles_per_seq=tiles, halo=halo)
    return pl.pallas_call(
        kern,
        out_shape=jax.ShapeDtypeStruct((bsz, t, d), F32),
        grid=(n_steps + 1,),
        in_specs=[
            tok(aw), tok(bw), tok(d),
            const(wa), const(wb),
            vec(),
            pl.BlockSpec((1, 1, d), lambda s: (cur(s)[0], 0, 2)),
            pl.BlockSpec((1, 1, d), lambda s: (cur(s)[0], 0, 0)),
            pl.BlockSpec((1, 1, d), lambda s: (cur(s)[0], 0, 1)),
            vec(),
            const(w1),
            pl.BlockSpec((1, 3 * d), lambda s: (0, 0)),
            pl.BlockSpec(dw_w.shape, lambda s: (0, 0)),
            vec(), vec(), vec(),
            const(w_out),
            vec(), vec(),
            pl.BlockSpec((1, 1, d), lambda s: (prev(s)[0], 0, 2)),
        ],
        out_specs=pl.BlockSpec((1, tm, d), lambda s: (prev(s)[0], prev(s)[1], 0)),
        scratch_shapes=[
            pltpu.VMEM((tm, d), BF16),
            pltpu.VMEM((3, tm, d), BF16),
            pltpu.VMEM((2, tm, d), BF16),
            pltpu.VMEM((2, tm, d), F32),
            pltpu.VMEM((tm + 2 * halo, d), F32),
            pltpu.VMEM((tm, d), F32),
        ],
        compiler_params=_cparams(("arbitrary",)),
        name="conformer",
    )(ya, yb, x, wa, wb, post_g0.reshape(1, d), m0, m1, m1, pre_g1.reshape(1, d), w1, b1.reshape(1, 3 * d),
      dw_w, dw_b.reshape(1, d), ln_g.reshape(1, d), ln_b.reshape(1, d), w_out, b_out.reshape(1, d),
      post_g1.reshape(1, d), m1)


def _rope_tables(t, nctx):
    f32 = np.float32
    rows = t // GRID_W
    row = np.repeat(np.arange(rows, dtype=f32), GRID_W)
    col = np.tile(np.arange(GRID_W, dtype=f32), rows)
    axis_dim = B_HD // 2
    inv_freq = (f32(ROPE_THETA) ** (-np.arange(0, axis_dim, 2, dtype=f32) / f32(axis_dim))).astype(f32)
    ang = np.concatenate([row[:, None] * inv_freq, col[:, None] * inv_freq], axis=1).astype(f32)
    cs = np.concatenate([np.cos(ang), np.cos(ang)], axis=1)
    sn = np.concatenate([-np.sin(ang), np.sin(ang)], axis=1)
    cs = np.concatenate([cs, np.ones((nctx, B_HD), f32)], axis=0).astype(f32)
    sn = np.concatenate([sn, np.zeros((nctx, B_HD), f32)], axis=0).astype(f32)
    return jnp.asarray(cs), jnp.asarray(sn)


def _head_perm():
    q = B_HD // 4
    return np.concatenate([np.arange(0, q), np.arange(2 * q, 3 * q), np.arange(q, 2 * q), np.arange(3 * q, 4 * q)])


def kernel(x, c, ctx, c_ctx, ada_w, ada_b, pre_norm_g, post_norm_g, ev_w_in, ev_short_conv_w, ev_a_log,
           ev_dt_bias, ev_gdn_norm_g, ev_q_norm_g, ev_k_norm_g, ev_w_out, od_w_in, od_b_in, od_dw_w, od_dw_b,
           od_ln_g, od_ln_b, od_w_out, od_b_out):
    bsz, t, d = x.shape
    nctx = ctx.shape[1]
    a_width = d // 2
    a_heads = a_width // A_DV
    a_qkv = a_heads * (2 * A_DK + A_DV)
    b_width = d - a_width
    b_heads = b_width // B_HD
    b_kv = b_heads // 2
    assert ada_w.shape[0] == 2 and ev_w_in.shape[0] == 1 and od_w_in.shape[0] == 1

    mod_rows = ((bsz + 1 + 7) // 8) * 8
    cond = jnp.zeros((mod_rows, d), F32).at[:bsz].set(c).at[bsz].set(c_ctx)
    mod = _ada(cond, ada_w, ada_b)

    w_in = ev_w_in[0]
    splits = np.cumsum([0, a_qkv, a_width, 2 * a_heads, 2 * a_heads, b_heads * B_HD, b_kv * B_HD, b_kv * B_HD,
                        b_width])
    w_qkv, w_za, w_a, w_b, w_qb, w_kb, w_vb, w_zb = [w_in[:, splits[j]:splits[j + 1]] for j in range(8)]
    perm = _head_perm()
    perm_q = np.concatenate([h * B_HD + perm for h in range(b_heads)])
    perm_k = np.concatenate([h * B_HD + perm for h in range(b_kv)])
    ab_pad = 128 - 4 * a_heads
    pieces = [("qkv", w_qkv), ("za", w_za), ("qb", w_qb[:, perm_q]), ("kb", w_kb[:, perm_k]), ("vb", w_vb),
              ("zb", w_zb), ("ab", jnp.concatenate([w_a, w_b, jnp.zeros((d, ab_pad), F32)], axis=1))]
    secs, off = {}, 0
    for name, wpart in pieces:
        secs[name] = (off, off + wpart.shape[1])
        off += wpart.shape[1]
    w2 = jnp.concatenate([p[1].astype(BF16) for p in pieces], axis=1)
    cs_tab, sn_tab = _rope_tables(t, nctx)
    qg = ev_q_norm_g[0][perm].reshape(1, B_HD)
    kg = ev_k_norm_g[0][perm].reshape(1, B_HD)
    alog = jnp.zeros((1, 128), F32).at[0, :2 * a_heads].set(ev_a_log[0].reshape(-1))
    dtb = jnp.zeros((1, 128), F32).at[0, :2 * a_heads].set(ev_dt_bias[0].reshape(-1))

    tm0 = nctx
    qkvn, sza, qb, kb, vb, szb, gb = _inproj0(x, ctx, mod[0], pre_norm_g[0], w2, secs, ev_short_conv_w[0], cs_tab,
                                              sn_tab, qg, kg, alog, dtb, tm0, a_heads)

    tall = t + nctx
    n_chunks = tall // GDN_CHUNK
    ncp = ((n_chunks + 7) // 8) * 8
    grow = gb.reshape(bsz, 4, a_heads, n_chunks, GDN_CHUNK).transpose(0, 2, 1, 3, 4)
    grow = jnp.pad(grow, ((0, 0), (0, 0), (0, 0), (0, ncp - n_chunks), (0, 0)))
    ya = _gdn(qkvn, grow, sza, ev_gdn_norm_g[0].reshape(1, A_DV), a_heads, t)

    yb = _attn(qb, kb, vb, szb, t, 512)

    w_out0 = ev_w_out[0].astype(BF16)
    return _conformer(ya, yb, x, w_out0[:a_width], w_out0[a_width:], post_norm_g[0], mod[0], mod[1], pre_norm_g[1],
                      od_w_in[0].astype(BF16), od_b_in[0], od_dw_w[0], od_dw_b[0], od_ln_g[0], od_ln_b[0],
                      od_w_out[0].astype(BF16), od_b_out[0], post_norm_g[1], 512)
```
